```python
import jax, jax.numpy as jnp
from jax import lax
import numpy as np

D_MODEL = 4096
BATCH = 1
SEQ = 16384
DEPTH = 1
DEC_BATCH = 8
DEC_SEQ = 16
PAST_LEN = 2048

CHUNK = 64
HEAD_DIM = 128
GDN_QK_HEADS = 8
GDN_V_HEADS = 16
GDN_QK_WIDTH = GDN_QK_HEADS * HEAD_DIM
GDN_V_WIDTH = GDN_V_HEADS * HEAD_DIM
CONV_WIDTH = 4
CONV_CH = 2 * GDN_QK_WIDTH + GDN_V_WIDTH
SB_HEADS = 16
SB_WIDTH = SB_HEADS * HEAD_DIM
SB_BLOCK = 128
SB_SEG = 2048
D_FF = 4 * D_MODEL
EPS = 1e-6
FAR_POS = 2 ** 30

OFF_Z = CONV_CH
OFF_A = OFF_Z + GDN_V_WIDTH
OFF_B = OFF_A + GDN_V_HEADS
OFF_SB = OFF_B + GDN_V_HEADS
OFF_GATE = OFF_SB + 3 * SB_WIDTH
IN_WIDTH = OFF_GATE + 2 * D_MODEL

kernel_name = "hybrid_gdn_stickbreaking_stream_step"


def rmsnorm(x, w):
    xf = x.astype(jnp.float32)
    var = jnp.mean(xf * xf, axis=-1, keepdims=True)
    return (xf * lax.rsqrt(var + EPS) * w.astype(jnp.float32)).astype(x.dtype)


def l2norm(t):
    return t * lax.rsqrt(jnp.sum(t * t, axis=-1, keepdims=True) + EPS)


def causal_conv_silu(x, past, w):
    L = x.shape[1]
    xp = jnp.concatenate([past.astype(x.dtype), x], axis=1)
    out = w[0] * xp[:, 0:L]
    for i in range(1, CONV_WIDTH):
        out = out + w[i] * xp[:, i:i + L]
    return jax.nn.silu(out), xp[:, -(CONV_WIDTH - 1):]


def _to_blocks(t, chunk):
    B, L, H = t.shape[:3]
    t = t.reshape((B, L // chunk, chunk, H) + t.shape[3:])
    return jnp.moveaxis(t, (1, 3), (0, 2))


def gated_delta_rule(q, k, v, g, beta, S0, chunk):
    B, L, H, dk = k.shape
    dv = v.shape[-1]
    qc, kc, vc = _to_blocks(q, chunk), _to_blocks(k, chunk), _to_blocks(v, chunk)
    gc = jnp.cumsum(_to_blocks(g, chunk), axis=-1)
    bc = _to_blocks(beta, chunk)
    idx = jnp.arange(chunk)
    causal = idx[:, None] >= idx[None, :]
    strict = idx[:, None] > idx[None, :]
    diff = gc[..., :, None] - gc[..., None, :]
    decay = jnp.where(causal, jnp.exp(jnp.where(causal, diff, 0.0)), 0.0)
    kk = jnp.einsum('nbhid,nbhjd->nbhij', kc, kc)
    tri = jnp.eye(chunk, dtype=jnp.float32) + jnp.where(strict, bc[..., :, None] * kk * decay, 0.0)
    rhs = jnp.concatenate([vc * bc[..., None], kc * (bc * jnp.exp(gc))[..., None]], axis=-1)
    sol = lax.linalg.triangular_solve(tri, rhs, left_side=True, lower=True, unit_diagonal=True)
    u0, w = sol[..., :dv], sol[..., dv:]
    aqk = jnp.einsum('nbhid,nbhjd->nbhij', qc, kc) * decay
    qg = qc * jnp.exp(gc)[..., None]
    g_last = gc[..., -1]
    kd = kc * jnp.exp(g_last[..., None] - gc)[..., None]

    def step(S, xs):
        u0_, w_, aqk_, qg_, kd_, gl_ = xs
        u = u0_ - jnp.einsum('bhcd,bhde->bhce', w_, S)
        o = jnp.einsum('bhcd,bhde->bhce', qg_, S) + jnp.einsum('bhij,bhje->bhie', aqk_, u)
        S = jnp.exp(gl_)[..., None, None] * S + jnp.einsum('bhcd,bhce->bhde', kd_, u)
        return S, o

    S, o = lax.scan(step, S0, (u0, w, aqk, qg, kd, g_last))
    o = jnp.moveaxis(o, (0, 2), (1, 3)).reshape(B, L, H, dv)
    return o, S


def stick_breaking(q, k, v, q_pos, k_pos):
    B, K, H, d = k.shape
    pad = (-K) % SB_BLOCK
    if pad:
        k = jnp.pad(k, ((0, 0), (0, pad), (0, 0), (0, 0)))
        v = jnp.pad(v, ((0, 0), (0, pad), (0, 0), (0, 0)))
        k_pos = jnp.concatenate([k_pos, jnp.full((pad,), FAR_POS, jnp.int32)])
    nk = (K + pad) // SB_BLOCK
    kb = k.reshape(B, nk, SB_BLOCK, H, d).astype(jnp.float32)
    vb = v.reshape(B, nk, SB_BLOCK, H, d).astype(jnp.float32)
    z = jnp.einsum('bqhd,bnkhd->bhqnk', q.astype(jnp.float32), kb) * (HEAD_DIM ** -0.5)
    mask = k_pos.reshape(nk, SB_BLOCK)[None] < q_pos[:, None, None]
    log_fail = jnp.where(mask, jax.nn.log_sigmoid(-z), 0.0)
    idx = jnp.arange(SB_BLOCK)
    upper = (idx[:, None] >= idx[None, :]).astype(jnp.float32)
    within = jnp.einsum('bhqnj,js->bhqns', log_fail, upper)
    totals = jnp.sum(log_fail, axis=-1)
    later = lax.cumsum(totals, axis=3, reverse=True) - totals
    tail = within + later[..., None] - log_fail
    att = jnp.where(mask, jnp.exp(jax.nn.log_sigmoid(z) + tail), 0.0)
    return jnp.einsum('bhqnk,bnkhd->bqhd', att, vb)


def stick_breaking_prompt(q, k, v):
    B, L, H, d = q.shape
    outs = []
    for s0 in range(0, L, SB_SEG):
        ls = min(SB_SEG, L - s0)
        nb = ls // SB_BLOCK
        kk, vv = k[:, :s0 + ls], v[:, :s0 + ls]
        k_pos = jnp.arange(s0 + ls, dtype=jnp.int32)
        qb = jnp.moveaxis(q[:, s0:s0 + ls].reshape(B, nb, SB_BLOCK, H, d), 1, 0)

        def one_block(args, kk=kk, vv=vv, k_pos=k_pos, s0=s0):
            q_blk, i = args
            q_pos = s0 + i * SB_BLOCK + jnp.arange(SB_BLOCK, dtype=jnp.int32)
            return stick_breaking(q_blk, kk, vv, q_pos, k_pos)

        o = lax.map(one_block, (qb, jnp.arange(nb, dtype=jnp.int32)))
        outs.append(jnp.moveaxis(o, 0, 1).reshape(B, ls, H * d))
    return jnp.concatenate(outs, axis=1)


def layer(x, conv_past, S0, past_k, past_v, chunk,
          norm1_w, w_in, conv_w, A_log, dt_bias, gdn_norm_w, w_gdn_o, w_sb_o, w_out,
          norm2_w, w_up, w_down):
    B, L, _ = x.shape
    xn = rmsnorm(x, norm1_w)
    proj = xn @ w_in
    qkv_raw, z, a, b, sb_qkv, gates = jnp.split(proj, [OFF_Z, OFF_A, OFF_B, OFF_SB, OFF_GATE], axis=-1)

    qkv, conv_state = causal_conv_silu(qkv_raw, conv_past, conv_w)
    qkv = qkv.astype(jnp.float32)
    q, k, v = jnp.split(qkv, [GDN_QK_WIDTH, 2 * GDN_QK_WIDTH], axis=-1)
    rep = GDN_V_HEADS // GDN_QK_HEADS
    q = jnp.repeat(l2norm(q.reshape(B, L, GDN_QK_HEADS, HEAD_DIM)) * (HEAD_DIM ** -0.5), rep, axis=2)
    k = jnp.repeat(l2norm(k.reshape(B, L, GDN_QK_HEADS, HEAD_DIM)), rep, axis=2)
    v = v.reshape(B, L, GDN_V_HEADS, HEAD_DIM)
    g = -jnp.exp(A_log.astype(jnp.float32)) * jax.nn.softplus(a.astype(jnp.float32) + dt_bias.astype(jnp.float32))
    beta = jax.nn.sigmoid(b.astype(jnp.float32))
    o_a, S = gated_delta_rule(q, k, v, g, beta, S0.astype(jnp.float32), chunk)
    o_a = rmsnorm(o_a, gdn_norm_w) * jax.nn.silu(z.astype(jnp.float32).reshape(B, L, GDN_V_HEADS, HEAD_DIM))
    y_a = o_a.reshape(B, L, GDN_V_WIDTH).astype(x.dtype) @ w_gdn_o

    q_sb, k_sb, v_sb = [t.reshape(B, L, SB_HEADS, HEAD_DIM) for t in jnp.split(sb_qkv, 3, axis=-1)]
    if past_k is None:
        o_b = stick_breaking_prompt(q_sb, k_sb, v_sb)
    else:
        P = past_k.shape[1]
        keys = jnp.concatenate([past_k.astype(x.dtype), k_sb], axis=1)
        vals = jnp.concatenate([past_v.astype(x.dtype), v_sb], axis=1)
        q_pos = P + jnp.arange(L, dtype=jnp.int32)
        k_pos = jnp.arange(P + L, dtype=jnp.int32)
        o_b = stick_breaking(q_sb, keys, vals, q_pos, k_pos).reshape(B, L, SB_WIDTH)
    y_b = o_b.astype(x.dtype) @ w_sb_o

    g_a, g_b = jnp.split(jax.nn.sigmoid(gates), 2, axis=-1)
    h = x + (g_a * y_a + g_b * y_b) @ w_out

    h = h + jnp.square(jax.nn.relu(rmsnorm(h, norm2_w) @ w_up)) @ w_down
    return h, conv_state, S.astype(S0.dtype), k_sb, v_sb


def setup_inputs(seed: int = 0) -> dict:
    key = jax.random.key(seed)
    ks = jax.random.split(key, 24)

    def nrm(k, shape, scale):
        return jax.random.normal(k, shape, jnp.float32) * scale

    dt = jnp.exp(jax.random.uniform(ks[9], (DEPTH, GDN_V_HEADS), jnp.float32,
                                    np.log(1e-3).astype(np.float32), np.log(1e-1).astype(np.float32)))
    return {
        "x_prompt": nrm(ks[0], (BATCH, SEQ, D_MODEL), 1.0),
        "x_sample": nrm(ks[1], (DEC_BATCH, DEC_SEQ, D_MODEL), 1.0),
        "cache_sb_k": nrm(ks[2], (DEPTH, DEC_BATCH, PAST_LEN, SB_HEADS, HEAD_DIM), 1.0),
        "cache_sb_v": nrm(ks[3], (DEPTH, DEC_BATCH, PAST_LEN, SB_HEADS, HEAD_DIM), 1.0),
        "state_gdn_S": nrm(ks[4], (DEPTH, DEC_BATCH, GDN_V_HEADS, HEAD_DIM, HEAD_DIM), 0.1),
        "state_gdn_conv": nrm(ks[5], (DEPTH, DEC_BATCH, CONV_WIDTH - 1, CONV_CH), 1.0),
        "norm1_w": 1.0 + nrm(ks[6], (DEPTH, D_MODEL), 0.01),
        "w_in": nrm(ks[7], (DEPTH, D_MODEL, IN_WIDTH), D_MODEL ** -0.5),
        "conv_w": nrm(ks[8], (DEPTH, CONV_WIDTH, CONV_CH), CONV_WIDTH ** -0.5),
        "A_log": jnp.log(jax.random.uniform(ks[10], (DEPTH, GDN_V_HEADS), jnp.float32, 1.0, 16.0)),
        "dt_bias": dt + jnp.log(-jnp.expm1(-dt)),
        "gdn_norm_w": 1.0 + nrm(ks[11], (DEPTH, HEAD_DIM), 0.01),
        "w_gdn_o": nrm(ks[12], (DEPTH, GDN_V_WIDTH, D_MODEL), GDN_V_WIDTH ** -0.5),
        "w_sb_o": nrm(ks[13], (DEPTH, SB_WIDTH, D_MODEL), SB_WIDTH ** -0.5),
        "w_out": nrm(ks[14], (DEPTH, D_MODEL, D_MODEL), D_MODEL ** -0.5),
        "norm2_w": 1.0 + nrm(ks[15], (DEPTH, D_MODEL), 0.01),
        "w_up": nrm(ks[16], (DEPTH, D_MODEL, D_FF), D_MODEL ** -0.5),
        "w_down": nrm(ks[17], (DEPTH, D_FF, D_MODEL), D_FF ** -0.5),
        "final_norm_w": 1.0 + nrm(ks[18], (D_MODEL,), 0.01),
    }


def reference(x_prompt, x_sample, cache_sb_k, cache_sb_v, state_gdn_S, state_gdn_conv,
              norm1_w, w_in, conv_w, A_log, dt_bias, gdn_norm_w, w_gdn_o, w_sb_o, w_out,
              norm2_w, w_up, w_down, final_norm_w):
    hp, hs = x_prompt, x_sample
    Bp = x_prompt.shape[0]
    pk, pv, pS, pc, sk, sv, sS, sc = [], [], [], [], [], [], [], []
    for l in range(DEPTH):
        weights = (norm1_w[l], w_in[l], conv_w[l], A_log[l], dt_bias[l], gdn_norm_w[l],
                   w_gdn_o[l], w_sb_o[l], w_out[l], norm2_w[l], w_up[l], w_down[l])
        conv0 = jnp.zeros((Bp, CONV_WIDTH - 1, CONV_CH), hp.dtype)
        S0 = jnp.zeros((Bp, GDN_V_HEADS, HEAD_DIM, HEAD_DIM), state_gdn_S.dtype)
        hp, c_p, S_p, k_p, v_p = layer(hp, conv0, S0, None, None, CHUNK, *weights)
        hs, c_s, S_s, k_s, v_s = layer(hs, state_gdn_conv[l], state_gdn_S[l], cache_sb_k[l], cache_sb_v[l],
                                       hs.shape[1], *weights)
        pk.append(k_p); pv.append(v_p); pS.append(S_p); pc.append(c_p)
        sk.append(k_s); sv.append(v_s); sS.append(S_s); sc.append(c_s)
    y_prompt = rmsnorm(hp, final_norm_w)
    y_sample = rmsnorm(hs, final_norm_w)
    return (y_prompt, y_sample,
            jnp.stack(pk), jnp.stack(pv), jnp.stack(pS), jnp.stack(pc),
            jnp.stack(sk), jnp.stack(sv), jnp.stack(sS), jnp.stack(sc))
```

```python
import functools
import math

import jax
import jax.numpy as jnp
from jax import lax
from jax.experimental import pallas as pl
from jax.experimental.pallas import tpu as pltpu

F32 = jnp.float32
BF16 = jnp.bfloat16

EPS = 1e-6
HEAD_DIM = 128
GDN_QK_HEADS = 8
GDN_V_HEADS = 16
GDN_QK_WIDTH = GDN_QK_HEADS * HEAD_DIM
GDN_V_WIDTH = GDN_V_HEADS * HEAD_DIM
CONV_WIDTH = 4
CONV_CH = 2 * GDN_QK_WIDTH + GDN_V_WIDTH
SB_HEADS = 16
SB_WIDTH = SB_HEADS * HEAD_DIM
SB_BLOCK = 128
PROMPT_CHUNK = 64

V7X_MXU_DIM = 256
V7X_VMEM_LIMIT_BYTES = 56 * 1024 * 1024
SUBLANES = 8

OFF_Z = CONV_CH
OFF_A = OFF_Z + GDN_V_WIDTH
OFF_B = OFF_A + GDN_V_HEADS
OFF_SB = OFF_B + GDN_V_HEADS
OFF_GATE = OFF_SB + 3 * SB_WIDTH


def _cparams(*sem):
    return pltpu.CompilerParams(dimension_semantics=sem, vmem_limit_bytes=V7X_VMEM_LIMIT_BYTES)


def _dot(a, b):
    return jnp.dot(a, b, preferred_element_type=F32)


def _dot_nt(a, b):
    return lax.dot_general(a, b, (((1,), (1,)), ((), ())), preferred_element_type=F32)


def _dot_tn(a, b):
    return lax.dot_general(a, b, (((0,), (0,)), ((), ())), preferred_element_type=F32)


def _sigmoid(x):
    return 1.0 / (1.0 + jnp.exp(-x))


def _softplus(x):
    return jnp.maximum(x, 0.0) + jnp.log1p(jnp.exp(-jnp.abs(x)))


def _split3(x):
    x1 = x.astype(BF16)
    r1 = x - x1.astype(F32)
    x2 = r1.astype(BF16)
    x3 = (r1 - x2.astype(F32)).astype(BF16)
    return x1, x2, x3


def _rmsnorm_kernel(x_ref, w_ref, o_ref):
    x = x_ref[...]
    var = jnp.mean(x * x, axis=-1, keepdims=True)
    o_ref[...] = (x * lax.rsqrt(var + EPS) * w_ref[...]).astype(o_ref.dtype)


def _rmsnorm(x, w, out_dtype):
    m, d = x.shape
    tm = min(m, 256)
    return pl.pallas_call(
        _rmsnorm_kernel,
        grid=(m // tm,),
        in_specs=[pl.BlockSpec((tm, d), lambda i: (i, 0)), pl.BlockSpec((1, d), lambda i: (0, 0))],
        out_specs=pl.BlockSpec((tm, d), lambda i: (i, 0)),
        out_shape=jax.ShapeDtypeStruct((m, d), out_dtype),
        compiler_params=_cparams("parallel"),
        name="rmsnorm",
    )(x, w.reshape(1, d).astype(F32))


def _mm_tiles(m, n):
    tm = min(m, 1024)
    tn = min(n, 512 if m > 256 else 1024)
    assert m % tm == 0 and n % tn == 0, (m, n)
    return tm, tn


def _mm_kernel(a_ref, w_ref, *o_refs):
    acc = _dot(a_ref[...], w_ref[...])
    for o_ref in o_refs:
        o_ref[...] = acc.astype(o_ref.dtype)


def _matmul(a, w, out_dtypes):
    m, k = a.shape
    n = w.shape[1]
    tm, tn = _mm_tiles(m, n)
    outs = pl.pallas_call(
        _mm_kernel,
        grid=(m // tm, n // tn),
        in_specs=[pl.BlockSpec((tm, k), lambda i, j: (i, 0)), pl.BlockSpec((k, tn), lambda i, j: (0, j))],
        out_specs=[pl.BlockSpec((tm, tn), lambda i, j: (i, j)) for _ in out_dtypes],
        out_shape=[jax.ShapeDtypeStruct((m, n), dt) for dt in out_dtypes],
        compiler_params=_cparams("parallel", "arbitrary"),
        name="matmul",
    )(a, w)
    return outs


def _mm_relu2_kernel(a_ref, w_ref, o_ref):
    acc = _dot(a_ref[...], w_ref[...])
    o_ref[...] = jnp.square(jnp.maximum(acc, 0.0)).astype(o_ref.dtype)


def _matmul_relu2(a, w):
    m, k = a.shape
    n = w.shape[1]
    tm, tn = _mm_tiles(m, n)
    return pl.pallas_call(
        _mm_relu2_kernel,
        grid=(m // tm, n // tn),
        in_specs=[pl.BlockSpec((tm, k), lambda i, j: (i, 0)), pl.BlockSpec((k, tn), lambda i, j: (0, j))],
        out_specs=pl.BlockSpec((tm, tn), lambda i, j: (i, j)),
        out_shape=jax.ShapeDtypeStruct((m, n), BF16),
        compiler_params=_cparams("parallel", "arbitrary"),
        name="matmul_relu2",
    )(a, w)


def _mm_res_kernel(a_ref, w_ref, r_ref, o_ref):
    o_ref[...] = r_ref[...] + _dot(a_ref[...], w_ref[...])


def _matmul_residual(a, w, res):
    m, k = a.shape
    n = w.shape[1]
    tm, tn = _mm_tiles(m, n)
    return pl.pallas_call(
        _mm_res_kernel,
        grid=(m // tm, n // tn),
        in_specs=[pl.BlockSpec((tm, k), lambda i, j: (i, 0)), pl.BlockSpec((k, tn), lambda i, j: (0, j)),
                  pl.BlockSpec((tm, tn), lambda i, j: (i, j))],
        out_specs=pl.BlockSpec((tm, tn), lambda i, j: (i, j)),
        out_shape=jax.ShapeDtypeStruct((m, n), F32),
        compiler_params=_cparams("parallel", "arbitrary"),
        name="matmul_residual",
    )(a, w, res)


def _mm_kacc_kernel(a_ref, w_ref, r_ref, o_ref, acc_ref):
    kk = pl.program_id(2)

    @pl.when(kk == 0)
    def _():
        acc_ref[...] = jnp.zeros_like(acc_ref)

    acc_ref[...] += _dot(a_ref[...], w_ref[...])

    @pl.when(kk == pl.num_programs(2) - 1)
    def _():
        o_ref[...] = r_ref[...] + acc_ref[...]


def _matmul_residual_ktiled(a, w, res):
    m, k = a.shape
    n = w.shape[1]
    tm = min(m, 1024)
    tn = min(n, 1024)
    tk = min(k, 2048)
    return pl.pallas_call(
        _mm_kacc_kernel,
        grid=(m // tm, n // tn, k // tk),
        in_specs=[pl.BlockSpec((tm, tk), lambda i, j, l: (i, l)), pl.BlockSpec((tk, tn), lambda i, j, l: (l, j)),
                  pl.BlockSpec((tm, tn), lambda i, j, l: (i, j))],
        out_specs=pl.BlockSpec((tm, tn), lambda i, j, l: (i, j)),
        out_shape=jax.ShapeDtypeStruct((m, n), F32),
        scratch_shapes=[pltpu.VMEM((tm, tn), F32)],
        compiler_params=_cparams("parallel", "parallel", "arbitrary"),
        name="matmul_residual_ktiled",
    )(a, w, res)


def _merge_kernel(oa_ref, ob_ref, wa_ref, wb_ref, ga_ref, gb_ref, o_ref):
    ya = _dot(oa_ref[...], wa_ref[...])
    yb = _dot(ob_ref[...], wb_ref[...])
    o_ref[...] = (_sigmoid(ga_ref[...]) * ya + _sigmoid(gb_ref[...]) * yb).astype(o_ref.dtype)


def _merge(oa, ob, wa, wb, gates):
    m, k = oa.shape
    n = wa.shape[1]
    tm, tn = _mm_tiles(m, n)
    nb = n // tn
    return pl.pallas_call(
        _merge_kernel,
        grid=(m // tm, nb),
        in_specs=[pl.BlockSpec((tm, k), lambda i, j: (i, 0)), pl.BlockSpec((tm, k), lambda i, j: (i, 0)),
                  pl.BlockSpec((k, tn), lambda i, j: (0, j)), pl.BlockSpec((k, tn), lambda i, j: (0, j)),
                  pl.BlockSpec((tm, tn), lambda i, j: (i, j)), pl.BlockSpec((tm, tn), lambda i, j: (i, j + nb))],
        out_specs=pl.BlockSpec((tm, tn), lambda i, j: (i, j)),
        out_shape=jax.ShapeDtypeStruct((m, n), BF16),
        compiler_params=_cparams("parallel", "arbitrary"),
        name="merge",
    )(oa, ob, wa, wb, gates, gates)


def _gdn_kernel(qkv_ref, z_ref, abc_ref, ar_ref, past_ref, s0_ref, cw_ref, alc_ref, dtc_ref, alr_ref, dtr_ref,
                gnw_ref, o_ref, s_ref, xp_ref, act_ref, ltri_ref, ubd_ref, cmask_ref, smask_ref, *, chunk, group):
    c_len, g_sz = chunk, group
    r_len = c_len * g_sz
    n_groups = GDN_V_HEADS // g_sz
    rep = GDN_V_HEADS // GDN_QK_HEADS
    c_idx = pl.program_id(1)

    @pl.when(c_idx == 0)
    def _init():
        s_ref[...] = s0_ref[...]
        xp_ref[0:SUBLANES, :] = past_ref[0]
        ri = lax.broadcasted_iota(jnp.int32, (r_len, r_len), 0)
        ci = lax.broadcasted_iota(jnp.int32, (r_len, r_len), 1)
        shift = int(math.log2(c_len))
        same = (ri >> shift) == (ci >> shift)
        cmask_ref[...] = jnp.where(same & (ri >= ci), 1.0, 0.0)
        smask_ref[...] = jnp.where(same & (ri > ci), 1.0, 0.0)
        ubd_ref[...] = jnp.where(same & (ri <= ci), 1.0, 0.0).astype(BF16)
        rc = lax.broadcasted_iota(jnp.int32, (c_len, c_len), 0)
        cc = lax.broadcasted_iota(jnp.int32, (c_len, c_len), 1)
        ltri_ref[...] = jnp.where(rc >= cc, 1.0, 0.0).astype(BF16)

    @pl.when(c_idx > 0)
    def _carry():
        xp_ref[0:SUBLANES, :] = xp_ref[c_len:c_len + SUBLANES, :]

    xp_ref[SUBLANES:SUBLANES + c_len, :] = qkv_ref[0]

    for s in range(CONV_CH // HEAD_DIM):
        cols = slice(s * HEAD_DIM, (s + 1) * HEAD_DIM)
        y = cw_ref[0:1, cols] * xp_ref[SUBLANES - 3:SUBLANES - 3 + c_len, cols]
        for i in range(1, CONV_WIDTH):
            y = y + cw_ref[i:i + 1, cols] * xp_ref[SUBLANES - 3 + i:SUBLANES - 3 + i + c_len, cols]
        y = y * _sigmoid(y)
        if s < 2 * GDN_QK_HEADS:
            y = y * lax.rsqrt(jnp.sum(y * y, axis=-1, keepdims=True) + EPS)
            if s < GDN_QK_HEADS:
                y = y * (HEAD_DIM ** -0.5)
        act_ref[:, cols] = y

    ab = abc_ref[0]
    g_col = -jnp.exp(alc_ref[...]) * _softplus(ab + dtc_ref[...])
    beta_col = _sigmoid(ab)
    ltri = ltri_ref[...]
    gc_col = sum(_dot(ltri, t) for t in _split3(g_col))
    g_row = -jnp.exp(alr_ref[...]) * _softplus(ar_ref[0, 0] + dtr_ref[...])
    ubd = ubd_ref[...]
    gc_row = sum(_dot(t, ubd) for t in _split3(g_row))

    cmask = cmask_ref[...]
    smask = smask_ref[...]
    gnw = gnw_ref[...]

    for grp in range(n_groups):
        heads = [grp * g_sz + hb for hb in range(g_sz)]
        kst = jnp.concatenate(
            [act_ref[:, GDN_QK_WIDTH + (h // rep) * HEAD_DIM:GDN_QK_WIDTH + (h // rep + 1) * HEAD_DIM] for h in heads], axis=0)
        qst = jnp.concatenate([act_ref[:, (h // rep) * HEAD_DIM:(h // rep + 1) * HEAD_DIM] for h in heads], axis=0)
        vst = jnp.concatenate(
            [act_ref[:, 2 * GDN_QK_WIDTH + h * HEAD_DIM:2 * GDN_QK_WIDTH + (h + 1) * HEAD_DIM] for h in heads], axis=0)
        gcb = jnp.concatenate([jnp.broadcast_to(gc_col[:, h:h + 1], (c_len, HEAD_DIM)) for h in heads], axis=0)
        bcb = jnp.concatenate(
            [jnp.broadcast_to(beta_col[:, GDN_V_HEADS + h:GDN_V_HEADS + h + 1], (c_len, HEAD_DIM)) for h in heads], axis=0)
        gcb2 = jnp.concatenate([gcb] * (r_len // HEAD_DIM), axis=1)
        bcb2 = jnp.concatenate([bcb] * (r_len // HEAD_DIM), axis=1)
        kst_b = kst.astype(BF16)
        gram = _dot_nt(kst_b, kst_b)
        qk = _dot_nt(qst.astype(BF16), kst_b)
        decay = jnp.exp(jnp.minimum(gcb2 - gc_row[grp:grp + 1, :], 0.0))
        aqk = qk * decay * cmask
        nm = -(gram * decay * bcb2 * smask)
        tp = nm
        xpow = nm
        n_steps = int(math.log2(c_len))
        for step in range(n_steps):
            xb = xpow.astype(BF16)
            tp = tp + _dot(xb, tp.astype(BF16))
            if step + 1 < n_steps:
                xpow = _dot(xb, xb)
        eg = jnp.exp(gcb)
        rhs = jnp.concatenate([vst * bcb, kst * (bcb * eg)], axis=1)
        sol = rhs + _dot(tp.astype(BF16), rhs.astype(BF16))
        qg = qst * eg
        us, qss = [], []
        for hb, h in enumerate(heads):
            rows = slice(hb * c_len, (hb + 1) * c_len)
            s_old = s_ref[0, h]
            wq = jnp.concatenate([sol[rows, HEAD_DIM:2 * HEAD_DIM], qg[rows]], axis=0).astype(BF16)
            r1 = _dot(wq, s_old.astype(BF16))
            u = sol[rows, 0:HEAD_DIM] - r1[0:c_len]
            qss.append(r1[c_len:2 * c_len])
            gl = gcb[(hb + 1) * c_len - 1:(hb + 1) * c_len, :]
            kd = kst[rows] * jnp.exp(gl - gcb[rows])
            s_ref[0, h] = jnp.exp(gl) * s_old + _dot_tn(kd.astype(BF16), u.astype(BF16))
            us.append(u)
        ust = jnp.concatenate(us, axis=0)
        o_intra = _dot(aqk.astype(BF16), ust.astype(BF16))
        for hb, h in enumerate(heads):
            rows = slice(hb * c_len, (hb + 1) * c_len)
            o = qss[hb] + o_intra[rows]
            o = o * lax.rsqrt(jnp.mean(o * o, axis=-1, keepdims=True) + EPS) * gnw
            zz = z_ref[0, :, h * HEAD_DIM:(h + 1) * HEAD_DIM]
            o_ref[0, :, h * HEAD_DIM:(h + 1) * HEAD_DIM] = (o * (zz * _sigmoid(zz))).astype(o_ref.dtype)


def _gdn(proj_a, ab, conv_past, s0, conv_w, a_log, dt_bias, gnw, chunk):
    b, l, _ = proj_a.shape
    n_chunks = l // chunk
    group = V7X_MXU_DIM // chunk
    n_groups = GDN_V_HEADS // group
    r_len = chunk * group
    a_row = ab[:, :, :GDN_V_HEADS].reshape(b, n_chunks, chunk, n_groups, group)
    a_row = a_row.transpose(0, 1, 3, 4, 2).reshape(b, n_chunks, n_groups, r_len)
    past8 = jnp.pad(conv_past.astype(F32), ((0, 0), (SUBLANES - (CONV_WIDTH - 1), 0), (0, 0)))
    pad16 = HEAD_DIM - GDN_V_HEADS
    alc = jnp.pad(a_log.astype(F32), (0, pad16)).reshape(1, HEAD_DIM)
    dtc = jnp.pad(dt_bias.astype(F32), (0, pad16)).reshape(1, HEAD_DIM)
    alr = jnp.repeat(a_log.astype(F32), chunk).reshape(n_groups, r_len)
    dtr = jnp.repeat(dt_bias.astype(F32), chunk).reshape(n_groups, r_len)
    const = lambda *shape: pl.BlockSpec(shape, lambda i, c: (0,) * len(shape))
    return pl.pallas_call(
        functools.partial(_gdn_kernel, chunk=chunk, group=group),
        grid=(b, n_chunks),
        in_specs=[
            pl.BlockSpec((1, chunk, CONV_CH), lambda i, c: (i, c, 0)),
            pl.BlockSpec((1, chunk, GDN_V_WIDTH), lambda i, c: (i, c, CONV_CH // GDN_V_WIDTH)),
            pl.BlockSpec((1, chunk, HEAD_DIM), lambda i, c: (i, c, 0)),
            pl.BlockSpec((1, 1, n_groups, r_len), lambda i, c: (i, c, 0, 0)),
            pl.BlockSpec((1, SUBLANES, CONV_CH), lambda i, c: (i, 0, 0)),
            pl.BlockSpec((1, GDN_V_HEADS, HEAD_DIM, HEAD_DIM), lambda i, c: (i, 0, 0, 0)),
            const(CONV_WIDTH, CONV_CH), const(1, HEAD_DIM), const(1, HEAD_DIM),
            const(n_groups, r_len), const(n_groups, r_len), const(1, HEAD_DIM),
        ],
        out_specs=[
            pl.BlockSpec((1, chunk, GDN_V_WIDTH), lambda i, c: (i, c, 0)),
            pl.BlockSpec((1, GDN_V_HEADS, HEAD_DIM, HEAD_DIM), lambda i, c: (i, 0, 0, 0)),
        ],
        out_shape=[jax.ShapeDtypeStruct((b, l, GDN_V_WIDTH), BF16),
                   jax.ShapeDtypeStruct((b, GDN_V_HEADS, HEAD_DIM, HEAD_DIM), F32)],
        scratch_shapes=[
            pltpu.VMEM((chunk + SUBLANES, CONV_CH), F32),
            pltpu.VMEM((chunk, CONV_CH), F32),
            pltpu.VMEM((chunk, chunk), BF16),
            pltpu.VMEM((r_len, r_len), BF16),
            pltpu.VMEM((r_len, r_len), F32),
            pltpu.VMEM((r_len, r_len), F32),
        ],
        compiler_params=_cparams("parallel", "arbitrary"),
        name="gated_delta_rule",
    )(proj_a, proj_a, ab, a_row, past8, s0.astype(F32), conv_w.astype(F32), alc, dtc, alr, dtr,
      gnw.astype(F32).reshape(1, HEAD_DIM))


def _sb_kernel(q_ref, kt_ref, v_ref, o_ref, *, tq, q_pos0):
    i = pl.program_id(2)
    q = q_ref[0]
    qpos0 = q_pos0 + i * tq
    row = lax.broadcasted_iota(jnp.int32, (tq, SB_BLOCK), 0)
    col = lax.broadcasted_iota(jnp.int32, (tq, SB_BLOCK), 1)
    ri = lax.broadcasted_iota(jnp.int32, (SB_BLOCK, SB_BLOCK), 0)
    ci = lax.broadcasted_iota(jnp.int32, (SB_BLOCK, SB_BLOCK), 1)
    ustrict = jnp.where(ri > ci, 1.0, 0.0).astype(BF16)
    n_masked = max(tq // SB_BLOCK, 1)
    j_full = qpos0 // SB_BLOCK

    def tile(j, later, acc, masked):
        kt = kt_ref[0, 0, j]
        vj = v_ref[0, pl.ds(pl.multiple_of(j * SB_BLOCK, SB_BLOCK), SB_BLOCK), :]
        z = _dot(q, kt) * (HEAD_DIM ** -0.5)
        ls = -_softplus(z)
        if masked:
            vis = (j * SB_BLOCK + col) < (qpos0 + row)
            lf = jnp.where(vis, ls, 0.0)
        else:
            lf = ls
        within = _dot(lf.astype(BF16), ustrict)
        p = jnp.exp((z + ls) + within + later)
        if masked:
            p = jnp.where(vis, p, 0.0)
        acc = acc + _dot(p.astype(BF16), vj)
        return later + jnp.sum(lf, axis=-1, keepdims=True), acc

    later = jnp.zeros((tq, 1), F32)
    acc = jnp.zeros((tq, HEAD_DIM), F32)
    for m in range(n_masked):
        later, acc = tile(j_full + (n_masked - 1 - m), later, acc, True)

    def body(it, carry):
        return tile(j_full - 1 - it, carry[0], carry[1], False)

    later, acc = lax.fori_loop(0, j_full, body, (later, acc))
    o_ref[0] = acc.astype(o_ref.dtype)


def _stick_breaking(q, k_bf, v_bf, q_pos0, tq):
    b, lq, _ = q.shape
    lk = k_bf.shape[1]
    nkb = lk // SB_BLOCK
    kt = k_bf.reshape(b, nkb, SB_BLOCK, SB_HEADS, HEAD_DIM).transpose(0, 3, 1, 4, 2)
    return pl.pallas_call(
        functools.partial(_sb_kernel, tq=tq, q_pos0=q_pos0),
        grid=(b, SB_HEADS, lq // tq),
        in_specs=[
            pl.BlockSpec((1, tq, HEAD_DIM), lambda s, h, i: (s, i, h)),
            pl.BlockSpec((1, 1, nkb, HEAD_DIM, SB_BLOCK), lambda s, h, i: (s, h, 0, 0, 0)),
            pl.BlockSpec((1, lk, HEAD_DIM), lambda s, h, i: (s, 0, h)),
        ],
        out_specs=pl.BlockSpec((1, tq, HEAD_DIM), lambda s, h, i: (s, i, h)),
        out_shape=jax.ShapeDtypeStruct((b, lq, SB_WIDTH), BF16),
        compiler_params=_cparams("parallel", "parallel", "arbitrary"),
        name="stick_breaking",
    )(q, kt, v_bf)


def _layer(x, conv_past, s0, past_k, past_v, chunk, wts):
    b, l, d = x.shape
    m = b * l
    x2 = x.reshape(m, d)
    xn = _rmsnorm(x2, wts["norm1_w"], BF16)
    (proj_a,) = _matmul(xn, wts["w_in_a"], (F32,))
    (ab,) = _matmul(xn, wts["w_in_ab"], (F32,))
    (q_sb,) = _matmul(xn, wts["w_in_q"], (BF16,))
    k_sb, k_bf = _matmul(xn, wts["w_in_k"], (F32, BF16))
    v_sb, v_bf = _matmul(xn, wts["w_in_v"], (F32, BF16))
    (gates,) = _matmul(xn, wts["w_in_g"], (F32,))

    o_a, s_new = _gdn(proj_a.reshape(b, l, -1), ab.reshape(b, l, -1), conv_past, s0, wts["conv_w"], wts["A_log"],
                      wts["dt_bias"], wts["gdn_norm_w"], chunk)
    conv_state = proj_a.reshape(b, l, -1)[:, l - (CONV_WIDTH - 1):, :CONV_CH]

    k_bf = k_bf.reshape(b, l, SB_WIDTH)
    v_bf = v_bf.reshape(b, l, SB_WIDTH)
    if past_k is None:
        q_pos0, tq = 0, min(l, 256)
    else:
        p = past_k.shape[1]
        pad = (-(p + l)) % SB_BLOCK
        k_bf = jnp.concatenate([past_k.reshape(b, p, SB_WIDTH).astype(BF16), k_bf,
                                jnp.zeros((b, pad, SB_WIDTH), BF16)], axis=1)
        v_bf = jnp.concatenate([past_v.reshape(b, p, SB_WIDTH).astype(BF16), v_bf,
                                jnp.zeros((b, pad, SB_WIDTH), BF16)], axis=1)
        q_pos0, tq = p, l
    o_b = _stick_breaking(q_sb.reshape(b, l, SB_WIDTH), k_bf, v_bf, q_pos0, tq)

    merged = _merge(o_a.reshape(m, GDN_V_WIDTH), o_b.reshape(m, SB_WIDTH), wts["w_gdn_o"], wts["w_sb_o"], gates)
    h = _matmul_residual(merged, wts["w_out"], x2)
    hn = _rmsnorm(h, wts["norm2_w"], BF16)
    hid = _matmul_relu2(hn, wts["w_up"])
    h = _matmul_residual_ktiled(hid, wts["w_down"], h)
    return (h, conv_state, s_new, k_sb.reshape(b, l, SB_HEADS, HEAD_DIM), v_sb.reshape(b, l, SB_HEADS, HEAD_DIM))


def kernel(x_prompt, x_sample, cache_sb_k, cache_sb_v, state_gdn_S, state_gdn_conv, norm1_w, w_in, conv_w, A_log,
           dt_bias, gdn_norm_w, w_gdn_o, w_sb_o, w_out, norm2_w, w_up, w_down, final_norm_w):
    depth = w_in.shape[0]
    assert depth == 1
    w_in0 = w_in[0]
    ab_cols = jnp.pad(w_in0[:, OFF_A:OFF_SB], ((0, 0), (0, HEAD_DIM - 2 * GDN_V_HEADS)))
    wts = {
        "norm1_w": norm1_w[0], "norm2_w": norm2_w[0], "conv_w": conv_w[0], "A_log": A_log[0], "dt_bias": dt_bias[0],
        "gdn_norm_w": gdn_norm_w[0],
        "w_in_a": w_in0[:, :OFF_A].astype(BF16),
        "w_in_ab": ab_cols.astype(BF16),
        "w_in_q": w_in0[:, OFF_SB:OFF_SB + SB_WIDTH].astype(BF16),
        "w_in_k": w_in0[:, OFF_SB + SB_WIDTH:OFF_SB + 2 * SB_WIDTH].astype(BF16),
        "w_in_v": w_in0[:, OFF_SB + 2 * SB_WIDTH:OFF_GATE].astype(BF16),
        "w_in_g": w_in0[:, OFF_GATE:].astype(BF16),
        "w_gdn_o": w_gdn_o[0].astype(BF16), "w_sb_o": w_sb_o[0].astype(BF16), "w_out": w_out[0].astype(BF16),
        "w_up": w_up[0].astype(BF16), "w_down": w_down[0].astype(BF16),
    }
    bp, lp, d = x_prompt.shape
    bs, ls, _ = x_sample.shape
    conv0 = jnp.zeros((bp, CONV_WIDTH - 1, CONV_CH), F32)
    s_zero = jnp.zeros((bp, GDN_V_HEADS, HEAD_DIM, HEAD_DIM), F32)
    hp, c_p, s_p, k_p, v_p = _layer(x_prompt, conv0, s_zero, None, None, min(PROMPT_CHUNK, lp), wts)
    hs, c_s, s_s, k_s, v_s = _layer(x_sample, state_gdn_conv[0], state_gdn_S[0], cache_sb_k[0], cache_sb_v[0], ls, wts)
    y_prompt = _rmsnorm(hp, final_norm_w, F32).reshape(bp, lp, d)
    y_sample = _rmsnorm(hs, final_norm_w, F32).reshape(bs, ls, d)
    return (y_prompt, y_sample, k_p[None], v_p[None], s_p[None], c_p[None],
            k_s[None], v_s[None], s_s[None].astype(state_gdn_S.dtype), c_s[None])
```

```python
import functools
import math

import jax
import jax.numpy as jnp
from jax import lax
from jax.experimental import pallas as pl
from jax.experimental.pallas import tpu as pltpu

F32 = jnp.float32
BF16 = jnp.bfloat16

EPS = 1e-6
HEAD_DIM = 128
GDN_QK_HEADS = 8
GDN_V_HEADS = 16
GDN_QK_WIDTH = GDN_QK_HEADS * HEAD_DIM
GDN_V_WIDTH = GDN_V_HEADS * HEAD_DIM
CONV_WIDTH = 4
CONV_CH = 2 * GDN_QK_WIDTH + GDN_V_WIDTH
SB_HEADS = 16
SB_WIDTH = SB_HEADS * HEAD_DIM
SB_BLOCK = 128
SB_UNDERFLOW = 106.0
PROMPT_CHUNK = 64

V7X_MXU_DIM = 256
V7X_VMEM_LIMIT_BYTES = 56 * 1024 * 1024
SUBLANES = 8

OFF_Z = CONV_CH
OFF_A = OFF_Z + GDN_V_WIDTH
OFF_B = OFF_A + GDN_V_HEADS
OFF_SB = OFF_B + GDN_V_HEADS
OFF_GATE = OFF_SB + 3 * SB_WIDTH


def _cparams(*sem):
    return pltpu.CompilerParams(dimension_semantics=sem, vmem_limit_bytes=V7X_VMEM_LIMIT_BYTES)


def _dot(a, b):
    return jnp.dot(a, b, preferred_element_type=F32)


def _dot_nt(a, b):
    return lax.dot_general(a, b, (((1,), (1,)), ((), ())), preferred_element_type=F32)


def _dot_tn(a, b):
    return lax.dot_general(a, b, (((0,), (0,)), ((), ())), preferred_element_type=F32)


def _sigmoid(x):
    return 1.0 / (1.0 + jnp.exp(-x))


def _softplus(x):
    return jnp.maximum(x, 0.0) + jnp.log1p(jnp.exp(-jnp.abs(x)))


def _split3(x):
    x1 = x.astype(BF16)
    r1 = x - x1.astype(F32)
    x2 = r1.astype(BF16)
    x3 = (r1 - x2.astype(F32)).astype(BF16)
    return x1, x2, x3


def _rmsnorm_kernel(x_ref, w_ref, o_ref):
    x = x_ref[...]
    var = jnp.mean(x * x, axis=-1, keepdims=True)
    o_ref[...] = (x * lax.rsqrt(var + EPS) * w_ref[...]).astype(o_ref.dtype)


def _rmsnorm(x, w, out_dtype):
    m, d = x.shape
    tm = min(m, 256)
    return pl.pallas_call(
        _rmsnorm_kernel,
        grid=(m // tm,),
        in_specs=[pl.BlockSpec((tm, d), lambda i: (i, 0)), pl.BlockSpec((1, d), lambda i: (0, 0))],
        out_specs=pl.BlockSpec((tm, d), lambda i: (i, 0)),
        out_shape=jax.ShapeDtypeStruct((m, d), out_dtype),
        compiler_params=_cparams("parallel"),
        name="rmsnorm",
    )(x, w.reshape(1, d).astype(F32))


def _mm_tiles(m, n):
    tm = min(m, 1024)
    tn = min(n, 512 if m > 256 else 1024)
    assert m % tm == 0 and n % tn == 0, (m, n)
    return tm, tn


def _mm_kernel(a_ref, w_ref, *o_refs):
    acc = _dot(a_ref[...], w_ref[...])
    for o_ref in o_refs:
        o_ref[...] = acc.astype(o_ref.dtype)


def _matmul(a, w, out_dtypes):
    m, k = a.shape
    n = w.shape[1]
    tm, tn = _mm_tiles(m, n)
    outs = pl.pallas_call(
        _mm_kernel,
        grid=(m // tm, n // tn),
        in_specs=[pl.BlockSpec((tm, k), lambda i, j: (i, 0)), pl.BlockSpec((k, tn), lambda i, j: (0, j))],
        out_specs=[pl.BlockSpec((tm, tn), lambda i, j: (i, j)) for _ in out_dtypes],
        out_shape=[jax.ShapeDtypeStruct((m, n), dt) for dt in out_dtypes],
        compiler_params=_cparams("parallel", "arbitrary"),
        name="matmul",
    )(a, w)
    return outs


def _mm_relu2_kernel(a_ref, w_ref, o_ref):
    acc = _dot(a_ref[...], w_ref[...])
    o_ref[...] = jnp.square(jnp.maximum(acc, 0.0)).astype(o_ref.dtype)


def _matmul_relu2(a, w):
    m, k = a.shape
    n = w.shape[1]
    tm, tn = _mm_tiles(m, n)
    return pl.pallas_call(
        _mm_relu2_kernel,
        grid=(m // tm, n // tn),
        in_specs=[pl.BlockSpec((tm, k), lambda i, j: (i, 0)), pl.BlockSpec((k, tn), lambda i, j: (0, j))],
        out_specs=pl.BlockSpec((tm, tn), lambda i, j: (i, j)),
        out_shape=jax.ShapeDtypeStruct((m, n), BF16),
        compiler_params=_cparams("parallel", "arbitrary"),
        name="matmul_relu2",
    )(a, w)


def _mm_res_kernel(a_ref, w_ref, r_ref, o_ref):
    o_ref[...] = r_ref[...] + _dot(a_ref[...], w_ref[...])


def _matmul_residual(a, w, res):
    m, k = a.shape
    n = w.shape[1]
    tm, tn = _mm_tiles(m, n)
    return pl.pallas_call(
        _mm_res_kernel,
        grid=(m // tm, n // tn),
        in_specs=[pl.BlockSpec((tm, k), lambda i, j: (i, 0)), pl.BlockSpec((k, tn), lambda i, j: (0, j)),
                  pl.BlockSpec((tm, tn), lambda i, j: (i, j))],
        out_specs=pl.BlockSpec((tm, tn), lambda i, j: (i, j)),
        out_shape=jax.ShapeDtypeStruct((m, n), F32),
        compiler_params=_cparams("parallel", "arbitrary"),
        name="matmul_residual",
    )(a, w, res)


def _mm_kacc_kernel(a_ref, w_ref, r_ref, o_ref, acc_ref):
    kk = pl.program_id(2)

    @pl.when(kk == 0)
    def _():
        acc_ref[...] = jnp.zeros_like(acc_ref)

    acc_ref[...] += _dot(a_ref[...], w_ref[...])

    @pl.when(kk == pl.num_programs(2) - 1)
    def _():
        o_ref[...] = r_ref[...] + acc_ref[...]


def _matmul_residual_ktiled(a, w, res):
    m, k = a.shape
    n = w.shape[1]
    tm = min(m, 1024)
    tn = min(n, 1024)
    tk = min(k, 2048)
    return pl.pallas_call(
        _mm_kacc_kernel,
        grid=(m // tm, n // tn, k // tk),
        in_specs=[pl.BlockSpec((tm, tk), lambda i, j, l: (i, l)), pl.BlockSpec((tk, tn), lambda i, j, l: (l, j)),
                  pl.BlockSpec((tm, tn), lambda i, j, l: (i, j))],
        out_specs=pl.BlockSpec((tm, tn), lambda i, j, l: (i, j)),
        out_shape=jax.ShapeDtypeStruct((m, n), F32),
        scratch_shapes=[pltpu.VMEM((tm, tn), F32)],
        compiler_params=_cparams("parallel", "parallel", "arbitrary"),
        name="matmul_residual_ktiled",
    )(a, w, res)


def _merge_kernel(oa_ref, ob_ref, wa_ref, wb_ref, ga_ref, gb_ref, o_ref):
    ya = _dot(oa_ref[...], wa_ref[...])
    yb = _dot(ob_ref[...], wb_ref[...])
    o_ref[...] = (_sigmoid(ga_ref[...]) * ya + _sigmoid(gb_ref[...]) * yb).astype(o_ref.dtype)


def _merge(oa, ob, wa, wb, gates):
    m, k = oa.shape
    n = wa.shape[1]
    tm, tn = _mm_tiles(m, n)
    nb = n // tn
    return pl.pallas_call(
        _merge_kernel,
        grid=(m // tm, nb),
        in_specs=[pl.BlockSpec((tm, k), lambda i, j: (i, 0)), pl.BlockSpec((tm, k), lambda i, j: (i, 0)),
                  pl.BlockSpec((k, tn), lambda i, j: (0, j)), pl.BlockSpec((k, tn), lambda i, j: (0, j)),
                  pl.BlockSpec((tm, tn), lambda i, j: (i, j)), pl.BlockSpec((tm, tn), lambda i, j: (i, j + nb))],
        out_specs=pl.BlockSpec((tm, tn), lambda i, j: (i, j)),
        out_shape=jax.ShapeDtypeStruct((m, n), BF16),
        compiler_params=_cparams("parallel", "arbitrary"),
        name="merge",
    )(oa, ob, wa, wb, gates, gates)


def _gdn_kernel(qkv_ref, z_ref, abc_ref, ar_ref, past_ref, s0_ref, cw_ref, alc_ref, dtc_ref, alr_ref, dtr_ref,
                gnw_ref, o_ref, s_ref, xp_ref, act_ref, ltri_ref, ubd_ref, cmask_ref, smask_ref, *, chunk, group):
    c_len, g_sz = chunk, group
    r_len = c_len * g_sz
    n_groups = GDN_V_HEADS // g_sz
    rep = GDN_V_HEADS // GDN_QK_HEADS
    c_idx = pl.program_id(1)

    @pl.when(c_idx == 0)
    def _init():
        s_ref[...] = s0_ref[...]
        xp_ref[0:SUBLANES, :] = past_ref[0]
        ri = lax.broadcasted_iota(jnp.int32, (r_len, r_len), 0)
        ci = lax.broadcasted_iota(jnp.int32, (r_len, r_len), 1)
        shift = int(math.log2(c_len))
        same = (ri >> shift) == (ci >> shift)
        cmask_ref[...] = jnp.where(same & (ri >= ci), 1.0, 0.0)
        smask_ref[...] = jnp.where(same & (ri > ci), 1.0, 0.0)
        ubd_ref[...] = jnp.where(same & (ri <= ci), 1.0, 0.0).astype(BF16)
        rc = lax.broadcasted_iota(jnp.int32, (c_len, c_len), 0)
        cc = lax.broadcasted_iota(jnp.int32, (c_len, c_len), 1)
        ltri_ref[...] = jnp.where(rc >= cc, 1.0, 0.0).astype(BF16)

    @pl.when(c_idx > 0)
    def _carry():
        xp_ref[0:SUBLANES, :] = xp_ref[c_len:c_len + SUBLANES, :]

    xp_ref[SUBLANES:SUBLANES + c_len, :] = qkv_ref[0]

    for s in range(CONV_CH // HEAD_DIM):
        cols = slice(s * HEAD_DIM, (s + 1) * HEAD_DIM)
        y = cw_ref[0:1, cols] * xp_ref[SUBLANES - 3:SUBLANES - 3 + c_len, cols]
        for i in range(1, CONV_WIDTH):
            y = y + cw_ref[i:i + 1, cols] * xp_ref[SUBLANES - 3 + i:SUBLANES - 3 + i + c_len, cols]
        y = y * _sigmoid(y)
        if s < 2 * GDN_QK_HEADS:
            y = y * lax.rsqrt(jnp.sum(y * y, axis=-1, keepdims=True) + EPS)
            if s < GDN_QK_HEADS:
                y = y * (HEAD_DIM ** -0.5)
        act_ref[:, cols] = y

    ab = abc_ref[0]
    g_col = -jnp.exp(alc_ref[...]) * _softplus(ab + dtc_ref[...])
    beta_col = _sigmoid(ab)
    ltri = ltri_ref[...]
    gc_col = sum(_dot(ltri, t) for t in _split3(g_col))
    g_row = -jnp.exp(alr_ref[...]) * _softplus(ar_ref[0, 0] + dtr_ref[...])
    ubd = ubd_ref[...]
    gc_row = sum(_dot(t, ubd) for t in _split3(g_row))

    cmask = cmask_ref[...]
    smask = smask_ref[...]
    gnw = gnw_ref[...]

    for grp in range(n_groups):
        heads = [grp * g_sz + hb for hb in range(g_sz)]
        kst = jnp.concatenate(
            [act_ref[:, GDN_QK_WIDTH + (h // rep) * HEAD_DIM:GDN_QK_WIDTH + (h // rep + 1) * HEAD_DIM] for h in heads], axis=0)
        qst = jnp.concatenate([act_ref[:, (h // rep) * HEAD_DIM:(h // rep + 1) * HEAD_DIM] for h in heads], axis=0)
        vst = jnp.concatenate(
            [act_ref[:, 2 * GDN_QK_WIDTH + h * HEAD_DIM:2 * GDN_QK_WIDTH + (h + 1) * HEAD_DIM] for h in heads], axis=0)
        gcb = jnp.concatenate([jnp.broadcast_to(gc_col[:, h:h + 1], (c_len, HEAD_DIM)) for h in heads], axis=0)
        bcb = jnp.concatenate(
            [jnp.broadcast_to(beta_col[:, GDN_V_HEADS + h:GDN_V_HEADS + h + 1], (c_len, HEAD_DIM)) for h in heads], axis=0)
        gcb2 = jnp.concatenate([gcb] * (r_len // HEAD_DIM), axis=1)
        bcb2 = jnp.concatenate([bcb] * (r_len // HEAD_DIM), axis=1)
        kst_b = kst.astype(BF16)
        gram = _dot_nt(kst_b, kst_b)
        qk = _dot_nt(qst.astype(BF16), kst_b)
        decay = jnp.exp(jnp.minimum(gcb2 - gc_row[grp:grp + 1, :], 0.0))
        aqk = qk * decay * cmask
        nm = -(gram * decay * bcb2 * smask)
        tp = nm
        xpow = nm
        n_steps = int(math.log2(c_len))
        for step in range(n_steps):
            xb = xpow.astype(BF16)
            tp = tp + _dot(xb, tp.astype(BF16))
            if step + 1 < n_steps:
                xpow = _dot(xb, xb)
        eg = jnp.exp(gcb)
        rhs = jnp.concatenate([vst * bcb, kst * (bcb * eg)], axis=1)
        sol = rhs + _dot(tp.astype(BF16), rhs.astype(BF16))
        qg = qst * eg
        us, qss = [], []
        for hb, h in enumerate(heads):
            rows = slice(hb * c_len, (hb + 1) * c_len)
            s_old = s_ref[0, h]
            wq = jnp.concatenate([sol[rows, HEAD_DIM:2 * HEAD_DIM], qg[rows]], axis=0).astype(BF16)
            r1 = _dot(wq, s_old.astype(BF16))
            u = sol[rows, 0:HEAD_DIM] - r1[0:c_len]
            qss.append(r1[c_len:2 * c_len])
            gl = gcb[(hb + 1) * c_len - 1:(hb + 1) * c_len, :]
            kd = kst[rows] * jnp.exp(gl - gcb[rows])
            s_ref[0, h] = jnp.exp(gl) * s_old + _dot_tn(kd.astype(BF16), u.astype(BF16))
            us.append(u)
        ust = jnp.concatenate(us, axis=0)
        o_intra = _dot(aqk.astype(BF16), ust.astype(BF16))
        for hb, h in enumerate(heads):
            rows = slice(hb * c_len, (hb + 1) * c_len)
            o = qss[hb] + o_intra[rows]
            o = o * lax.rsqrt(jnp.mean(o * o, axis=-1, keepdims=True) + EPS) * gnw
            zz = z_ref[0, :, h * HEAD_DIM:(h + 1) * HEAD_DIM]
            o_ref[0, :, h * HEAD_DIM:(h + 1) * HEAD_DIM] = (o * (zz * _sigmoid(zz))).astype(o_ref.dtype)


def _gdn(proj_a, ab, conv_past, s0, conv_w, a_log, dt_bias, gnw, chunk):
    b, l, _ = proj_a.shape
    n_chunks = l // chunk
    group = V7X_MXU_DIM // chunk
    n_groups = GDN_V_HEADS // group
    r_len = chunk * group
    a_row = ab[:, :, :GDN_V_HEADS].reshape(b, n_chunks, chunk, n_groups, group)
    a_row = a_row.transpose(0, 1, 3, 4, 2).reshape(b, n_chunks, n_groups, r_len)
    past8 = jnp.pad(conv_past.astype(F32), ((0, 0), (SUBLANES - (CONV_WIDTH - 1), 0), (0, 0)))
    pad16 = HEAD_DIM - GDN_V_HEADS
    alc = jnp.pad(a_log.astype(F32), (0, pad16)).reshape(1, HEAD_DIM)
    dtc = jnp.pad(dt_bias.astype(F32), (0, pad16)).reshape(1, HEAD_DIM)
    alr = jnp.repeat(a_log.astype(F32), chunk).reshape(n_groups, r_len)
    dtr = jnp.repeat(dt_bias.astype(F32), chunk).reshape(n_groups, r_len)
    const = lambda *shape: pl.BlockSpec(shape, lambda i, c: (0,) * len(shape))
    return pl.pallas_call(
        functools.partial(_gdn_kernel, chunk=chunk, group=group),
        grid=(b, n_chunks),
        in_specs=[
            pl.BlockSpec((1, chunk, CONV_CH), lambda i, c: (i, c, 0)),
            pl.BlockSpec((1, chunk, GDN_V_WIDTH), lambda i, c: (i, c, CONV_CH // GDN_V_WIDTH)),
            pl.BlockSpec((1, chunk, HEAD_DIM), lambda i, c: (i, c, 0)),
            pl.BlockSpec((1, 1, n_groups, r_len), lambda i, c: (i, c, 0, 0)),
            pl.BlockSpec((1, SUBLANES, CONV_CH), lambda i, c: (i, 0, 0)),
            pl.BlockSpec((1, GDN_V_HEADS, HEAD_DIM, HEAD_DIM), lambda i, c: (i, 0, 0, 0)),
            const(CONV_WIDTH, CONV_CH), const(1, HEAD_DIM), const(1, HEAD_DIM),
            const(n_groups, r_len), const(n_groups, r_len), const(1, HEAD_DIM),
        ],
        out_specs=[
            pl.BlockSpec((1, chunk, GDN_V_WIDTH), lambda i, c: (i, c, 0)),
            pl.BlockSpec((1, GDN_V_HEADS, HEAD_DIM, HEAD_DIM), lambda i, c: (i, 0, 0, 0)),
        ],
        out_shape=[jax.ShapeDtypeStruct((b, l, GDN_V_WIDTH), BF16),
                   jax.ShapeDtypeStruct((b, GDN_V_HEADS, HEAD_DIM, HEAD_DIM), F32)],
        scratch_shapes=[
            pltpu.VMEM((chunk + SUBLANES, CONV_CH), F32),
            pltpu.VMEM((chunk, CONV_CH), F32),
            pltpu.VMEM((chunk, chunk), BF16),
            pltpu.VMEM((r_len, r_len), BF16),
            pltpu.VMEM((r_len, r_len), F32),
            pltpu.VMEM((r_len, r_len), F32),
        ],
        compiler_params=_cparams("parallel", "arbitrary"),
        name="gated_delta_rule",
    )(proj_a, proj_a, ab, a_row, past8, s0.astype(F32), conv_w.astype(F32), alc, dtc, alr, dtr,
      gnw.astype(F32).reshape(1, HEAD_DIM))


def _sb_kernel(q_ref, kt_ref, v_ref, o_ref, *, tq, q_pos0):
    i = pl.program_id(2)
    q = q_ref[0]
    qpos0 = q_pos0 + i * tq
    row = lax.broadcasted_iota(jnp.int32, (tq, SB_BLOCK), 0)
    col = lax.broadcasted_iota(jnp.int32, (tq, SB_BLOCK), 1)
    ri = lax.broadcasted_iota(jnp.int32, (SB_BLOCK, SB_BLOCK), 0)
    ci = lax.broadcasted_iota(jnp.int32, (SB_BLOCK, SB_BLOCK), 1)
    ustrict = jnp.where(ri > ci, 1.0, 0.0).astype(BF16)
    n_masked = max(tq // SB_BLOCK, 1)
    j_full = qpos0 // SB_BLOCK

    def tile(j, later, acc, masked):
        kt = kt_ref[0, 0, j]
        vj = v_ref[0, pl.ds(pl.multiple_of(j * SB_BLOCK, SB_BLOCK), SB_BLOCK), :]
        z = _dot(q, kt) * (HEAD_DIM ** -0.5)
        ls = -_softplus(z)
        if masked:
            vis = (j * SB_BLOCK + col) < (qpos0 + row)
            lf = jnp.where(vis, ls, 0.0)
        else:
            lf = ls
        within = _dot(lf.astype(BF16), ustrict)
        p = jnp.exp((z + ls) + within + later)
        if masked:
            p = jnp.where(vis, p, 0.0)
        acc = acc + _dot(p.astype(BF16), vj)
        return later + jnp.sum(lf, axis=-1, keepdims=True), acc

    later = jnp.zeros((tq, 1), F32)
    acc = jnp.zeros((tq, HEAD_DIM), F32)
    for m in range(n_masked):
        later, acc = tile(j_full + (n_masked - 1 - m), later, acc, True)

    def cond(carry):
        return jnp.logical_and(carry[0] >= 0, jnp.max(carry[1]) > -SB_UNDERFLOW)

    def body(carry):
        later, acc = tile(carry[0], carry[1], carry[2], False)
        return carry[0] - 1, later, acc

    _, later, acc = lax.while_loop(cond, body, (j_full - 1, later, acc))
    o_ref[0] = acc.astype(o_ref.dtype)


def _stick_breaking(q, k_bf, v_bf, q_pos0, tq):
    b, lq, _ = q.shape
    lk = k_bf.shape[1]
    nkb = lk // SB_BLOCK
    kt = k_bf.reshape(b, nkb, SB_BLOCK, SB_HEADS, HEAD_DIM).transpose(0, 3, 1, 4, 2)
    return pl.pallas_call(
        functools.partial(_sb_kernel, tq=tq, q_pos0=q_pos0),
        grid=(b, SB_HEADS, lq // tq),
        in_specs=[
            pl.BlockSpec((1, tq, HEAD_DIM), lambda s, h, i: (s, i, h)),
            pl.BlockSpec((1, 1, nkb, HEAD_DIM, SB_BLOCK), lambda s, h, i: (s, h, 0, 0, 0)),
            pl.BlockSpec((1, lk, HEAD_DIM), lambda s, h, i: (s, 0, h)),
        ],
        out_specs=pl.BlockSpec((1, tq, HEAD_DIM), lambda s, h, i: (s, i, h)),
        out_shape=jax.ShapeDtypeStruct((b, lq, SB_WIDTH), BF16),
        compiler_params=_cparams("parallel", "parallel", "arbitrary"),
        name="stick_breaking",
    )(q, kt, v_bf)


def _layer(x, conv_past, s0, past_k, past_v, chunk, wts):
    b, l, d = x.shape
    m = b * l
    x2 = x.reshape(m, d)
    xn = _rmsnorm(x2, wts["norm1_w"], BF16)
    (proj_a,) = _matmul(xn, wts["w_in_a"], (F32,))
    (ab,) = _matmul(xn, wts["w_in_ab"], (F32,))
    (q_sb,) = _matmul(xn, wts["w_in_q"], (BF16,))
    k_sb, k_bf = _matmul(xn, wts["w_in_k"], (F32, BF16))
    v_sb, v_bf = _matmul(xn, wts["w_in_v"], (F32, BF16))
    (gates,) = _matmul(xn, wts["w_in_g"], (F32,))

    o_a, s_new = _gdn(proj_a.reshape(b, l, -1), ab.reshape(b, l, -1), conv_past, s0, wts["conv_w"], wts["A_log"],
                      wts["dt_bias"], wts["gdn_norm_w"], chunk)
    conv_state = proj_a.reshape(b, l, -1)[:, l - (CONV_WIDTH - 1):, :CONV_CH]

    k_bf = k_bf.reshape(b, l, SB_WIDTH)
    v_bf = v_bf.reshape(b, l, SB_WIDTH)
    if past_k is None:
        q_pos0, tq = 0, min(l, 256)
    else:
        p = past_k.shape[1]
        pad = (-(p + l)) % SB_BLOCK
        k_bf = jnp.concatenate([past_k.reshape(b, p, SB_WIDTH).astype(BF16), k_bf,
                                jnp.zeros((b, pad, SB_WIDTH), BF16)], axis=1)
        v_bf = jnp.concatenate([past_v.reshape(b, p, SB_WIDTH).astype(BF16), v_bf,
                                jnp.zeros((b, pad, SB_WIDTH), BF16)], axis=1)
        q_pos0, tq = p, l
    o_b = _stick_breaking(q_sb.reshape(b, l, SB_WIDTH), k_bf, v_bf, q_pos0, tq)

    merged = _merge(o_a.reshape(m, GDN_V_WIDTH), o_b.reshape(m, SB_WIDTH), wts["w_gdn_o"], wts["w_sb_o"], gates)
    h = _matmul_residual(merged, wts["w_out"], x2)
    hn = _rmsnorm(h, wts["norm2_w"], BF16)
    hid = _matmul_relu2(hn, wts["w_up"])
    h = _matmul_residual_ktiled(hid, wts["w_down"], h)
    return (h, conv_state, s_new, k_sb.reshape(b, l, SB_HEADS, HEAD_DIM), v_sb.reshape(b, l, SB_HEADS, HEAD_DIM))


def kernel(x_prompt, x_sample, cache_sb_k, cache_sb_v, state_gdn_S, state_gdn_conv, norm1_w, w_in, conv_w, A_log,
           dt_bias, gdn_norm_w, w_gdn_o, w_sb_o, w_out, norm2_w, w_up, w_down, final_norm_w):
    depth = w_in.shape[0]
    assert depth == 1
    w_in0 = w_in[0]
    ab_cols = jnp.pad(w_in0[:, OFF_A:OFF_SB], ((0, 0), (0, HEAD_DIM - 2 * GDN_V_HEADS)))
    wts = {
        "norm1_w": norm1_w[0], "norm2_w": norm2_w[0], "conv_w": conv_w[0], "A_log": A_log[0], "dt_bias": dt_bias[0],
        "gdn_norm_w": gdn_norm_w[0],
        "w_in_a": w_in0[:, :OFF_A].astype(BF16),
        "w_in_ab": ab_cols.astype(BF16),
        "w_in_q": w_in0[:, OFF_SB:OFF_SB + SB_WIDTH].astype(BF16),
        "w_in_k": w_in0[:, OFF_SB + SB_WIDTH:OFF_SB + 2 * SB_WIDTH].astype(BF16),
        "w_in_v": w_in0[:, OFF_SB + 2 * SB_WIDTH:OFF_GATE].astype(BF16),
        "w_in_g": w_in0[:, OFF_GATE:].astype(BF16),
        "w_gdn_o": w_gdn_o[0].astype(BF16), "w_sb_o": w_sb_o[0].astype(BF16), "w_out": w_out[0].astype(BF16),
        "w_up": w_up[0].astype(BF16), "w_down": w_down[0].astype(BF16),
    }
    bp, lp, d = x_prompt.shape
    bs, ls, _ = x_sample.shape
    conv0 = jnp.zeros((bp, CONV_WIDTH - 1, CONV_CH), F32)
    s_zero = jnp.zeros((bp, GDN_V_HEADS, HEAD_DIM, HEAD_DIM), F32)
    hp, c_p, s_p, k_p, v_p = _layer(x_prompt, conv0, s_zero, None, None, min(PROMPT_CHUNK, lp), wts)
    hs, c_s, s_s, k_s, v_s = _layer(x_sample, state_gdn_conv[0], state_gdn_S[0], cache_sb_k[0], cache_sb_v[0], ls, wts)
    y_prompt = _rmsnorm(hp, final_norm_w, F32).reshape(bp, lp, d)
    y_sample = _rmsnorm(hs, final_norm_w, F32).reshape(bs, ls, d)
    return (y_prompt, y_sample, k_p[None], v_p[None], s_p[None], c_p[None],
            k_s[None], v_s[None], s_s[None].astype(state_gdn_S.dtype), c_s[None])
```

```python
import functools
import math

import jax
import jax.numpy as jnp
from jax import lax
from jax.experimental import pallas as pl
from jax.experimental.pallas import tpu as pltpu

F32 = jnp.float32
BF16 = jnp.bfloat16

EPS = 1e-6
HEAD_DIM = 128
GDN_QK_HEADS = 8
GDN_V_HEADS = 16
GDN_QK_WIDTH = GDN_QK_HEADS * HEAD_DIM
GDN_V_WIDTH = GDN_V_HEADS * HEAD_DIM
CONV_WIDTH = 4
CONV_CH = 2 * GDN_QK_WIDTH + GDN_V_WIDTH
SB_HEADS = 16
SB_WIDTH = SB_HEADS * HEAD_DIM
SB_BLOCK = 128
SB_UNDERFLOW = 106.0
SB_WINDOW = 3
PROMPT_CHUNK = 64

V7X_MXU_DIM = 256
V7X_VMEM_LIMIT_BYTES = 56 * 1024 * 1024
SUBLANES = 8

OFF_Z = CONV_CH
OFF_A = OFF_Z + GDN_V_WIDTH
OFF_B = OFF_A + GDN_V_HEADS
OFF_SB = OFF_B + GDN_V_HEADS
OFF_GATE = OFF_SB + 3 * SB_WIDTH


def _cparams(*sem):
    return pltpu.CompilerParams(dimension_semantics=sem, vmem_limit_bytes=V7X_VMEM_LIMIT_BYTES)


def _dot(a, b):
    return jnp.dot(a, b, preferred_element_type=F32)


def _dot_nt(a, b):
    return lax.dot_general(a, b, (((1,), (1,)), ((), ())), preferred_element_type=F32)


def _dot_tn(a, b):
    return lax.dot_general(a, b, (((0,), (0,)), ((), ())), preferred_element_type=F32)


def _sigmoid(x):
    return 1.0 / (1.0 + jnp.exp(-x))


def _softplus(x):
    return jnp.maximum(x, 0.0) + jnp.log1p(jnp.exp(-jnp.abs(x)))


def _split3(x):
    x1 = x.astype(BF16)
    r1 = x - x1.astype(F32)
    x2 = r1.astype(BF16)
    x3 = (r1 - x2.astype(F32)).astype(BF16)
    return x1, x2, x3


def _rmsnorm_kernel(x_ref, w_ref, o_ref):
    x = x_ref[...]
    var = jnp.mean(x * x, axis=-1, keepdims=True)
    o_ref[...] = (x * lax.rsqrt(var + EPS) * w_ref[...]).astype(o_ref.dtype)


def _rmsnorm(x, w, out_dtype):
    m, d = x.shape
    tm = min(m, 256)
    return pl.pallas_call(
        _rmsnorm_kernel,
        grid=(m // tm,),
        in_specs=[pl.BlockSpec((tm, d), lambda i: (i, 0)), pl.BlockSpec((1, d), lambda i: (0, 0))],
        out_specs=pl.BlockSpec((tm, d), lambda i: (i, 0)),
        out_shape=jax.ShapeDtypeStruct((m, d), out_dtype),
        compiler_params=_cparams("parallel"),
        name="rmsnorm",
    )(x, w.reshape(1, d).astype(F32))


def _mm_tiles(m, n):
    tm = min(m, 1024)
    tn = min(n, 512 if m > 256 else 1024)
    assert m % tm == 0 and n % tn == 0, (m, n)
    return tm, tn


def _mm_kernel(a_ref, w_ref, *o_refs):
    acc = _dot(a_ref[...], w_ref[...])
    for o_ref in o_refs:
        o_ref[...] = acc.astype(o_ref.dtype)


def _matmul(a, w, out_dtypes, col0=0, n=None):
    m, k = a.shape
    n = w.shape[1] if n is None else n
    tm, tn = _mm_tiles(m, n)
    assert col0 % tn == 0
    jb = col0 // tn
    outs = pl.pallas_call(
        _mm_kernel,
        grid=(m // tm, n // tn),
        in_specs=[pl.BlockSpec((tm, k), lambda i, j: (i, 0)), pl.BlockSpec((k, tn), lambda i, j: (0, j + jb))],
        out_specs=[pl.BlockSpec((tm, tn), lambda i, j: (i, j)) for _ in out_dtypes],
        out_shape=[jax.ShapeDtypeStruct((m, n), dt) for dt in out_dtypes],
        compiler_params=_cparams("parallel", "arbitrary"),
        name="matmul",
    )(a, w)
    return outs


def _mm_relu2_kernel(a_ref, w_ref, o_ref):
    acc = _dot(a_ref[...], w_ref[...])
    o_ref[...] = jnp.square(jnp.maximum(acc, 0.0)).astype(o_ref.dtype)


def _matmul_relu2(a, w):
    m, k = a.shape
    n = w.shape[1]
    tm, tn = _mm_tiles(m, n)
    return pl.pallas_call(
        _mm_relu2_kernel,
        grid=(m // tm, n // tn),
        in_specs=[pl.BlockSpec((tm, k), lambda i, j: (i, 0)), pl.BlockSpec((k, tn), lambda i, j: (0, j))],
        out_specs=pl.BlockSpec((tm, tn), lambda i, j: (i, j)),
        out_shape=jax.ShapeDtypeStruct((m, n), BF16),
        compiler_params=_cparams("parallel", "arbitrary"),
        name="matmul_relu2",
    )(a, w)


def _mm_res_kernel(a_ref, w_ref, r_ref, o_ref):
    o_ref[...] = r_ref[...] + _dot(a_ref[...], w_ref[...])


def _matmul_residual(a, w, res):
    m, k = a.shape
    n = w.shape[1]
    tm, tn = _mm_tiles(m, n)
    return pl.pallas_call(
        _mm_res_kernel,
        grid=(m // tm, n // tn),
        in_specs=[pl.BlockSpec((tm, k), lambda i, j: (i, 0)), pl.BlockSpec((k, tn), lambda i, j: (0, j)),
                  pl.BlockSpec((tm, tn), lambda i, j: (i, j))],
        out_specs=pl.BlockSpec((tm, tn), lambda i, j: (i, j)),
        out_shape=jax.ShapeDtypeStruct((m, n), F32),
        compiler_params=_cparams("parallel", "arbitrary"),
        name="matmul_residual",
    )(a, w, res)


def _mm_kacc_kernel(a_ref, w_ref, r_ref, o_ref, acc_ref):
    kk = pl.program_id(2)

    @pl.when(kk == 0)
    def _():
        acc_ref[...] = jnp.zeros_like(acc_ref)

    acc_ref[...] += _dot(a_ref[...], w_ref[...])

    @pl.when(kk == pl.num_programs(2) - 1)
    def _():
        o_ref[...] = r_ref[...] + acc_ref[...]


def _matmul_residual_ktiled(a, w, res):
    m, k = a.shape
    n = w.shape[1]
    tm = min(m, 1024)
    tn = min(n, 1024)
    tk = min(k, 2048)
    return pl.pallas_call(
        _mm_kacc_kernel,
        grid=(m // tm, n // tn, k // tk),
        in_specs=[pl.BlockSpec((tm, tk), lambda i, j, l: (i, l)), pl.BlockSpec((tk, tn), lambda i, j, l: (l, j)),
                  pl.BlockSpec((tm, tn), lambda i, j, l: (i, j))],
        out_specs=pl.BlockSpec((tm, tn), lambda i, j, l: (i, j)),
        out_shape=jax.ShapeDtypeStruct((m, n), F32),
        scratch_shapes=[pltpu.VMEM((tm, tn), F32)],
        compiler_params=_cparams("parallel", "parallel", "arbitrary"),
        name="matmul_residual_ktiled",
    )(a, w, res)


def _merge_kernel(oa_ref, ob_ref, wa_ref, wb_ref, ga_ref, gb_ref, o_ref):
    ya = _dot(oa_ref[...], wa_ref[...])
    yb = _dot(ob_ref[...], wb_ref[...])
    o_ref[...] = (_sigmoid(ga_ref[...]) * ya + _sigmoid(gb_ref[...]) * yb).astype(o_ref.dtype)


def _merge(oa, ob, wa, wb, gates):
    m, k = oa.shape
    n = wa.shape[1]
    tm, tn = _mm_tiles(m, n)
    nb = n // tn
    return pl.pallas_call(
        _merge_kernel,
        grid=(m // tm, nb),
        in_specs=[pl.BlockSpec((tm, k), lambda i, j: (i, 0)), pl.BlockSpec((tm, k), lambda i, j: (i, 0)),
                  pl.BlockSpec((k, tn), lambda i, j: (0, j)), pl.BlockSpec((k, tn), lambda i, j: (0, j)),
                  pl.BlockSpec((tm, tn), lambda i, j: (i, j)), pl.BlockSpec((tm, tn), lambda i, j: (i, j + nb))],
        out_specs=pl.BlockSpec((tm, tn), lambda i, j: (i, j)),
        out_shape=jax.ShapeDtypeStruct((m, n), BF16),
        compiler_params=_cparams("parallel", "arbitrary"),
        name="merge",
    )(oa, ob, wa, wb, gates, gates)


def _gdn_kernel(qkv_ref, z_ref, abc_ref, ar_ref, past_ref, s0_ref, cw_ref, alc_ref, dtc_ref, alr_ref, dtr_ref,
                gnw_ref, o_ref, s_ref, xp_ref, act_ref, ltri_ref, ubd_ref, cmask_ref, smask_ref, *, chunk, group):
    c_len, g_sz = chunk, group
    r_len = c_len * g_sz
    n_groups = GDN_V_HEADS // g_sz
    rep = GDN_V_HEADS // GDN_QK_HEADS
    c_idx = pl.program_id(1)

    @pl.when(c_idx == 0)
    def _init():
        s_ref[...] = s0_ref[...]
        xp_ref[0:SUBLANES, :] = past_ref[0]
        ri = lax.broadcasted_iota(jnp.int32, (r_len, r_len), 0)
        ci = lax.broadcasted_iota(jnp.int32, (r_len, r_len), 1)
        shift = int(math.log2(c_len))
        same = (ri >> shift) == (ci >> shift)
        cmask_ref[...] = jnp.where(same & (ri >= ci), 1.0, 0.0)
        smask_ref[...] = jnp.where(same & (ri > ci), 1.0, 0.0)
        ubd_ref[...] = jnp.where(same & (ri <= ci), 1.0, 0.0).astype(BF16)
        rc = lax.broadcasted_iota(jnp.int32, (c_len, c_len), 0)
        cc = lax.broadcasted_iota(jnp.int32, (c_len, c_len), 1)
        ltri_ref[...] = jnp.where(rc >= cc, 1.0, 0.0).astype(BF16)

    @pl.when(c_idx > 0)
    def _carry():
        xp_ref[0:SUBLANES, :] = xp_ref[c_len:c_len + SUBLANES, :]

    xp_ref[SUBLANES:SUBLANES + c_len, :] = qkv_ref[0]

    for s in range(CONV_CH // HEAD_DIM):
        cols = slice(s * HEAD_DIM, (s + 1) * HEAD_DIM)
        y = cw_ref[0:1, cols] * xp_ref[SUBLANES - 3:SUBLANES - 3 + c_len, cols]
        for i in range(1, CONV_WIDTH):
            y = y + cw_ref[i:i + 1, cols] * xp_ref[SUBLANES - 3 + i:SUBLANES - 3 + i + c_len, cols]
        y = y * _sigmoid(y)
        if s < 2 * GDN_QK_HEADS:
            y = y * lax.rsqrt(jnp.sum(y * y, axis=-1, keepdims=True) + EPS)
            if s < GDN_QK_HEADS:
                y = y * (HEAD_DIM ** -0.5)
        act_ref[:, cols] = y

    ab = abc_ref[0]
    g_col = -jnp.exp(alc_ref[...]) * _softplus(ab + dtc_ref[...])
    beta_col = _sigmoid(ab)
    ltri = ltri_ref[...]
    gc_col = sum(_dot(ltri, t) for t in _split3(g_col))
    g_row = -jnp.exp(alr_ref[...]) * _softplus(ar_ref[0, 0] + dtr_ref[...])
    ubd = ubd_ref[...]
    gc_row = sum(_dot(t, ubd) for t in _split3(g_row))

    cmask = cmask_ref[...]
    smask = smask_ref[...]
    gnw = gnw_ref[...]

    for grp in range(n_groups):
        heads = [grp * g_sz + hb for hb in range(g_sz)]
        kst = jnp.concatenate(
            [act_ref[:, GDN_QK_WIDTH + (h // rep) * HEAD_DIM:GDN_QK_WIDTH + (h // rep + 1) * HEAD_DIM] for h in heads], axis=0)
        qst = jnp.concatenate([act_ref[:, (h // rep) * HEAD_DIM:(h // rep + 1) * HEAD_DIM] for h in heads], axis=0)
        vst = jnp.concatenate(
            [act_ref[:, 2 * GDN_QK_WIDTH + h * HEAD_DIM:2 * GDN_QK_WIDTH + (h + 1) * HEAD_DIM] for h in heads], axis=0)
        gcb = jnp.concatenate([jnp.broadcast_to(gc_col[:, h:h + 1], (c_len, HEAD_DIM)) for h in heads], axis=0)
        bcb = jnp.concatenate(
            [jnp.broadcast_to(beta_col[:, GDN_V_HEADS + h:GDN_V_HEADS + h + 1], (c_len, HEAD_DIM)) for h in heads], axis=0)
        gcb2 = jnp.concatenate([gcb] * (r_len // HEAD_DIM), axis=1)
        bcb2 = jnp.concatenate([bcb] * (r_len // HEAD_DIM), axis=1)
        kst_b = kst.astype(BF16)
        gram = _dot_nt(kst_b, kst_b)
        qk = _dot_nt(qst.astype(BF16), kst_b)
        decay = jnp.exp(jnp.minimum(gcb2 - gc_row[grp:grp + 1, :], 0.0))
        aqk = qk * decay * cmask
        nm = -(gram * decay * bcb2 * smask)
        tp = nm
        xpow = nm
        n_steps = int(math.log2(c_len))
        for step in range(n_steps):
            xb = xpow.astype(BF16)
            tp = tp + _dot(xb, tp.astype(BF16))
            if step + 1 < n_steps:
                xpow = _dot(xb, xb)
        eg = jnp.exp(gcb)
        rhs = jnp.concatenate([vst * bcb, kst * (bcb * eg)], axis=1)
        sol = rhs + _dot(tp.astype(BF16), rhs.astype(BF16))
        qg = qst * eg
        us, qss = [], []
        for hb, h in enumerate(heads):
            rows = slice(hb * c_len, (hb + 1) * c_len)
            s_old = s_ref[0, h]
            wq = jnp.concatenate([sol[rows, HEAD_DIM:2 * HEAD_DIM], qg[rows]], axis=0).astype(BF16)
            r1 = _dot(wq, s_old.astype(BF16))
            u = sol[rows, 0:HEAD_DIM] - r1[0:c_len]
            qss.append(r1[c_len:2 * c_len])
            gl = gcb[(hb + 1) * c_len - 1:(hb + 1) * c_len, :]
            kd = kst[rows] * jnp.exp(gl - gcb[rows])
            s_ref[0, h] = jnp.exp(gl) * s_old + _dot_tn(kd.astype(BF16), u.astype(BF16))
            us.append(u)
        ust = jnp.concatenate(us, axis=0)
        o_intra = _dot(aqk.astype(BF16), ust.astype(BF16))
        for hb, h in enumerate(heads):
            rows = slice(hb * c_len, (hb + 1) * c_len)
            o = qss[hb] + o_intra[rows]
            o = o * lax.rsqrt(jnp.mean(o * o, axis=-1, keepdims=True) + EPS) * gnw
            zz = z_ref[0, :, h * HEAD_DIM:(h + 1) * HEAD_DIM]
            o_ref[0, :, h * HEAD_DIM:(h + 1) * HEAD_DIM] = (o * (zz * _sigmoid(zz))).astype(o_ref.dtype)


def _gdn(proj_a, ab, conv_past, s0, conv_w, a_log, dt_bias, gnw, chunk):
    b, l, _ = proj_a.shape
    n_chunks = l // chunk
    group = V7X_MXU_DIM // chunk
    n_groups = GDN_V_HEADS // group
    r_len = chunk * group
    a_row = ab[:, :, :GDN_V_HEADS].reshape(b, n_chunks, chunk, n_groups, group)
    a_row = a_row.transpose(0, 1, 3, 4, 2).reshape(b, n_chunks, n_groups, r_len)
    past8 = jnp.pad(conv_past.astype(F32), ((0, 0), (SUBLANES - (CONV_WIDTH - 1), 0), (0, 0)))
    pad16 = HEAD_DIM - GDN_V_HEADS
    alc = jnp.pad(a_log.astype(F32), (0, pad16)).reshape(1, HEAD_DIM)
    dtc = jnp.pad(dt_bias.astype(F32), (0, pad16)).reshape(1, HEAD_DIM)
    alr = jnp.repeat(a_log.astype(F32), chunk).reshape(n_groups, r_len)
    dtr = jnp.repeat(dt_bias.astype(F32), chunk).reshape(n_groups, r_len)
    const = lambda *shape: pl.BlockSpec(shape, lambda i, c: (0,) * len(shape))
    return pl.pallas_call(
        functools.partial(_gdn_kernel, chunk=chunk, group=group),
        grid=(b, n_chunks),
        in_specs=[
            pl.BlockSpec((1, chunk, CONV_CH), lambda i, c: (i, c, 0)),
            pl.BlockSpec((1, chunk, GDN_V_WIDTH), lambda i, c: (i, c, CONV_CH // GDN_V_WIDTH)),
            pl.BlockSpec((1, chunk, HEAD_DIM), lambda i, c: (i, c, 0)),
            pl.BlockSpec((1, 1, n_groups, r_len), lambda i, c: (i, c, 0, 0)),
            pl.BlockSpec((1, SUBLANES, CONV_CH), lambda i, c: (i, 0, 0)),
            pl.BlockSpec((1, GDN_V_HEADS, HEAD_DIM, HEAD_DIM), lambda i, c: (i, 0, 0, 0)),
            const(CONV_WIDTH, CONV_CH), const(1, HEAD_DIM), const(1, HEAD_DIM),
            const(n_groups, r_len), const(n_groups, r_len), const(1, HEAD_DIM),
        ],
        out_specs=[
            pl.BlockSpec((1, chunk, GDN_V_WIDTH), lambda i, c: (i, c, 0)),
            pl.BlockSpec((1, GDN_V_HEADS, HEAD_DIM, HEAD_DIM), lambda i, c: (i, 0, 0, 0)),
        ],
        out_shape=[jax.ShapeDtypeStruct((b, l, GDN_V_WIDTH), BF16),
                   jax.ShapeDtypeStruct((b, GDN_V_HEADS, HEAD_DIM, HEAD_DIM), F32)],
        scratch_shapes=[
            pltpu.VMEM((chunk + SUBLANES, CONV_CH), F32),
            pltpu.VMEM((chunk, CONV_CH), F32),
            pltpu.VMEM((chunk, chunk), BF16),
            pltpu.VMEM((r_len, r_len), BF16),
            pltpu.VMEM((r_len, r_len), F32),
            pltpu.VMEM((r_len, r_len), F32),
        ],
        compiler_params=_cparams("parallel", "arbitrary"),
        name="gated_delta_rule",
    )(proj_a, proj_a, ab, a_row, past8, s0.astype(F32), conv_w.astype(F32), alc, dtc, alr, dtr,
      gnw.astype(F32).reshape(1, HEAD_DIM))


def _sb_kernel(q_ref, kt_ref, v_ref, o_ref, *, tq, q_pos0):
    i = pl.program_id(2)
    q = q_ref[0]
    qpos0 = q_pos0 + i * tq
    row = lax.broadcasted_iota(jnp.int32, (tq, SB_BLOCK), 0)
    col = lax.broadcasted_iota(jnp.int32, (tq, SB_BLOCK), 1)
    ri = lax.broadcasted_iota(jnp.int32, (SB_BLOCK, SB_BLOCK), 0)
    ci = lax.broadcasted_iota(jnp.int32, (SB_BLOCK, SB_BLOCK), 1)
    ustrict = jnp.where(ri > ci, 1.0, 0.0).astype(BF16)
    n_masked = max(tq // SB_BLOCK, 1)
    j_full = qpos0 // SB_BLOCK

    def tile(j, later, acc, masked):
        kt = kt_ref[0, 0, j]
        vj = v_ref[0, pl.ds(pl.multiple_of(j * SB_BLOCK, SB_BLOCK), SB_BLOCK), :]
        z = _dot(q, kt) * (HEAD_DIM ** -0.5)
        ls = -_softplus(z)
        if masked:
            vis = (j * SB_BLOCK + col) < (qpos0 + row)
            lf = jnp.where(vis, ls, 0.0)
        else:
            lf = ls
        within = _dot(lf.astype(BF16), ustrict)
        p = jnp.exp((z + ls) + within + later)
        if masked:
            p = jnp.where(vis, p, 0.0)
        acc = acc + _dot(p.astype(BF16), vj)
        return later + jnp.sum(lf, axis=-1, keepdims=True), acc

    later = jnp.zeros((tq, 1), F32)
    acc = jnp.zeros((tq, HEAD_DIM), F32)
    for m in range(n_masked):
        later, acc = tile(j_full + (n_masked - 1 - m), later, acc, True)

    def cond(carry):
        return jnp.logical_and(carry[0] >= 0, jnp.max(carry[1]) > -SB_UNDERFLOW)

    def body(carry):
        later, acc = tile(carry[0], carry[1], carry[2], False)
        return carry[0] - 1, later, acc

    _, later, acc = lax.while_loop(cond, body, (j_full - 1, later, acc))
    o_ref[0] = acc.astype(o_ref.dtype)


def _stick_breaking(q, k_bf, v_bf, q_pos0, tq):
    b, lq, _ = q.shape
    lk = k_bf.shape[1]
    nkb = lk // SB_BLOCK
    kt = k_bf.reshape(b, nkb, SB_BLOCK, SB_HEADS, HEAD_DIM).transpose(0, 3, 1, 4, 2)
    return pl.pallas_call(
        functools.partial(_sb_kernel, tq=tq, q_pos0=q_pos0),
        grid=(b, SB_HEADS, lq // tq),
        in_specs=[
            pl.BlockSpec((1, tq, HEAD_DIM), lambda s, h, i: (s, i, h)),
            pl.BlockSpec((1, 1, nkb, HEAD_DIM, SB_BLOCK), lambda s, h, i: (s, h, 0, 0, 0)),
            pl.BlockSpec((1, lk, HEAD_DIM), lambda s, h, i: (s, 0, h)),
        ],
        out_specs=pl.BlockSpec((1, tq, HEAD_DIM), lambda s, h, i: (s, i, h)),
        out_shape=jax.ShapeDtypeStruct((b, lq, SB_WIDTH), BF16),
        compiler_params=_cparams("parallel", "parallel", "arbitrary"),
        name="stick_breaking",
    )(q, kt, v_bf)


def _sb_window_kernel(q_ref, k0_ref, k1_ref, k2_ref, v0_ref, v1_ref, v2_ref, o_ref, need_ref, *, tq, base, older):
    i = pl.program_id(1)
    row = lax.broadcasted_iota(jnp.int32, (tq, SB_BLOCK), 0)
    col = lax.broadcasted_iota(jnp.int32, (tq, SB_BLOCK), 1)
    ri = lax.broadcasted_iota(jnp.int32, (SB_BLOCK, SB_BLOCK), 0)
    ci = lax.broadcasted_iota(jnp.int32, (SB_BLOCK, SB_BLOCK), 1)
    ustrict = jnp.where(ri > ci, 1.0, 0.0).astype(BF16)
    k_refs = (k0_ref, k1_ref, k2_ref)
    v_refs = (v0_ref, v1_ref, v2_ref)
    worst = jnp.full((tq, 1), -jnp.inf, F32)
    for h in range(SB_HEADS):
        cols = slice(h * HEAD_DIM, (h + 1) * HEAD_DIM)
        q = q_ref[0, :, cols]
        later = jnp.zeros((tq, 1), F32)
        acc = jnp.zeros((tq, HEAD_DIM), F32)
        for w in range(SB_WINDOW):
            vis = (col < row) if w == 0 else jnp.broadcast_to(base + i - w >= 0, (tq, SB_BLOCK))
            z = _dot_nt(q, k_refs[w][0, :, cols]) * (HEAD_DIM ** -0.5)
            ls = -_softplus(z)
            lf = jnp.where(vis, ls, 0.0)
            within = _dot(lf.astype(BF16), ustrict)
            p = jnp.where(vis, jnp.exp((z + ls) + within + later), 0.0)
            acc = acc + _dot(p.astype(BF16), v_refs[w][0, :, cols])
            later = later + jnp.sum(lf, axis=-1, keepdims=True)
        o_ref[0, :, cols] = acc.astype(o_ref.dtype)
        worst = jnp.maximum(worst, later)
    has_older = base + i + older >= SB_WINDOW
    need_ref[...] = jnp.broadcast_to(jnp.where(has_older, jnp.max(worst), -jnp.inf), need_ref.shape)


def _stick_breaking_window(q, k_bf, v_bf, tq, base, older):
    b, lq, _ = q.shape
    nq = lq // tq
    qspec = pl.BlockSpec((1, tq, SB_WIDTH), lambda s, i: (s, i, 0))
    kspecs = [pl.BlockSpec((1, SB_BLOCK, SB_WIDTH), functools.partial(
        lambda s, i, w: (s, jnp.maximum(base + i - w, 0), 0), w=w)) for w in range(SB_WINDOW)]
    return pl.pallas_call(
        functools.partial(_sb_window_kernel, tq=tq, base=base, older=older),
        grid=(b, nq),
        in_specs=[qspec] + kspecs + kspecs,
        out_specs=[qspec, pl.BlockSpec((1, 1, SUBLANES, HEAD_DIM), lambda s, i: (s, i, 0, 0))],
        out_shape=[jax.ShapeDtypeStruct((b, lq, SB_WIDTH), BF16),
                   jax.ShapeDtypeStruct((b, nq, SUBLANES, HEAD_DIM), F32)],
        compiler_params=_cparams("parallel", "arbitrary"),
        name="stick_breaking_window",
    )(q, k_bf, k_bf, k_bf, v_bf, v_bf, v_bf)


def _layer(x, conv_past, s0, past_k, past_v, chunk, wts):
    b, l, d = x.shape
    m = b * l
    x2 = x.reshape(m, d)
    xn = _rmsnorm(x2, wts["norm1_w"], BF16)
    w_in = wts["w_in"]
    c_q = OFF_A
    c_g = c_q + 3 * SB_WIDTH
    c_ab = c_g + 2 * d
    (proj_a,) = _matmul(xn, w_in, (F32,), 0, OFF_A)
    (ab,) = _matmul(xn, w_in, (F32,), c_ab, HEAD_DIM)
    (q_sb,) = _matmul(xn, w_in, (BF16,), c_q, SB_WIDTH)
    k_sb, k_bf = _matmul(xn, w_in, (F32, BF16), c_q + SB_WIDTH, SB_WIDTH)
    v_sb, v_bf = _matmul(xn, w_in, (F32, BF16), c_q + 2 * SB_WIDTH, SB_WIDTH)
    (gates,) = _matmul(xn, w_in, (F32,), c_g, 2 * d)

    o_a, s_new = _gdn(proj_a.reshape(b, l, -1), ab.reshape(b, l, -1), conv_past, s0, wts["conv_w"], wts["A_log"],
                      wts["dt_bias"], wts["gdn_norm_w"], chunk)
    conv_state = proj_a.reshape(b, l, -1)[:, l - (CONV_WIDTH - 1):, :CONV_CH]

    k_bf = k_bf.reshape(b, l, SB_WIDTH)
    v_bf = v_bf.reshape(b, l, SB_WIDTH)
    q3 = q_sb.reshape(b, l, SB_WIDTH)
    if past_k is None:
        o_win, need = _stick_breaking_window(q3, k_bf, v_bf, min(l, SB_BLOCK), 0, 0)

        def full_sweep():
            return _stick_breaking(q3, k_bf, v_bf, 0, min(l, 256))
    else:
        p = past_k.shape[1]
        n_cached = (SB_WINDOW - 1) * SB_BLOCK
        assert p % SB_BLOCK == 0 and p >= n_cached and l <= SB_BLOCK

        def with_cache(new, past, keep):
            return jnp.concatenate([past[:, p - keep:].reshape(b, keep, SB_WIDTH).astype(BF16), new,
                                    jnp.zeros((b, SB_BLOCK - l, SB_WIDTH), BF16)], axis=1)

        o_win, need = _stick_breaking_window(q3, with_cache(k_bf, past_k, n_cached), with_cache(v_bf, past_v, n_cached),
                                             l, SB_WINDOW - 1, (p - n_cached) // SB_BLOCK)

        def full_sweep():
            return _stick_breaking(q3, with_cache(k_bf, past_k, p), with_cache(v_bf, past_v, p), p, l)
    o_b = lax.cond(jnp.max(need) > -SB_UNDERFLOW, full_sweep, lambda: o_win)

    merged = _merge(o_a.reshape(m, GDN_V_WIDTH), o_b.reshape(m, SB_WIDTH), wts["w_gdn_o"], wts["w_sb_o"], gates)
    h = _matmul_residual(merged, wts["w_out"], x2)
    hn = _rmsnorm(h, wts["norm2_w"], BF16)
    hid = _matmul_relu2(hn, wts["w_up"])
    h = _matmul_residual_ktiled(hid, wts["w_down"], h)
    return (h, conv_state, s_new, k_sb.reshape(b, l, SB_HEADS, HEAD_DIM), v_sb.reshape(b, l, SB_HEADS, HEAD_DIM))


def kernel(x_prompt, x_sample, cache_sb_k, cache_sb_v, state_gdn_S, state_gdn_conv, norm1_w, w_in, conv_w, A_log,
           dt_bias, gdn_norm_w, w_gdn_o, w_sb_o, w_out, norm2_w, w_up, w_down, final_norm_w):
    depth = w_in.shape[0]
    assert depth == 1
    w_in0 = w_in[0]
    ab_cols = jnp.pad(w_in0[:, OFF_A:OFF_SB], ((0, 0), (0, HEAD_DIM - 2 * GDN_V_HEADS)))
    wts = {
        "norm1_w": norm1_w[0], "norm2_w": norm2_w[0], "conv_w": conv_w[0], "A_log": A_log[0], "dt_bias": dt_bias[0],
        "gdn_norm_w": gdn_norm_w[0],
        "w_in": jnp.concatenate([w_in0[:, :OFF_A], w_in0[:, OFF_SB:], ab_cols], axis=1).astype(BF16),
        "w_gdn_o": w_gdn_o[0].astype(BF16), "w_sb_o": w_sb_o[0].astype(BF16), "w_out": w_out[0].astype(BF16),
        "w_up": w_up[0].astype(BF16), "w_down": w_down[0].astype(BF16),
    }
    bp, lp, d = x_prompt.shape
    bs, ls, _ = x_sample.shape
    conv0 = jnp.zeros((bp, CONV_WIDTH - 1, CONV_CH), F32)
    s_zero = jnp.zeros((bp, GDN_V_HEADS, HEAD_DIM, HEAD_DIM), F32)
    hp, c_p, s_p, k_p, v_p = _layer(x_prompt, conv0, s_zero, None, None, min(PROMPT_CHUNK, lp), wts)
    hs, c_s, s_s, k_s, v_s = _layer(x_sample, state_gdn_conv[0], state_gdn_S[0], cache_sb_k[0], cache_sb_v[0], ls, wts)
    y_prompt = _rmsnorm(hp, final_norm_w, F32).reshape(bp, lp, d)
    y_sample = _rmsnorm(hs, final_norm_w, F32).reshape(bs, ls, d)
    return (y_prompt, y_sample, k_p[None], v_p[None], s_p[None], c_p[None],
            k_s[None], v_s[None], s_s[None].astype(state_gdn_S.dtype), c_s[None])
```

```python
import functools
import math

import jax
import jax.numpy as jnp
from jax import lax
from jax.experimental import pallas as pl
from jax.experimental.pallas import tpu as pltpu

F32 = jnp.float32
BF16 = jnp.bfloat16

EPS = 1e-6
HEAD_DIM = 128
GDN_QK_HEADS = 8
GDN_V_HEADS = 16
GDN_QK_WIDTH = GDN_QK_HEADS * HEAD_DIM
GDN_V_WIDTH = GDN_V_HEADS * HEAD_DIM
CONV_WIDTH = 4
CONV_CH = 2 * GDN_QK_WIDTH + GDN_V_WIDTH
SB_HEADS = 16
SB_WIDTH = SB_HEADS * HEAD_DIM
SB_BLOCK = 128
SB_UNDERFLOW = 106.0
SB_WINDOW = 3
SB_HEAD_BATCH = 4
PROMPT_CHUNK = 64

V7X_MXU_DIM = 256
V7X_VMEM_LIMIT_BYTES = 56 * 1024 * 1024
SUBLANES = 8

OFF_Z = CONV_CH
OFF_A = OFF_Z + GDN_V_WIDTH
OFF_B = OFF_A + GDN_V_HEADS
OFF_SB = OFF_B + GDN_V_HEADS
OFF_GATE = OFF_SB + 3 * SB_WIDTH


def _cparams(*sem):
    return pltpu.CompilerParams(dimension_semantics=sem, vmem_limit_bytes=V7X_VMEM_LIMIT_BYTES)


def _dot(a, b):
    return jnp.dot(a, b, preferred_element_type=F32)


def _dot_nt(a, b):
    return lax.dot_general(a, b, (((1,), (1,)), ((), ())), preferred_element_type=F32)


def _dot_tn(a, b):
    return lax.dot_general(a, b, (((0,), (0,)), ((), ())), preferred_element_type=F32)


def _sigmoid(x):
    return 1.0 / (1.0 + jnp.exp(-x))


def _softplus(x):
    return jnp.maximum(x, 0.0) + jnp.log1p(jnp.exp(-jnp.abs(x)))


def _split3(x):
    x1 = x.astype(BF16)
    r1 = x - x1.astype(F32)
    x2 = r1.astype(BF16)
    x3 = (r1 - x2.astype(F32)).astype(BF16)
    return x1, x2, x3


def _rmsnorm_kernel(x_ref, w_ref, o_ref):
    x = x_ref[...]
    var = jnp.mean(x * x, axis=-1, keepdims=True)
    o_ref[...] = (x * lax.rsqrt(var + EPS) * w_ref[...]).astype(o_ref.dtype)


def _rmsnorm(x, w, out_dtype):
    m, d = x.shape
    tm = min(m, 256)
    return pl.pallas_call(
        _rmsnorm_kernel,
        grid=(m // tm,),
        in_specs=[pl.BlockSpec((tm, d), lambda i: (i, 0)), pl.BlockSpec((1, d), lambda i: (0, 0))],
        out_specs=pl.BlockSpec((tm, d), lambda i: (i, 0)),
        out_shape=jax.ShapeDtypeStruct((m, d), out_dtype),
        compiler_params=_cparams("parallel"),
        name="rmsnorm",
    )(x, w.reshape(1, d).astype(F32))


def _mm_tiles(m, n, tn_max=1024):
    tm = min(m, 1024)
    tn = min(n, tn_max)
    assert m % tm == 0 and n % tn == 0, (m, n)
    return tm, tn


def _mm_kernel(a_ref, w_ref, *o_refs):
    acc = _dot(a_ref[...], w_ref[...])
    for o_ref in o_refs:
        o_ref[...] = acc.astype(o_ref.dtype)


def _matmul(a, w, out_dtypes, col0=0, n=None):
    m, k = a.shape
    n = w.shape[1] if n is None else n
    tm, tn = _mm_tiles(m, n)
    assert col0 % tn == 0
    jb = col0 // tn
    outs = pl.pallas_call(
        _mm_kernel,
        grid=(m // tm, n // tn),
        in_specs=[pl.BlockSpec((tm, k), lambda i, j: (i, 0)), pl.BlockSpec((k, tn), lambda i, j: (0, j + jb))],
        out_specs=[pl.BlockSpec((tm, tn), lambda i, j: (i, j)) for _ in out_dtypes],
        out_shape=[jax.ShapeDtypeStruct((m, n), dt) for dt in out_dtypes],
        compiler_params=_cparams("parallel", "arbitrary"),
        name="matmul",
    )(a, w)
    return outs


def _mm_relu2_kernel(a_ref, w_ref, o_ref):
    acc = _dot(a_ref[...], w_ref[...])
    o_ref[...] = jnp.square(jnp.maximum(acc, 0.0)).astype(o_ref.dtype)


def _matmul_relu2(a, w):
    m, k = a.shape
    n = w.shape[1]
    tm, tn = _mm_tiles(m, n)
    return pl.pallas_call(
        _mm_relu2_kernel,
        grid=(m // tm, n // tn),
        in_specs=[pl.BlockSpec((tm, k), lambda i, j: (i, 0)), pl.BlockSpec((k, tn), lambda i, j: (0, j))],
        out_specs=pl.BlockSpec((tm, tn), lambda i, j: (i, j)),
        out_shape=jax.ShapeDtypeStruct((m, n), BF16),
        compiler_params=_cparams("parallel", "arbitrary"),
        name="matmul_relu2",
    )(a, w)


def _mm_res_kernel(a_ref, w_ref, r_ref, o_ref):
    o_ref[...] = r_ref[...] + _dot(a_ref[...], w_ref[...])


def _matmul_residual(a, w, res):
    m, k = a.shape
    n = w.shape[1]
    tm, tn = _mm_tiles(m, n)
    return pl.pallas_call(
        _mm_res_kernel,
        grid=(m // tm, n // tn),
        in_specs=[pl.BlockSpec((tm, k), lambda i, j: (i, 0)), pl.BlockSpec((k, tn), lambda i, j: (0, j)),
                  pl.BlockSpec((tm, tn), lambda i, j: (i, j))],
        out_specs=pl.BlockSpec((tm, tn), lambda i, j: (i, j)),
        out_shape=jax.ShapeDtypeStruct((m, n), F32),
        compiler_params=_cparams("parallel", "arbitrary"),
        name="matmul_residual",
    )(a, w, res)


def _mm_kacc_kernel(a_ref, w_ref, r_ref, o_ref, acc_ref):
    kk = pl.program_id(2)

    @pl.when(kk == 0)
    def _():
        acc_ref[...] = jnp.zeros_like(acc_ref)

    acc_ref[...] += _dot(a_ref[...], w_ref[...])

    @pl.when(kk == pl.num_programs(2) - 1)
    def _():
        o_ref[...] = r_ref[...] + acc_ref[...]


def _matmul_residual_ktiled(a, w, res):
    m, k = a.shape
    n = w.shape[1]
    tm = min(m, 1024)
    tn = min(n, 1024)
    tk = min(k, 2048)
    return pl.pallas_call(
        _mm_kacc_kernel,
        grid=(m // tm, n // tn, k // tk),
        in_specs=[pl.BlockSpec((tm, tk), lambda i, j, l: (i, l)), pl.BlockSpec((tk, tn), lambda i, j, l: (l, j)),
                  pl.BlockSpec((tm, tn), lambda i, j, l: (i, j))],
        out_specs=pl.BlockSpec((tm, tn), lambda i, j, l: (i, j)),
        out_shape=jax.ShapeDtypeStruct((m, n), F32),
        scratch_shapes=[pltpu.VMEM((tm, tn), F32)],
        compiler_params=_cparams("parallel", "parallel", "arbitrary"),
        name="matmul_residual_ktiled",
    )(a, w, res)


def _merge_kernel(oa_ref, ob_ref, wa_ref, wb_ref, ga_ref, gb_ref, o_ref):
    ya = _dot(oa_ref[...], wa_ref[...])
    yb = _dot(ob_ref[...], wb_ref[...])
    o_ref[...] = (_sigmoid(ga_ref[...]) * ya + _sigmoid(gb_ref[...]) * yb).astype(o_ref.dtype)


def _merge(oa, ob, wa, wb, gates):
    m, k = oa.shape
    n = wa.shape[1]
    tm, tn = _mm_tiles(m, n, 512)
    nb = n // tn
    return pl.pallas_call(
        _merge_kernel,
        grid=(m // tm, nb),
        in_specs=[pl.BlockSpec((tm, k), lambda i, j: (i, 0)), pl.BlockSpec((tm, k), lambda i, j: (i, 0)),
                  pl.BlockSpec((k, tn), lambda i, j: (0, j)), pl.BlockSpec((k, tn), lambda i, j: (0, j)),
                  pl.BlockSpec((tm, tn), lambda i, j: (i, j)), pl.BlockSpec((tm, tn), lambda i, j: (i, j + nb))],
        out_specs=pl.BlockSpec((tm, tn), lambda i, j: (i, j)),
        out_shape=jax.ShapeDtypeStruct((m, n), BF16),
        compiler_params=_cparams("parallel", "arbitrary"),
        name="merge",
    )(oa, ob, wa, wb, gates, gates)


def _gdn_kernel(qkv_ref, z_ref, abc_ref, ar_ref, past_ref, s0_ref, cw_ref, alc_ref, dtc_ref, alr_ref, dtr_ref,
                gnw_ref, o_ref, s_ref, xp_ref, act_ref, ltri_ref, ubd_ref, cmask_ref, smask_ref, *, chunk, group):
    c_len, g_sz = chunk, group
    r_len = c_len * g_sz
    n_groups = GDN_V_HEADS // g_sz
    rep = GDN_V_HEADS // GDN_QK_HEADS
    c_idx = pl.program_id(1)

    @pl.when(c_idx == 0)
    def _init():
        s_ref[...] = s0_ref[...]
        xp_ref[0:SUBLANES, :] = past_ref[0]
        ri = lax.broadcasted_iota(jnp.int32, (r_len, r_len), 0)
        ci = lax.broadcasted_iota(jnp.int32, (r_len, r_len), 1)
        shift = int(math.log2(c_len))
        same = (ri >> shift) == (ci >> shift)
        cmask_ref[...] = jnp.where(same & (ri >= ci), 1.0, 0.0)
        smask_ref[...] = jnp.where(same & (ri > ci), 1.0, 0.0)
        ubd_ref[...] = jnp.where(same & (ri <= ci), 1.0, 0.0).astype(BF16)
        rc = lax.broadcasted_iota(jnp.int32, (c_len, c_len), 0)
        cc = lax.broadcasted_iota(jnp.int32, (c_len, c_len), 1)
        ltri_ref[...] = jnp.where(rc >= cc, 1.0, 0.0).astype(BF16)

    @pl.when(c_idx > 0)
    def _carry():
        xp_ref[0:SUBLANES, :] = xp_ref[c_len:c_len + SUBLANES, :]

    xp_ref[SUBLANES:SUBLANES + c_len, :] = qkv_ref[0]

    for s in range(CONV_CH // HEAD_DIM):
        cols = slice(s * HEAD_DIM, (s + 1) * HEAD_DIM)
        y = cw_ref[0:1, cols] * xp_ref[SUBLANES - 3:SUBLANES - 3 + c_len, cols]
        for i in range(1, CONV_WIDTH):
            y = y + cw_ref[i:i + 1, cols] * xp_ref[SUBLANES - 3 + i:SUBLANES - 3 + i + c_len, cols]
        y = y * _sigmoid(y)
        if s < 2 * GDN_QK_HEADS:
            y = y * lax.rsqrt(jnp.sum(y * y, axis=-1, keepdims=True) + EPS)
            if s < GDN_QK_HEADS:
                y = y * (HEAD_DIM ** -0.5)
        act_ref[:, cols] = y

    ab = abc_ref[0]
    g_col = -jnp.exp(alc_ref[...]) * _softplus(ab + dtc_ref[...])
    beta_col = _sigmoid(ab)
    ltri = ltri_ref[...]
    gc_col = sum(_dot(ltri, t) for t in _split3(g_col))
    g_row = -jnp.exp(alr_ref[...]) * _softplus(ar_ref[0, 0] + dtr_ref[...])
    ubd = ubd_ref[...]
    gc_row = sum(_dot(t, ubd) for t in _split3(g_row))

    cmask = cmask_ref[...]
    smask = smask_ref[...]
    gnw = gnw_ref[...]

    groups = range(n_groups)
    heads_of = [[grp * g_sz + hb for hb in range(g_sz)] for grp in groups]

    def stacked(off, div):
        return [jnp.concatenate([act_ref[:, off + (h // div) * HEAD_DIM:off + (h // div + 1) * HEAD_DIM]
                                 for h in heads_of[grp]], axis=0) for grp in groups]

    kst = stacked(GDN_QK_WIDTH, rep)
    qst = stacked(0, rep)
    vst = stacked(2 * GDN_QK_WIDTH, 1)
    gcb = [jnp.concatenate([jnp.broadcast_to(gc_col[:, h:h + 1], (c_len, HEAD_DIM)) for h in heads_of[grp]], axis=0)
           for grp in groups]
    bcb = [jnp.concatenate([jnp.broadcast_to(beta_col[:, GDN_V_HEADS + h:GDN_V_HEADS + h + 1], (c_len, HEAD_DIM))
                            for h in heads_of[grp]], axis=0) for grp in groups]
    kst_b = [t.astype(BF16) for t in kst]
    gram = [_dot_nt(kst_b[grp], kst_b[grp]) for grp in groups]
    qk = [_dot_nt(qst[grp].astype(BF16), kst_b[grp]) for grp in groups]
    n_rep = r_len // HEAD_DIM
    aqk, tp, xpow = [], [], []
    for grp in groups:
        gcb2 = jnp.concatenate([gcb[grp]] * n_rep, axis=1)
        nbeta = -bcb[grp]
        nbeta2 = jnp.concatenate([nbeta] * n_rep, axis=1) * smask
        decay = jnp.exp(jnp.minimum(gcb2 - gc_row[grp:grp + 1, :], 0.0)) * cmask
        aqk.append((qk[grp] * decay).astype(BF16))
        nm = (gram[grp] * decay) * nbeta2
        tp.append(nm)
        xpow.append(nm)
    n_steps = int(math.log2(c_len))
    for step in range(n_steps):
        for grp in groups:
            xb = xpow[grp].astype(BF16)
            tp[grp] = tp[grp] + _dot(xb, tp[grp].astype(BF16))
            if step + 1 < n_steps:
                xpow[grp] = _dot(xb, xb)
    eg = [jnp.exp(gcb[grp]) for grp in groups]
    sol = []
    for grp in groups:
        rhs = jnp.concatenate([vst[grp] * bcb[grp], kst[grp] * (bcb[grp] * eg[grp])], axis=1)
        sol.append(rhs + _dot(tp[grp].astype(BF16), rhs.astype(BF16)))
    qss = {}
    ust = []
    for grp in groups:
        qg = qst[grp] * eg[grp]
        us = []
        for hb, h in enumerate(heads_of[grp]):
            rows = slice(hb * c_len, (hb + 1) * c_len)
            s_old = s_ref[0, h]
            wq = jnp.concatenate([sol[grp][rows, HEAD_DIM:2 * HEAD_DIM], qg[rows]], axis=0).astype(BF16)
            r1 = _dot(wq, s_old.astype(BF16))
            u = sol[grp][rows, 0:HEAD_DIM] - r1[0:c_len]
            qss[h] = r1[c_len:2 * c_len]
            gl = gcb[grp][(hb + 1) * c_len - 1:(hb + 1) * c_len, :]
            kd = kst[grp][rows] * jnp.exp(gl - gcb[grp][rows])
            s_ref[0, h] = jnp.exp(gl) * s_old + _dot_tn(kd.astype(BF16), u.astype(BF16))
            us.append(u)
        ust.append(jnp.concatenate(us, axis=0).astype(BF16))
    o_intra = [_dot(aqk[grp], ust[grp]) for grp in groups]
    for grp in groups:
        for hb, h in enumerate(heads_of[grp]):
            rows = slice(hb * c_len, (hb + 1) * c_len)
            o = qss[h] + o_intra[grp][rows]
            o = o * lax.rsqrt(jnp.mean(o * o, axis=-1, keepdims=True) + EPS) * gnw
            zz = z_ref[0, :, h * HEAD_DIM:(h + 1) * HEAD_DIM]
            o_ref[0, :, h * HEAD_DIM:(h + 1) * HEAD_DIM] = (o * (zz * _sigmoid(zz))).astype(o_ref.dtype)


def _gdn(proj_a, ab, conv_past, s0, conv_w, a_log, dt_bias, gnw, chunk):
    b, l, _ = proj_a.shape
    n_chunks = l // chunk
    group = V7X_MXU_DIM // chunk
    n_groups = GDN_V_HEADS // group
    r_len = chunk * group
    a_row = ab[:, :, :GDN_V_HEADS].reshape(b, n_chunks, chunk, n_groups, group)
    a_row = a_row.transpose(0, 1, 3, 4, 2).reshape(b, n_chunks, n_groups, r_len)
    past8 = jnp.pad(conv_past.astype(F32), ((0, 0), (SUBLANES - (CONV_WIDTH - 1), 0), (0, 0)))
    pad16 = HEAD_DIM - GDN_V_HEADS
    alc = jnp.pad(a_log.astype(F32), (0, pad16)).reshape(1, HEAD_DIM)
    dtc = jnp.pad(dt_bias.astype(F32), (0, pad16)).reshape(1, HEAD_DIM)
    alr = jnp.repeat(a_log.astype(F32), chunk).reshape(n_groups, r_len)
    dtr = jnp.repeat(dt_bias.astype(F32), chunk).reshape(n_groups, r_len)
    const = lambda *shape: pl.BlockSpec(shape, lambda i, c: (0,) * len(shape))
    return pl.pallas_call(
        functools.partial(_gdn_kernel, chunk=chunk, group=group),
        grid=(b, n_chunks),
        in_specs=[
            pl.BlockSpec((1, chunk, CONV_CH), lambda i, c: (i, c, 0)),
            pl.BlockSpec((1, chunk, GDN_V_WIDTH), lambda i, c: (i, c, CONV_CH // GDN_V_WIDTH)),
            pl.BlockSpec((1, chunk, HEAD_DIM), lambda i, c: (i, c, 0)),
            pl.BlockSpec((1, 1, n_groups, r_len), lambda i, c: (i, c, 0, 0)),
            pl.BlockSpec((1, SUBLANES, CONV_CH), lambda i, c: (i, 0, 0)),
            pl.BlockSpec((1, GDN_V_HEADS, HEAD_DIM, HEAD_DIM), lambda i, c: (i, 0, 0, 0)),
            const(CONV_WIDTH, CONV_CH), const(1, HEAD_DIM), const(1, HEAD_DIM),
            const(n_groups, r_len), const(n_groups, r_len), const(1, HEAD_DIM),
        ],
        out_specs=[
            pl.BlockSpec((1, chunk, GDN_V_WIDTH), lambda i, c: (i, c, 0)),
            pl.BlockSpec((1, GDN_V_HEADS, HEAD_DIM, HEAD_DIM), lambda i, c: (i, 0, 0, 0)),
        ],
        out_shape=[jax.ShapeDtypeStruct((b, l, GDN_V_WIDTH), BF16),
                   jax.ShapeDtypeStruct((b, GDN_V_HEADS, HEAD_DIM, HEAD_DIM), F32)],
        scratch_shapes=[
            pltpu.VMEM((chunk + SUBLANES, CONV_CH), F32),
            pltpu.VMEM((chunk, CONV_CH), F32),
            pltpu.VMEM((chunk, chunk), BF16),
            pltpu.VMEM((r_len, r_len), BF16),
            pltpu.VMEM((r_len, r_len), F32),
            pltpu.VMEM((r_len, r_len), F32),
        ],
        compiler_params=_cparams("parallel", "arbitrary"),
        name="gated_delta_rule",
    )(proj_a, proj_a, ab, a_row, past8, s0.astype(F32), conv_w.astype(F32), alc, dtc, alr, dtr,
      gnw.astype(F32).reshape(1, HEAD_DIM))


def _sb_kernel(q_ref, kt_ref, v_ref, o_ref, *, tq, q_pos0):
    i = pl.program_id(2)
    q = q_ref[0]
    qpos0 = q_pos0 + i * tq
    row = lax.broadcasted_iota(jnp.int32, (tq, SB_BLOCK), 0)
    col = lax.broadcasted_iota(jnp.int32, (tq, SB_BLOCK), 1)
    ri = lax.broadcasted_iota(jnp.int32, (SB_BLOCK, SB_BLOCK), 0)
    ci = lax.broadcasted_iota(jnp.int32, (SB_BLOCK, SB_BLOCK), 1)
    ustrict = jnp.where(ri > ci, 1.0, 0.0).astype(BF16)
    n_masked = max(tq // SB_BLOCK, 1)
    j_full = qpos0 // SB_BLOCK

    def tile(j, later, acc, masked):
        kt = kt_ref[0, 0, j]
        vj = v_ref[0, pl.ds(pl.multiple_of(j * SB_BLOCK, SB_BLOCK), SB_BLOCK), :]
        z = _dot(q, kt) * (HEAD_DIM ** -0.5)
        ls = -_softplus(z)
        if masked:
            vis = (j * SB_BLOCK + col) < (qpos0 + row)
            lf = jnp.where(vis, ls, 0.0)
        else:
            lf = ls
        within = _dot(lf.astype(BF16), ustrict)
        p = jnp.exp((z + ls) + within + later)
        if masked:
            p = jnp.where(vis, p, 0.0)
        acc = acc + _dot(p.astype(BF16), vj)
        return later + jnp.sum(lf, axis=-1, keepdims=True), acc

    later = jnp.zeros((tq, 1), F32)
    acc = jnp.zeros((tq, HEAD_DIM), F32)
    for m in range(n_masked):
        later, acc = tile(j_full + (n_masked - 1 - m), later, acc, True)

    def cond(carry):
        return jnp.logical_and(carry[0] >= 0, jnp.max(carry[1]) > -SB_UNDERFLOW)

    def body(carry):
        later, acc = tile(carry[0], carry[1], carry[2], False)
        return carry[0] - 1, later, acc

    _, later, acc = lax.while_loop(cond, body, (j_full - 1, later, acc))
    o_ref[0] = acc.astype(o_ref.dtype)


def _stick_breaking(q, k_bf, v_bf, q_pos0, tq):
    b, lq, _ = q.shape
    lk = k_bf.shape[1]
    nkb = lk // SB_BLOCK
    kt = k_bf.reshape(b, nkb, SB_BLOCK, SB_HEADS, HEAD_DIM).transpose(0, 3, 1, 4, 2)
    return pl.pallas_call(
        functools.partial(_sb_kernel, tq=tq, q_pos0=q_pos0),
        grid=(b, SB_HEADS, lq // tq),
        in_specs=[
            pl.BlockSpec((1, tq, HEAD_DIM), lambda s, h, i: (s, i, h)),
            pl.BlockSpec((1, 1, nkb, HEAD_DIM, SB_BLOCK), lambda s, h, i: (s, h, 0, 0, 0)),
            pl.BlockSpec((1, lk, HEAD_DIM), lambda s, h, i: (s, 0, h)),
        ],
        out_specs=pl.BlockSpec((1, tq, HEAD_DIM), lambda s, h, i: (s, i, h)),
        out_shape=jax.ShapeDtypeStruct((b, lq, SB_WIDTH), BF16),
        compiler_params=_cparams("parallel", "parallel", "arbitrary"),
        name="stick_breaking",
    )(q, kt, v_bf)


def _sb_window_kernel(q_ref, k0_ref, k1_ref, k2_ref, v0_ref, v1_ref, v2_ref, o_ref, need_ref, *, tq, base, older):
    i = pl.program_id(1)
    row = lax.broadcasted_iota(jnp.int32, (tq, SB_WINDOW * SB_BLOCK), 0)
    col = lax.broadcasted_iota(jnp.int32, (tq, SB_WINDOW * SB_BLOCK), 1)
    ri = lax.broadcasted_iota(jnp.int32, (SB_BLOCK, SB_BLOCK), 0)
    ci = lax.broadcasted_iota(jnp.int32, (SB_BLOCK, SB_BLOCK), 1)
    ustrict = jnp.where(ri > ci, 1.0, 0.0).astype(BF16)
    k_refs = (k0_ref, k1_ref, k2_ref)
    v_refs = (v0_ref, v1_ref, v2_ref)
    n_win = SB_WINDOW * SB_BLOCK
    limit = row
    for w in range(1, SB_WINDOW):
        limit = jnp.where(col < w * SB_BLOCK, limit, jnp.where(base + i - w >= 0, n_win, 0))
    vis = col < limit
    worst = jnp.full((tq, 1), -jnp.inf, F32)
    for h0 in range(0, SB_HEADS, SB_HEAD_BATCH):
        batch = range(h0, h0 + SB_HEAD_BATCH)
        cols = {h: slice(h * HEAD_DIM, (h + 1) * HEAD_DIM) for h in batch}
        z = {h: _dot_nt(q_ref[0, :, cols[h]], jnp.concatenate([r[0, :, cols[h]] for r in k_refs], axis=0))
             * (HEAD_DIM ** -0.5) for h in batch}
        ls = {h: -_softplus(z[h]) for h in batch}
        lf = {h: jnp.where(vis, ls[h], 0.0) for h in batch}
        within = {h: [_dot(lf[h][:, w * SB_BLOCK:(w + 1) * SB_BLOCK].astype(BF16), ustrict) for w in range(SB_WINDOW)]
                  for h in batch}
        p = {}
        for h in batch:
            later = jnp.zeros((tq, 1), F32)
            shifted = []
            for w in range(SB_WINDOW):
                shifted.append(within[h][w] + later)
                later = later + jnp.sum(lf[h][:, w * SB_BLOCK:(w + 1) * SB_BLOCK], axis=-1, keepdims=True)
            worst = jnp.maximum(worst, later)
            p[h] = jnp.where(vis, jnp.exp((z[h] + ls[h]) + jnp.concatenate(shifted, axis=1)), 0.0).astype(BF16)
        for h in batch:
            acc = _dot(p[h], jnp.concatenate([r[0, :, cols[h]] for r in v_refs], axis=0))
            o_ref[0, :, cols[h]] = acc.astype(o_ref.dtype)
    has_older = base + i + older >= SB_WINDOW
    need_ref[...] = jnp.broadcast_to(jnp.where(has_older, jnp.max(worst), -jnp.inf), need_ref.shape)


def _stick_breaking_window(q, k_bf, v_bf, tq, base, older):
    b, lq, _ = q.shape
    nq = lq // tq
    qspec = pl.BlockSpec((1, tq, SB_WIDTH), lambda s, i: (s, i, 0))
    kspecs = [pl.BlockSpec((1, SB_BLOCK, SB_WIDTH), functools.partial(
        lambda s, i, w: (s, jnp.maximum(base + i - w, 0), 0), w=w)) for w in range(SB_WINDOW)]
    return pl.pallas_call(
        functools.partial(_sb_window_kernel, tq=tq, base=base, older=older),
        grid=(b, nq),
        in_specs=[qspec] + kspecs + kspecs,
        out_specs=[qspec, pl.BlockSpec((1, 1, SUBLANES, HEAD_DIM), lambda s, i: (s, i, 0, 0))],
        out_shape=[jax.ShapeDtypeStruct((b, lq, SB_WIDTH), BF16),
                   jax.ShapeDtypeStruct((b, nq, SUBLANES, HEAD_DIM), F32)],
        compiler_params=_cparams("parallel", "arbitrary"),
        name="stick_breaking_window",
    )(q, k_bf, k_bf, k_bf, v_bf, v_bf, v_bf)


def _layer(x, conv_past, s0, past_k, past_v, chunk, wts):
    b, l, d = x.shape
    m = b * l
    x2 = x.reshape(m, d)
    xn = _rmsnorm(x2, wts["norm1_w"], BF16)
    w_in = wts["w_in"]
    c_q = OFF_A
    c_g = c_q + 3 * SB_WIDTH
    c_ab = c_g + 2 * d
    (proj_a,) = _matmul(xn, w_in, (F32,), 0, OFF_A)
    (ab,) = _matmul(xn, w_in, (F32,), c_ab, HEAD_DIM)
    (q_sb,) = _matmul(xn, w_in, (BF16,), c_q, SB_WIDTH)
    k_sb, k_bf = _matmul(xn, w_in, (F32, BF16), c_q + SB_WIDTH, SB_WIDTH)
    v_sb, v_bf = _matmul(xn, w_in, (F32, BF16), c_q + 2 * SB_WIDTH, SB_WIDTH)
    (gates,) = _matmul(xn, w_in, (F32,), c_g, 2 * d)

    o_a, s_new = _gdn(proj_a.reshape(b, l, -1), ab.reshape(b, l, -1), conv_past, s0, wts["conv_w"], wts["A_log"],
                      wts["dt_bias"], wts["gdn_norm_w"], chunk)
    conv_state = proj_a.reshape(b, l, -1)[:, l - (CONV_WIDTH - 1):, :CONV_CH]

    k_bf = k_bf.reshape(b, l, SB_WIDTH)
    v_bf = v_bf.reshape(b, l, SB_WIDTH)
    q3 = q_sb.reshape(b, l, SB_WIDTH)
    if past_k is None:
        o_win, need = _stick_breaking_window(q3, k_bf, v_bf, min(l, SB_BLOCK), 0, 0)

        def full_sweep():
            return _stick_breaking(q3, k_bf, v_bf, 0, min(l, 256))
    else:
        p = past_k.shape[1]
        n_cached = (SB_WINDOW - 1) * SB_BLOCK
        assert p % SB_BLOCK == 0 and p >= n_cached and l <= SB_BLOCK

        def with_cache(new, past, keep):
            return jnp.concatenate([past[:, p - keep:].reshape(b, keep, SB_WIDTH).astype(BF16), new,
                                    jnp.zeros((b, SB_BLOCK - l, SB_WIDTH), BF16)], axis=1)

        o_win, need = _stick_breaking_window(q3, with_cache(k_bf, past_k, n_cached), with_cache(v_bf, past_v, n_cached),
                                             l, SB_WINDOW - 1, (p - n_cached) // SB_BLOCK)

        def full_sweep():
            return _stick_breaking(q3, with_cache(k_bf, past_k, p), with_cache(v_bf, past_v, p), p, l)
    o_b = lax.cond(jnp.max(need) > -SB_UNDERFLOW, full_sweep, lambda: o_win)

    merged = _merge(o_a.reshape(m, GDN_V_WIDTH), o_b.reshape(m, SB_WIDTH), wts["w_gdn_o"], wts["w_sb_o"], gates)
    h = _matmul_residual(merged, wts["w_out"], x2)
    hn = _rmsnorm(h, wts["norm2_w"], BF16)
    hid = _matmul_relu2(hn, wts["w_up"])
    h = _matmul_residual_ktiled(hid, wts["w_down"], h)
    return (h, conv_state, s_new, k_sb.reshape(b, l, SB_HEADS, HEAD_DIM), v_sb.reshape(b, l, SB_HEADS, HEAD_DIM))


def kernel(x_prompt, x_sample, cache_sb_k, cache_sb_v, state_gdn_S, state_gdn_conv, norm1_w, w_in, conv_w, A_log,
           dt_bias, gdn_norm_w, w_gdn_o, w_sb_o, w_out, norm2_w, w_up, w_down, final_norm_w):
    depth = w_in.shape[0]
    assert depth == 1
    w_in0 = w_in[0]
    ab_cols = jnp.pad(w_in0[:, OFF_A:OFF_SB], ((0, 0), (0, HEAD_DIM - 2 * GDN_V_HEADS)))
    wts = {
        "norm1_w": norm1_w[0], "norm2_w": norm2_w[0], "conv_w": conv_w[0], "A_log": A_log[0], "dt_bias": dt_bias[0],
        "gdn_norm_w": gdn_norm_w[0],
        "w_in": jnp.concatenate([w_in0[:, :OFF_A], w_in0[:, OFF_SB:], ab_cols], axis=1).astype(BF16),
        "w_gdn_o": w_gdn_o[0].astype(BF16), "w_sb_o": w_sb_o[0].astype(BF16), "w_out": w_out[0].astype(BF16),
        "w_up": w_up[0].astype(BF16), "w_down": w_down[0].astype(BF16),
    }
    bp, lp, d = x_prompt.shape
    bs, ls, _ = x_sample.shape
    conv0 = jnp.zeros((bp, CONV_WIDTH - 1, CONV_CH), F32)
    s_zero = jnp.zeros((bp, GDN_V_HEADS, HEAD_DIM, HEAD_DIM), F32)
    hp, c_p, s_p, k_p, v_p = _layer(x_prompt, conv0, s_zero, None, None, min(PROMPT_CHUNK, lp), wts)
    hs, c_s, s_s, k_s, v_s = _layer(x_sample, state_gdn_conv[0], state_gdn_S[0], cache_sb_k[0], cache_sb_v[0], ls, wts)
    y_prompt = _rmsnorm(hp, final_norm_w, F32).reshape(bp, lp, d)
    y_sample = _rmsnorm(hs, final_norm_w, F32).reshape(bs, ls, d)
    return (y_prompt, y_sample, k_p[None], v_p[None], s_p[None], c_p[None],
            k_s[None], v_s[None], s_s[None].astype(state_gdn_S.dtype), c_s[None])
```

```python
import functools
import math

import jax
import jax.numpy as jnp
from jax import lax
from jax.experimental import pallas as pl
from jax.experimental.pallas import tpu as pltpu

F32 = jnp.float32
BF16 = jnp.bfloat16

EPS = 1e-6
HEAD_DIM = 128
GDN_QK_HEADS = 8
GDN_V_HEADS = 16
GDN_QK_WIDTH = GDN_QK_HEADS * HEAD_DIM
GDN_V_WIDTH = GDN_V_HEADS * HEAD_DIM
CONV_WIDTH = 4
CONV_CH = 2 * GDN_QK_WIDTH + GDN_V_WIDTH
SB_HEADS = 16
SB_WIDTH = SB_HEADS * HEAD_DIM
SB_BLOCK = 128
SB_UNDERFLOW = 106.0
SB_WINDOW = 3
SB_HEAD_BATCH = 4
PROMPT_CHUNK = 64

V7X_MXU_DIM = 256
V7X_VMEM_LIMIT_BYTES = 56 * 1024 * 1024
SUBLANES = 8

OFF_Z = CONV_CH
OFF_A = OFF_Z + GDN_V_WIDTH
OFF_B = OFF_A + GDN_V_HEADS
OFF_SB = OFF_B + GDN_V_HEADS
OFF_GATE = OFF_SB + 3 * SB_WIDTH


def _cparams(*sem):
    return pltpu.CompilerParams(dimension_semantics=sem, vmem_limit_bytes=V7X_VMEM_LIMIT_BYTES)


def _dot(a, b):
    return jnp.dot(a, b, preferred_element_type=F32)


def _dot_nt(a, b):
    return lax.dot_general(a, b, (((1,), (1,)), ((), ())), preferred_element_type=F32)


def _dot_tn(a, b):
    return lax.dot_general(a, b, (((0,), (0,)), ((), ())), preferred_element_type=F32)


def _sigmoid(x):
    return 1.0 / (1.0 + jnp.exp(-x))


def _softplus(x):
    return jnp.maximum(x, 0.0) + jnp.log(1.0 + jnp.exp(-jnp.abs(x)))


def _split3(x):
    x1 = x.astype(BF16)
    r1 = x - x1.astype(F32)
    x2 = r1.astype(BF16)
    x3 = (r1 - x2.astype(F32)).astype(BF16)
    return x1, x2, x3


def _rmsnorm_kernel(x_ref, w_ref, o_ref):
    x = x_ref[...]
    var = jnp.mean(x * x, axis=-1, keepdims=True)
    o_ref[...] = (x * lax.rsqrt(var + EPS) * w_ref[...]).astype(o_ref.dtype)


def _rmsnorm(x, w, out_dtype):
    m, d = x.shape
    tm = min(m, 256)
    return pl.pallas_call(
        _rmsnorm_kernel,
        grid=(m // tm,),
        in_specs=[pl.BlockSpec((tm, d), lambda i: (i, 0)), pl.BlockSpec((1, d), lambda i: (0, 0))],
        out_specs=pl.BlockSpec((tm, d), lambda i: (i, 0)),
        out_shape=jax.ShapeDtypeStruct((m, d), out_dtype),
        compiler_params=_cparams("parallel"),
        name="rmsnorm",
    )(x, w.reshape(1, d).astype(F32))


def _mm_tiles(m, n, tn_max=1024):
    tm = min(m, 1024)
    tn = min(n, tn_max)
    assert m % tm == 0 and n % tn == 0, (m, n)
    return tm, tn


def _mm_kernel(a_ref, w_ref, *o_refs):
    acc = _dot(a_ref[...], w_ref[...])
    for o_ref in o_refs:
        o_ref[...] = acc.astype(o_ref.dtype)


def _matmul(a, w, out_dtypes, col0=0, n=None):
    m, k = a.shape
    n = w.shape[1] if n is None else n
    tm, tn = _mm_tiles(m, n)
    assert col0 % tn == 0
    jb = col0 // tn
    outs = pl.pallas_call(
        _mm_kernel,
        grid=(m // tm, n // tn),
        in_specs=[pl.BlockSpec((tm, k), lambda i, j: (i, 0)), pl.BlockSpec((k, tn), lambda i, j: (0, j + jb))],
        out_specs=[pl.BlockSpec((tm, tn), lambda i, j: (i, j)) for _ in out_dtypes],
        out_shape=[jax.ShapeDtypeStruct((m, n), dt) for dt in out_dtypes],
        compiler_params=_cparams("parallel", "arbitrary"),
        name="matmul",
    )(a, w)
    return outs


def _mm_relu2_kernel(a_ref, w_ref, o_ref):
    acc = _dot(a_ref[...], w_ref[...])
    o_ref[...] = jnp.square(jnp.maximum(acc, 0.0)).astype(o_ref.dtype)


def _matmul_relu2(a, w):
    m, k = a.shape
    n = w.shape[1]
    tm, tn = _mm_tiles(m, n)
    return pl.pallas_call(
        _mm_relu2_kernel,
        grid=(m // tm, n // tn),
        in_specs=[pl.BlockSpec((tm, k), lambda i, j: (i, 0)), pl.BlockSpec((k, tn), lambda i, j: (0, j))],
        out_specs=pl.BlockSpec((tm, tn), lambda i, j: (i, j)),
        out_shape=jax.ShapeDtypeStruct((m, n), BF16),
        compiler_params=_cparams("parallel", "arbitrary"),
        name="matmul_relu2",
    )(a, w)


def _mm_res_kernel(a_ref, w_ref, r_ref, o_ref):
    o_ref[...] = r_ref[...] + _dot(a_ref[...], w_ref[...])


def _matmul_residual(a, w, res):
    m, k = a.shape
    n = w.shape[1]
    tm, tn = _mm_tiles(m, n)
    return pl.pallas_call(
        _mm_res_kernel,
        grid=(m // tm, n // tn),
        in_specs=[pl.BlockSpec((tm, k), lambda i, j: (i, 0)), pl.BlockSpec((k, tn), lambda i, j: (0, j)),
                  pl.BlockSpec((tm, tn), lambda i, j: (i, j))],
        out_specs=pl.BlockSpec((tm, tn), lambda i, j: (i, j)),
        out_shape=jax.ShapeDtypeStruct((m, n), F32),
        compiler_params=_cparams("parallel", "arbitrary"),
        name="matmul_residual",
    )(a, w, res)


def _mm_kacc_kernel(a_ref, w_ref, r_ref, o_ref):
    @pl.when(pl.program_id(2) == 0)
    def _():
        o_ref[...] = r_ref[...]

    o_ref[...] += _dot(a_ref[...], w_ref[...])


def _matmul_residual_ktiled(a, w, res):
    m, k = a.shape
    n = w.shape[1]
    tm = min(m, 1024)
    tn = min(n, 1024)
    tk = min(k, 2048)
    return pl.pallas_call(
        _mm_kacc_kernel,
        grid=(m // tm, n // tn, k // tk),
        in_specs=[pl.BlockSpec((tm, tk), lambda i, j, l: (i, l)), pl.BlockSpec((tk, tn), lambda i, j, l: (l, j)),
                  pl.BlockSpec((tm, tn), lambda i, j, l: (i, j))],
        out_specs=pl.BlockSpec((tm, tn), lambda i, j, l: (i, j)),
        out_shape=jax.ShapeDtypeStruct((m, n), F32),
        compiler_params=_cparams("parallel", "parallel", "arbitrary"),
        name="matmul_residual_ktiled",
    )(a, w, res)


def _merge_kernel(oa_ref, ob_ref, wa_ref, wb_ref, ga_ref, gb_ref, o_ref):
    ya = _dot(oa_ref[...], wa_ref[...])
    yb = _dot(ob_ref[...], wb_ref[...])
    o_ref[...] = (_sigmoid(ga_ref[...]) * ya + _sigmoid(gb_ref[...]) * yb).astype(o_ref.dtype)


def _merge(oa, ob, wa, wb, gates):
    m, k = oa.shape
    n = wa.shape[1]
    tm, tn = _mm_tiles(m, n, 512)
    nb = n // tn
    return pl.pallas_call(
        _merge_kernel,
        grid=(m // tm, nb),
        in_specs=[pl.BlockSpec((tm, k), lambda i, j: (i, 0)), pl.BlockSpec((tm, k), lambda i, j: (i, 0)),
                  pl.BlockSpec((k, tn), lambda i, j: (0, j)), pl.BlockSpec((k, tn), lambda i, j: (0, j)),
                  pl.BlockSpec((tm, tn), lambda i, j: (i, j)), pl.BlockSpec((tm, tn), lambda i, j: (i, j + nb))],
        out_specs=pl.BlockSpec((tm, tn), lambda i, j: (i, j)),
        out_shape=jax.ShapeDtypeStruct((m, n), BF16),
        compiler_params=_cparams("parallel", "arbitrary"),
        name="merge",
    )(oa, ob, wa, wb, gates, gates)


def _gdn_kernel(qkv_ref, z_ref, abc_ref, ar_ref, past_ref, s0_ref, cw_ref, alc_ref, dtc_ref, alr_ref, dtr_ref,
                gnw_ref, o_ref, s_ref, xp_ref, act_ref, ltri_ref, ubd_ref, cmask_ref, smask_ref, *, chunk, group):
    c_len, g_sz = chunk, group
    r_len = c_len * g_sz
    n_groups = GDN_V_HEADS // g_sz
    rep = GDN_V_HEADS // GDN_QK_HEADS
    c_idx = pl.program_id(1)

    @pl.when(c_idx == 0)
    def _init():
        s_ref[...] = s0_ref[...]
        xp_ref[0:SUBLANES, :] = past_ref[0]
        ri = lax.broadcasted_iota(jnp.int32, (r_len, r_len), 0)
        ci = lax.broadcasted_iota(jnp.int32, (r_len, r_len), 1)
        shift = int(math.log2(c_len))
        same = (ri >> shift) == (ci >> shift)
        cmask_ref[...] = jnp.where(same & (ri >= ci), 1.0, 0.0)
        smask_ref[...] = jnp.where(same & (ri > ci), 1.0, 0.0)
        ubd_ref[...] = jnp.where(same & (ri <= ci), 1.0, 0.0).astype(BF16)
        rc = lax.broadcasted_iota(jnp.int32, (c_len, c_len), 0)
        cc = lax.broadcasted_iota(jnp.int32, (c_len, c_len), 1)
        ltri_ref[...] = jnp.where(rc >= cc, 1.0, 0.0).astype(BF16)

    @pl.when(c_idx > 0)
    def _carry():
        xp_ref[0:SUBLANES, :] = xp_ref[c_len:c_len + SUBLANES, :]

    xp_ref[SUBLANES:SUBLANES + c_len, :] = qkv_ref[0]

    for s in range(CONV_CH // HEAD_DIM):
        cols = slice(s * HEAD_DIM, (s + 1) * HEAD_DIM)
        y = cw_ref[0:1, cols] * xp_ref[SUBLANES - 3:SUBLANES - 3 + c_len, cols]
        for i in range(1, CONV_WIDTH):
            y = y + cw_ref[i:i + 1, cols] * xp_ref[SUBLANES - 3 + i:SUBLANES - 3 + i + c_len, cols]
        y = y * _sigmoid(y)
        if s < 2 * GDN_QK_HEADS:
            y = y * lax.rsqrt(jnp.sum(y * y, axis=-1, keepdims=True) + EPS)
            if s < GDN_QK_HEADS:
                y = y * (HEAD_DIM ** -0.5)
        act_ref[:, cols] = y

    ab = abc_ref[0]
    g_col = -jnp.exp(alc_ref[...]) * _softplus(ab + dtc_ref[...])
    beta_col = _sigmoid(ab)
    ltri = ltri_ref[...]
    gc_col = sum(_dot(ltri, t) for t in _split3(g_col))
    g_row = -jnp.exp(alr_ref[...]) * _softplus(ar_ref[0, 0] + dtr_ref[...])
    ubd = ubd_ref[...]
    gc_row = sum(_dot(t, ubd) for t in _split3(g_row))

    cmask = cmask_ref[...]
    smask = smask_ref[...]
    gnw = gnw_ref[...]

    groups = range(n_groups)
    heads_of = [[grp * g_sz + hb for hb in range(g_sz)] for grp in groups]

    def stacked(off, div):
        return [jnp.concatenate([act_ref[:, off + (h // div) * HEAD_DIM:off + (h // div + 1) * HEAD_DIM]
                                 for h in heads_of[grp]], axis=0) for grp in groups]

    kst = stacked(GDN_QK_WIDTH, rep)
    qst = stacked(0, rep)
    vst = stacked(2 * GDN_QK_WIDTH, 1)
    gcb = [jnp.concatenate([jnp.broadcast_to(gc_col[:, h:h + 1], (c_len, HEAD_DIM)) for h in heads_of[grp]], axis=0)
           for grp in groups]
    bcb = [jnp.concatenate([jnp.broadcast_to(beta_col[:, GDN_V_HEADS + h:GDN_V_HEADS + h + 1], (c_len, HEAD_DIM))
                            for h in heads_of[grp]], axis=0) for grp in groups]
    kst_b = [t.astype(BF16) for t in kst]
    gram = [_dot_nt(kst_b[grp], kst_b[grp]) for grp in groups]
    qk = [_dot_nt(qst[grp].astype(BF16), kst_b[grp]) for grp in groups]
    n_rep = r_len // HEAD_DIM
    aqk, tp, xpow = [], [], []
    for grp in groups:
        gcb2 = jnp.concatenate([gcb[grp]] * n_rep, axis=1)
        nbeta = -bcb[grp]
        nbeta2 = jnp.concatenate([nbeta] * n_rep, axis=1) * smask
        decay = jnp.exp(jnp.minimum(gcb2 - gc_row[grp:grp + 1, :], 0.0)) * cmask
        aqk.append((qk[grp] * decay).astype(BF16))
        nm = (gram[grp] * decay) * nbeta2
        tp.append(nm)
        xpow.append(nm)
    n_steps = int(math.log2(c_len))
    for step in range(n_steps):
        for grp in groups:
            xb = xpow[grp].astype(BF16)
            tp[grp] = tp[grp] + _dot(xb, tp[grp].astype(BF16))
            if step + 1 < n_steps:
                xpow[grp] = _dot(xb, xb)
    eg = [jnp.exp(gcb[grp]) for grp in groups]
    sol = []
    for grp in groups:
        rhs = jnp.concatenate([vst[grp] * bcb[grp], kst[grp] * (bcb[grp] * eg[grp])], axis=1)
        sol.append(rhs + _dot(tp[grp].astype(BF16), rhs.astype(BF16)))
    qss = {}
    ust = []
    for grp in groups:
        qg = qst[grp] * eg[grp]
        us = []
        for hb, h in enumerate(heads_of[grp]):
            rows = slice(hb * c_len, (hb + 1) * c_len)
            s_old = s_ref[0, h]
            wq = jnp.concatenate([sol[grp][rows, HEAD_DIM:2 * HEAD_DIM], qg[rows]], axis=0).astype(BF16)
            r1 = _dot(wq, s_old.astype(BF16))
            u = sol[grp][rows, 0:HEAD_DIM] - r1[0:c_len]
            qss[h] = r1[c_len:2 * c_len]
            gl = gcb[grp][(hb + 1) * c_len - 1:(hb + 1) * c_len, :]
            kd = kst[grp][rows] * jnp.exp(gl - gcb[grp][rows])
            s_ref[0, h] = jnp.exp(gl) * s_old + _dot_tn(kd.astype(BF16), u.astype(BF16))
            us.append(u)
        ust.append(jnp.concatenate(us, axis=0).astype(BF16))
    o_intra = [_dot(aqk[grp], ust[grp]) for grp in groups]
    for grp in groups:
        for hb, h in enumerate(heads_of[grp]):
            rows = slice(hb * c_len, (hb + 1) * c_len)
            o = qss[h] + o_intra[grp][rows]
            o = o * lax.rsqrt(jnp.mean(o * o, axis=-1, keepdims=True) + EPS) * gnw
            zz = z_ref[0, :, h * HEAD_DIM:(h + 1) * HEAD_DIM]
            o_ref[0, :, h * HEAD_DIM:(h + 1) * HEAD_DIM] = (o * (zz * _sigmoid(zz))).astype(o_ref.dtype)


def _gdn(proj_a, ab, conv_past, s0, conv_w, a_log, dt_bias, gnw, chunk):
    b, l, _ = proj_a.shape
    n_chunks = l // chunk
    group = V7X_MXU_DIM // chunk
    n_groups = GDN_V_HEADS // group
    r_len = chunk * group
    a_row = ab[:, :, :GDN_V_HEADS].reshape(b, n_chunks, chunk, n_groups, group)
    a_row = a_row.transpose(0, 1, 3, 4, 2).reshape(b, n_chunks, n_groups, r_len)
    past8 = jnp.pad(conv_past.astype(F32), ((0, 0), (SUBLANES - (CONV_WIDTH - 1), 0), (0, 0)))
    pad16 = HEAD_DIM - GDN_V_HEADS
    alc = jnp.pad(a_log.astype(F32), (0, pad16)).reshape(1, HEAD_DIM)
    dtc = jnp.pad(dt_bias.astype(F32), (0, pad16)).reshape(1, HEAD_DIM)
    alr = jnp.repeat(a_log.astype(F32), chunk).reshape(n_groups, r_len)
    dtr = jnp.repeat(dt_bias.astype(F32), chunk).reshape(n_groups, r_len)
    const = lambda *shape: pl.BlockSpec(shape, lambda i, c: (0,) * len(shape))
    return pl.pallas_call(
        functools.partial(_gdn_kernel, chunk=chunk, group=group),
        grid=(b, n_chunks),
        in_specs=[
            pl.BlockSpec((1, chunk, CONV_CH), lambda i, c: (i, c, 0)),
            pl.BlockSpec((1, chunk, GDN_V_WIDTH), lambda i, c: (i, c, CONV_CH // GDN_V_WIDTH)),
            pl.BlockSpec((1, chunk, HEAD_DIM), lambda i, c: (i, c, 0)),
            pl.BlockSpec((1, 1, n_groups, r_len), lambda i, c: (i, c, 0, 0)),
            pl.BlockSpec((1, SUBLANES, CONV_CH), lambda i, c: (i, 0, 0)),
            pl.BlockSpec((1, GDN_V_HEADS, HEAD_DIM, HEAD_DIM), lambda i, c: (i, 0, 0, 0)),
            const(CONV_WIDTH, CONV_CH), const(1, HEAD_DIM), const(1, HEAD_DIM),
            const(n_groups, r_len), const(n_groups, r_len), const(1, HEAD_DIM),
        ],
        out_specs=[
            pl.BlockSpec((1, chunk, GDN_V_WIDTH), lambda i, c: (i, c, 0)),
            pl.BlockSpec((1, GDN_V_HEADS, HEAD_DIM, HEAD_DIM), lambda i, c: (i, 0, 0, 0)),
        ],
        out_shape=[jax.ShapeDtypeStruct((b, l, GDN_V_WIDTH), BF16),
                   jax.ShapeDtypeStruct((b, GDN_V_HEADS, HEAD_DIM, HEAD_DIM), F32)],
        scratch_shapes=[
            pltpu.VMEM((chunk + SUBLANES, CONV_CH), F32),
            pltpu.VMEM((chunk, CONV_CH), F32),
            pltpu.VMEM((chunk, chunk), BF16),
            pltpu.VMEM((r_len, r_len), BF16),
            pltpu.VMEM((r_len, r_len), F32),
            pltpu.VMEM((r_len, r_len), F32),
        ],
        compiler_params=_cparams("parallel", "arbitrary"),
        name="gated_delta_rule",
    )(proj_a, proj_a, ab, a_row, past8, s0.astype(F32), conv_w.astype(F32), alc, dtc, alr, dtr,
      gnw.astype(F32).reshape(1, HEAD_DIM))


def _sb_kernel(q_ref, kt_ref, v_ref, o_ref, *, tq, q_pos0):
    i = pl.program_id(2)
    q = q_ref[0]
    qpos0 = q_pos0 + i * tq
    row = lax.broadcasted_iota(jnp.int32, (tq, SB_BLOCK), 0)
    col = lax.broadcasted_iota(jnp.int32, (tq, SB_BLOCK), 1)
    ri = lax.broadcasted_iota(jnp.int32, (SB_BLOCK, SB_BLOCK), 0)
    ci = lax.broadcasted_iota(jnp.int32, (SB_BLOCK, SB_BLOCK), 1)
    ustrict = jnp.where(ri > ci, 1.0, 0.0).astype(BF16)
    n_masked = max(tq // SB_BLOCK, 1)
    j_full = qpos0 // SB_BLOCK

    def tile(j, later, acc, masked):
        kt = kt_ref[0, 0, j]
        vj = v_ref[0, pl.ds(pl.multiple_of(j * SB_BLOCK, SB_BLOCK), SB_BLOCK), :]
        z = _dot(q, kt) * (HEAD_DIM ** -0.5)
        ls = -_softplus(z)
        if masked:
            vis = (j * SB_BLOCK + col) < (qpos0 + row)
            lf = jnp.where(vis, ls, 0.0)
        else:
            lf = ls
        within = _dot(lf.astype(BF16), ustrict)
        p = jnp.exp((z + ls) + within + later)
        if masked:
            p = jnp.where(vis, p, 0.0)
        acc = acc + _dot(p.astype(BF16), vj)
        return later + jnp.sum(lf, axis=-1, keepdims=True), acc

    later = jnp.zeros((tq, 1), F32)
    acc = jnp.zeros((tq, HEAD_DIM), F32)
    for m in range(n_masked):
        later, acc = tile(j_full + (n_masked - 1 - m), later, acc, True)

    def cond(carry):
        return jnp.logical_and(carry[0] >= 0, jnp.max(carry[1]) > -SB_UNDERFLOW)

    def body(carry):
        later, acc = tile(carry[0], carry[1], carry[2], False)
        return carry[0] - 1, later, acc

    _, later, acc = lax.while_loop(cond, body, (j_full - 1, later, acc))
    o_ref[0] = acc.astype(o_ref.dtype)


def _stick_breaking(q, k_bf, v_bf, q_pos0, tq):
    b, lq, _ = q.shape
    lk = k_bf.shape[1]
    nkb = lk // SB_BLOCK
    kt = k_bf.reshape(b, nkb, SB_BLOCK, SB_HEADS, HEAD_DIM).transpose(0, 3, 1, 4, 2)
    return pl.pallas_call(
        functools.partial(_sb_kernel, tq=tq, q_pos0=q_pos0),
        grid=(b, SB_HEADS, lq // tq),
        in_specs=[
            pl.BlockSpec((1, tq, HEAD_DIM), lambda s, h, i: (s, i, h)),
            pl.BlockSpec((1, 1, nkb, HEAD_DIM, SB_BLOCK), lambda s, h, i: (s, h, 0, 0, 0)),
            pl.BlockSpec((1, lk, HEAD_DIM), lambda s, h, i: (s, 0, h)),
        ],
        out_specs=pl.BlockSpec((1, tq, HEAD_DIM), lambda s, h, i: (s, i, h)),
        out_shape=jax.ShapeDtypeStruct((b, lq, SB_WIDTH), BF16),
        compiler_params=_cparams("parallel", "parallel", "arbitrary"),
        name="stick_breaking",
    )(q, kt, v_bf)


def _sb_window_kernel(q_ref, k0_ref, k1_ref, k2_ref, v0_ref, v1_ref, v2_ref, o_ref, need_ref, *, tq, base, older):
    i = pl.program_id(1)
    row = lax.broadcasted_iota(jnp.int32, (tq, SB_WINDOW * SB_BLOCK), 0)
    col = lax.broadcasted_iota(jnp.int32, (tq, SB_WINDOW * SB_BLOCK), 1)
    assert SB_WINDOW == 3 and SB_HEAD_BATCH % 2 == 0
    ri = lax.broadcasted_iota(jnp.int32, (2 * SB_BLOCK, 2 * SB_BLOCK), 0)
    ci = lax.broadcasted_iota(jnp.int32, (2 * SB_BLOCK, 2 * SB_BLOCK), 1)
    same = (ri < SB_BLOCK) == (ci < SB_BLOCK)
    upair = jnp.where(same & (ri > ci), 1.0, 0.0).astype(BF16)
    k_refs = (k0_ref, k1_ref, k2_ref)
    v_refs = (v0_ref, v1_ref, v2_ref)
    n_win = SB_WINDOW * SB_BLOCK
    limit = row
    for w in range(1, SB_WINDOW):
        limit = jnp.where(col < w * SB_BLOCK, limit, jnp.where(base + i - w >= 0, n_win, 0))
    vis = col < limit
    worst = jnp.full((tq, 1), -jnp.inf, F32)
    for h0 in range(0, SB_HEADS, SB_HEAD_BATCH):
        batch = range(h0, h0 + SB_HEAD_BATCH)
        cols = {h: slice(h * HEAD_DIM, (h + 1) * HEAD_DIM) for h in batch}
        z = {h: _dot_nt(q_ref[0, :, cols[h]], jnp.concatenate([r[0, :, cols[h]] for r in k_refs], axis=0))
             * (HEAD_DIM ** -0.5) for h in batch}
        ls = {h: -_softplus(z[h]) for h in batch}
        lf = {h: jnp.where(vis, ls[h], 0.0) for h in batch}
        lf_b = {h: lf[h].astype(BF16) for h in batch}
        within = {h: [] for h in batch}
        for h in batch:
            both = _dot(lf_b[h][:, 0:2 * SB_BLOCK], upair)
            within[h] += [both[:, 0:SB_BLOCK], both[:, SB_BLOCK:2 * SB_BLOCK]]
        for h in range(h0, h0 + SB_HEAD_BATCH, 2):
            both = _dot(jnp.concatenate([lf_b[h][:, 2 * SB_BLOCK:], lf_b[h + 1][:, 2 * SB_BLOCK:]], axis=1), upair)
            within[h].append(both[:, 0:SB_BLOCK])
            within[h + 1].append(both[:, SB_BLOCK:2 * SB_BLOCK])
        p = {}
        for h in batch:
            later = jnp.zeros((tq, 1), F32)
            shifted = []
            for w in range(SB_WINDOW):
                shifted.append(within[h][w] + later)
                later = later + jnp.sum(lf[h][:, w * SB_BLOCK:(w + 1) * SB_BLOCK], axis=-1, keepdims=True)
            worst = jnp.maximum(worst, later)
            p[h] = jnp.where(vis, jnp.exp((z[h] + ls[h]) + jnp.concatenate(shifted, axis=1)), 0.0).astype(BF16)
        for h in batch:
            acc = _dot(p[h], jnp.concatenate([r[0, :, cols[h]] for r in v_refs], axis=0))
            o_ref[0, :, cols[h]] = acc.astype(o_ref.dtype)
    has_older = base + i + older >= SB_WINDOW
    need_ref[...] = jnp.broadcast_to(jnp.where(has_older, jnp.max(worst), -jnp.inf), need_ref.shape)


def _stick_breaking_window(q, k_bf, v_bf, tq, base, older):
    b, lq, _ = q.shape
    nq = lq // tq
    qspec = pl.BlockSpec((1, tq, SB_WIDTH), lambda s, i: (s, i, 0))
    kspecs = [pl.BlockSpec((1, SB_BLOCK, SB_WIDTH), functools.partial(
        lambda s, i, w: (s, jnp.maximum(base + i - w, 0), 0), w=w)) for w in range(SB_WINDOW)]
    return pl.pallas_call(
        functools.partial(_sb_window_kernel, tq=tq, base=base, older=older),
        grid=(b, nq),
        in_specs=[qspec] + kspecs + kspecs,
        out_specs=[qspec, pl.BlockSpec((1, 1, SUBLANES, HEAD_DIM), lambda s, i: (s, i, 0, 0))],
        out_shape=[jax.ShapeDtypeStruct((b, lq, SB_WIDTH), BF16),
                   jax.ShapeDtypeStruct((b, nq, SUBLANES, HEAD_DIM), F32)],
        compiler_params=_cparams("parallel", "arbitrary"),
        name="stick_breaking_window",
    )(q, k_bf, k_bf, k_bf, v_bf, v_bf, v_bf)


def _layer(x, conv_past, s0, past_k, past_v, chunk, wts):
    b, l, d = x.shape
    m = b * l
    x2 = x.reshape(m, d)
    xn = _rmsnorm(x2, wts["norm1_w"], BF16)
    w_in = wts["w_in"]
    c_q = OFF_A
    c_g = c_q + 3 * SB_WIDTH
    c_ab = c_g + 2 * d
    (ab,) = _matmul(xn, w_in, (F32,), c_ab, HEAD_DIM)
    if m <= 256:
        (proj,) = _matmul(xn, w_in, (F32,), 0, c_ab)
        proj_a, q_sb, gates = proj[:, :OFF_A], proj[:, c_q:c_q + SB_WIDTH].astype(BF16), proj[:, c_g:]
        k_sb, v_sb = proj[:, c_q + SB_WIDTH:c_q + 2 * SB_WIDTH], proj[:, c_q + 2 * SB_WIDTH:c_g]
        k_bf, v_bf = k_sb.astype(BF16), v_sb.astype(BF16)
    else:
        (proj_a,) = _matmul(xn, w_in, (F32,), 0, OFF_A)
        (q_sb,) = _matmul(xn, w_in, (BF16,), c_q, SB_WIDTH)
        k_sb, k_bf = _matmul(xn, w_in, (F32, BF16), c_q + SB_WIDTH, SB_WIDTH)
        v_sb, v_bf = _matmul(xn, w_in, (F32, BF16), c_q + 2 * SB_WIDTH, SB_WIDTH)
        (gates,) = _matmul(xn, w_in, (F32,), c_g, 2 * d)

    o_a, s_new = _gdn(proj_a.reshape(b, l, -1), ab.reshape(b, l, -1), conv_past, s0, wts["conv_w"], wts["A_log"],
                      wts["dt_bias"], wts["gdn_norm_w"], chunk)
    conv_state = proj_a.reshape(b, l, -1)[:, l - (CONV_WIDTH - 1):, :CONV_CH]

    k_bf = k_bf.reshape(b, l, SB_WIDTH)
    v_bf = v_bf.reshape(b, l, SB_WIDTH)
    q3 = q_sb.reshape(b, l, SB_WIDTH)
    if past_k is None:
        o_win, need = _stick_breaking_window(q3, k_bf, v_bf, min(l, SB_BLOCK), 0, 0)

        def full_sweep():
            return _stick_breaking(q3, k_bf, v_bf, 0, min(l, 256))
    else:
        p = past_k.shape[1]
        n_cached = (SB_WINDOW - 1) * SB_BLOCK
        assert p % SB_BLOCK == 0 and p >= n_cached and l <= SB_BLOCK

        def with_cache(new, past, keep):
            return jnp.concatenate([past[:, p - keep:].reshape(b, keep, SB_WIDTH).astype(BF16), new,
                                    jnp.zeros((b, SB_BLOCK - l, SB_WIDTH), BF16)], axis=1)

        o_win, need = _stick_breaking_window(q3, with_cache(k_bf, past_k, n_cached), with_cache(v_bf, past_v, n_cached),
                                             l, SB_WINDOW - 1, (p - n_cached) // SB_BLOCK)

        def full_sweep():
            return _stick_breaking(q3, with_cache(k_bf, past_k, p), with_cache(v_bf, past_v, p), p, l)
    o_b = lax.cond(jnp.max(need) > -SB_UNDERFLOW, full_sweep, lambda: o_win)

    merged = _merge(o_a.reshape(m, GDN_V_WIDTH), o_b.reshape(m, SB_WIDTH), wts["w_gdn_o"], wts["w_sb_o"], gates)
    h = _matmul_residual(merged, wts["w_out"], x2)
    hn = _rmsnorm(h, wts["norm2_w"], BF16)
    hid = _matmul_relu2(hn, wts["w_up"])
    h = _matmul_residual_ktiled(hid, wts["w_down"], h)
    return (h, conv_state, s_new, k_sb.reshape(b, l, SB_HEADS, HEAD_DIM), v_sb.reshape(b, l, SB_HEADS, HEAD_DIM))


def kernel(x_prompt, x_sample, cache_sb_k, cache_sb_v, state_gdn_S, state_gdn_conv, norm1_w, w_in, conv_w, A_log,
           dt_bias, gdn_norm_w, w_gdn_o, w_sb_o, w_out, norm2_w, w_up, w_down, final_norm_w):
    depth = w_in.shape[0]
    assert depth == 1
    w_in0 = w_in[0]
    ab_cols = jnp.pad(w_in0[:, OFF_A:OFF_SB], ((0, 0), (0, HEAD_DIM - 2 * GDN_V_HEADS)))
    wts = {
        "norm1_w": norm1_w[0], "norm2_w": norm2_w[0], "conv_w": conv_w[0], "A_log": A_log[0], "dt_bias": dt_bias[0],
        "gdn_norm_w": gdn_norm_w[0],
        "w_in": jnp.concatenate([w_in0[:, :OFF_A], w_in0[:, OFF_SB:], ab_cols], axis=1).astype(BF16),
        "w_gdn_o": w_gdn_o[0].astype(BF16), "w_sb_o": w_sb_o[0].astype(BF16), "w_out": w_out[0].astype(BF16),
        "w_up": w_up[0].astype(BF16), "w_down": w_down[0].astype(BF16),
    }
    bp, lp, d = x_prompt.shape
    bs, ls, _ = x_sample.shape
    conv0 = jnp.zeros((bp, CONV_WIDTH - 1, CONV_CH), F32)
    s_zero = jnp.zeros((bp, GDN_V_HEADS, HEAD_DIM, HEAD_DIM), F32)
    hp, c_p, s_p, k_p, v_p = _layer(x_prompt, conv0, s_zero, None, None, min(PROMPT_CHUNK, lp), wts)
    hs, c_s, s_s, k_s, v_s = _layer(x_sample, state_gdn_conv[0], state_gdn_S[0], cache_sb_k[0], cache_sb_v[0], ls, wts)
    y_prompt = _rmsnorm(hp, final_norm_w, F32).reshape(bp, lp, d)
    y_sample = _rmsnorm(hs, final_norm_w, F32).reshape(bs, ls, d)
    return (y_prompt, y_sample, k_p[None], v_p[None], s_p[None], c_p[None],
            k_s[None], v_s[None], s_s[None].astype(state_gdn_S.dtype), c_s[None])
```

```python
import functools
import math

import jax
import jax.numpy as jnp
from jax import lax
from jax.experimental import pallas as pl
from jax.experimental.pallas import tpu as pltpu

F32 = jnp.float32
BF16 = jnp.bfloat16

EPS = 1e-6
HEAD_DIM = 128
GDN_QK_HEADS = 8
GDN_V_HEADS = 16
GDN_QK_WIDTH = GDN_QK_HEADS * HEAD_DIM
GDN_V_WIDTH = GDN_V_HEADS * HEAD_DIM
CONV_WIDTH = 4
CONV_CH = 2 * GDN_QK_WIDTH + GDN_V_WIDTH
SB_HEADS = 16
SB_WIDTH = SB_HEADS * HEAD_DIM
SB_BLOCK = 128
SB_UNDERFLOW = 106.0
SB_WINDOW = 3
SB_HEAD_BATCH = 4
PROMPT_CHUNK = 64

V7X_MXU_DIM = 256
V7X_VMEM_LIMIT_BYTES = 56 * 1024 * 1024
SUBLANES = 8

OFF_Z = CONV_CH
OFF_A = OFF_Z + GDN_V_WIDTH
OFF_B = OFF_A + GDN_V_HEADS
OFF_SB = OFF_B + GDN_V_HEADS
OFF_GATE = OFF_SB + 3 * SB_WIDTH


def _cparams(*sem):
    return pltpu.CompilerParams(dimension_semantics=sem, vmem_limit_bytes=V7X_VMEM_LIMIT_BYTES)


def _dot(a, b):
    return jnp.dot(a, b, preferred_element_type=F32)


def _dot_nt(a, b):
    return lax.dot_general(a, b, (((1,), (1,)), ((), ())), preferred_element_type=F32)


def _dot_tn(a, b):
    return lax.dot_general(a, b, (((0,), (0,)), ((), ())), preferred_element_type=F32)


def _sigmoid(x):
    return 1.0 / (1.0 + jnp.exp(-x))


def _softplus(x):
    return jnp.maximum(x, 0.0) + jnp.log(1.0 + jnp.exp(-jnp.abs(x)))


def _split3(x):
    x1 = x.astype(BF16)
    r1 = x - x1.astype(F32)
    x2 = r1.astype(BF16)
    x3 = (r1 - x2.astype(F32)).astype(BF16)
    return x1, x2, x3


def _rmsnorm_kernel(x_ref, w_ref, o_ref):
    x = x_ref[...]
    var = jnp.mean(x * x, axis=-1, keepdims=True)
    o_ref[...] = (x * lax.rsqrt(var + EPS) * w_ref[...]).astype(o_ref.dtype)


def _rmsnorm(x, w, out_dtype):
    m, d = x.shape
    tm = min(m, 256)
    return pl.pallas_call(
        _rmsnorm_kernel,
        grid=(m // tm,),
        in_specs=[pl.BlockSpec((tm, d), lambda i: (i, 0)), pl.BlockSpec((1, d), lambda i: (0, 0))],
        out_specs=pl.BlockSpec((tm, d), lambda i: (i, 0)),
        out_shape=jax.ShapeDtypeStruct((m, d), out_dtype),
        compiler_params=_cparams("parallel"),
        name="rmsnorm",
    )(x, w.reshape(1, d).astype(F32))


def _repack_kernel(a_ref, b_ref, o_ref, *, first_shifted, shift):
    j = pl.program_id(1)

    @pl.when(j < first_shifted)
    def _():
        o_ref[...] = a_ref[...].astype(o_ref.dtype)

    @pl.when(j >= first_shifted)
    def _():
        both = jnp.concatenate([a_ref[...], b_ref[...]], axis=1)
        o_ref[...] = both[:, shift:shift + o_ref.shape[1]].astype(o_ref.dtype)


def _repack_w_in(w):
    k, n_src = w.shape
    tn, tk = 1024, 512
    shift = OFF_SB - OFF_A
    n_out = n_src - shift
    assert OFF_A % tn == 0 and n_out % tn == 0 and k % tk == 0 and shift < HEAD_DIM
    return pl.pallas_call(
        functools.partial(_repack_kernel, first_shifted=OFF_A // tn, shift=shift),
        grid=(k // tk, n_out // tn),
        in_specs=[pl.BlockSpec((tk, tn), lambda i, j: (i, j)),
                  pl.BlockSpec((tk, HEAD_DIM), lambda i, j: (i, (j + 1) * (tn // HEAD_DIM)))],
        out_specs=pl.BlockSpec((tk, tn), lambda i, j: (i, j)),
        out_shape=jax.ShapeDtypeStruct((k, n_out), BF16),
        compiler_params=_cparams("parallel", "arbitrary"),
        name="repack_w_in",
    )(w, w)


def _mm_tiles(m, n, tn_max=1024):
    tm = min(m, 1024)
    tn = min(n, tn_max)
    assert m % tm == 0 and n % tn == 0, (m, n)
    return tm, tn


def _mm_kernel(a_ref, w_ref, *o_refs):
    acc = _dot(a_ref[...], w_ref[...])
    for o_ref in o_refs:
        o_ref[...] = acc.astype(o_ref.dtype)


def _matmul(a, w, out_dtypes, col0=0, n=None):
    m, k = a.shape
    n = w.shape[1] if n is None else n
    tm, tn = _mm_tiles(m, n)
    assert col0 % tn == 0
    jb = col0 // tn
    outs = pl.pallas_call(
        _mm_kernel,
        grid=(m // tm, n // tn),
        in_specs=[pl.BlockSpec((tm, k), lambda i, j: (i, 0)), pl.BlockSpec((k, tn), lambda i, j: (0, j + jb))],
        out_specs=[pl.BlockSpec((tm, tn), lambda i, j: (i, j)) for _ in out_dtypes],
        out_shape=[jax.ShapeDtypeStruct((m, n), dt) for dt in out_dtypes],
        compiler_params=_cparams("parallel", "arbitrary"),
        name="matmul",
    )(a, w)
    return outs


def _mm_relu2_kernel(a_ref, w_ref, o_ref):
    acc = _dot(a_ref[...], w_ref[...])
    o_ref[...] = jnp.square(jnp.maximum(acc, 0.0)).astype(o_ref.dtype)


def _matmul_relu2(a, w):
    m, k = a.shape
    n = w.shape[1]
    tm, tn = _mm_tiles(m, n)
    return pl.pallas_call(
        _mm_relu2_kernel,
        grid=(m // tm, n // tn),
        in_specs=[pl.BlockSpec((tm, k), lambda i, j: (i, 0)), pl.BlockSpec((k, tn), lambda i, j: (0, j))],
        out_specs=pl.BlockSpec((tm, tn), lambda i, j: (i, j)),
        out_shape=jax.ShapeDtypeStruct((m, n), BF16),
        compiler_params=_cparams("parallel", "arbitrary"),
        name="matmul_relu2",
    )(a, w)


def _mm_res_kernel(a_ref, w_ref, r_ref, o_ref):
    o_ref[...] = r_ref[...] + _dot(a_ref[...], w_ref[...])


def _matmul_residual(a, w, res):
    m, k = a.shape
    n = w.shape[1]
    tm, tn = _mm_tiles(m, n)
    return pl.pallas_call(
        _mm_res_kernel,
        grid=(m // tm, n // tn),
        in_specs=[pl.BlockSpec((tm, k), lambda i, j: (i, 0)), pl.BlockSpec((k, tn), lambda i, j: (0, j)),
                  pl.BlockSpec((tm, tn), lambda i, j: (i, j))],
        out_specs=pl.BlockSpec((tm, tn), lambda i, j: (i, j)),
        out_shape=jax.ShapeDtypeStruct((m, n), F32),
        compiler_params=_cparams("parallel", "arbitrary"),
        name="matmul_residual",
    )(a, w, res)


def _mm_kacc_kernel(a_ref, w_ref, r_ref, o_ref):
    @pl.when(pl.program_id(2) == 0)
    def _():
        o_ref[...] = r_ref[...]

    o_ref[...] += _dot(a_ref[...], w_ref[...])


def _matmul_residual_ktiled(a, w, res):
    m, k = a.shape
    n = w.shape[1]
    tm = min(m, 1024)
    tn = min(n, 1024)
    tk = min(k, 2048)
    return pl.pallas_call(
        _mm_kacc_kernel,
        grid=(m // tm, n // tn, k // tk),
        in_specs=[pl.BlockSpec((tm, tk), lambda i, j, l: (i, l)), pl.BlockSpec((tk, tn), lambda i, j, l: (l, j)),
                  pl.BlockSpec((tm, tn), lambda i, j, l: (i, j))],
        out_specs=pl.BlockSpec((tm, tn), lambda i, j, l: (i, j)),
        out_shape=jax.ShapeDtypeStruct((m, n), F32),
        compiler_params=_cparams("parallel", "parallel", "arbitrary"),
        name="matmul_residual_ktiled",
    )(a, w, res)


def _merge_kernel(oa_ref, ob_ref, wa_ref, wb_ref, ga_ref, gb_ref, o_ref):
    ya = _dot(oa_ref[...], wa_ref[...])
    yb = _dot(ob_ref[...], wb_ref[...])
    o_ref[...] = (_sigmoid(ga_ref[...]) * ya + _sigmoid(gb_ref[...]) * yb).astype(o_ref.dtype)


def _merge(oa, ob, wa, wb, gates):
    m, k = oa.shape
    n = wa.shape[1]
    tm, tn = _mm_tiles(m, n, 512)
    nb = n // tn
    return pl.pallas_call(
        _merge_kernel,
        grid=(m // tm, nb),
        in_specs=[pl.BlockSpec((tm, k), lambda i, j: (i, 0)), pl.BlockSpec((tm, k), lambda i, j: (i, 0)),
                  pl.BlockSpec((k, tn), lambda i, j: (0, j)), pl.BlockSpec((k, tn), lambda i, j: (0, j)),
                  pl.BlockSpec((tm, tn), lambda i, j: (i, j)), pl.BlockSpec((tm, tn), lambda i, j: (i, j + nb))],
        out_specs=pl.BlockSpec((tm, tn), lambda i, j: (i, j)),
        out_shape=jax.ShapeDtypeStruct((m, n), BF16),
        compiler_params=_cparams("parallel", "arbitrary"),
        name="merge",
    )(oa, ob, wa, wb, gates, gates)


def _gdn_kernel(qkv_ref, z_ref, abc_ref, ar_ref, past_ref, s0_ref, cw_ref, alc_ref, dtc_ref, alr_ref, dtr_ref,
                gnw_ref, o_ref, s_ref, xp_ref, act_ref, ltri_ref, ubd_ref, cmask_ref, smask_ref, *, chunk, group):
    c_len, g_sz = chunk, group
    r_len = c_len * g_sz
    n_groups = GDN_V_HEADS // g_sz
    rep = GDN_V_HEADS // GDN_QK_HEADS
    c_idx = pl.program_id(1)

    @pl.when(c_idx == 0)
    def _init():
        s_ref[...] = s0_ref[...]
        xp_ref[0:SUBLANES, :] = past_ref[0]
        ri = lax.broadcasted_iota(jnp.int32, (r_len, r_len), 0)
        ci = lax.broadcasted_iota(jnp.int32, (r_len, r_len), 1)
        shift = int(math.log2(c_len))
        same = (ri >> shift) == (ci >> shift)
        cmask_ref[...] = jnp.where(same & (ri >= ci), 1.0, 0.0)
        smask_ref[...] = jnp.where(same & (ri > ci), 1.0, 0.0)
        ubd_ref[...] = jnp.where(same & (ri <= ci), 1.0, 0.0).astype(BF16)
        rc = lax.broadcasted_iota(jnp.int32, (c_len, c_len), 0)
        cc = lax.broadcasted_iota(jnp.int32, (c_len, c_len), 1)
        ltri_ref[...] = jnp.where(rc >= cc, 1.0, 0.0).astype(BF16)

    @pl.when(c_idx > 0)
    def _carry():
        xp_ref[0:SUBLANES, :] = xp_ref[c_len:c_len + SUBLANES, :]

    xp_ref[SUBLANES:SUBLANES + c_len, :] = qkv_ref[0]

    for s in range(CONV_CH // HEAD_DIM):
        cols = slice(s * HEAD_DIM, (s + 1) * HEAD_DIM)
        y = cw_ref[0:1, cols] * xp_ref[SUBLANES - 3:SUBLANES - 3 + c_len, cols]
        for i in range(1, CONV_WIDTH):
            y = y + cw_ref[i:i + 1, cols] * xp_ref[SUBLANES - 3 + i:SUBLANES - 3 + i + c_len, cols]
        y = y * _sigmoid(y)
        if s < 2 * GDN_QK_HEADS:
            y = y * lax.rsqrt(jnp.sum(y * y, axis=-1, keepdims=True) + EPS)
            if s < GDN_QK_HEADS:
                y = y * (HEAD_DIM ** -0.5)
        act_ref[:, cols] = y

    ab = abc_ref[0]
    g_col = -jnp.exp(alc_ref[...]) * _softplus(ab + dtc_ref[...])
    beta_col = _sigmoid(ab)
    ltri = ltri_ref[...]
    gc_col = sum(_dot(ltri, t) for t in _split3(g_col))
    g_row = -jnp.exp(alr_ref[...]) * _softplus(ar_ref[0, 0] + dtr_ref[...])
    ubd = ubd_ref[...]
    gc_row = sum(_dot(t, ubd) for t in _split3(g_row))

    cmask = cmask_ref[...]
    smask = smask_ref[...]
    gnw = gnw_ref[...]

    groups = range(n_groups)
    heads_of = [[grp * g_sz + hb for hb in range(g_sz)] for grp in groups]

    def stacked(off, div):
        return [jnp.concatenate([act_ref[:, off + (h // div) * HEAD_DIM:off + (h // div + 1) * HEAD_DIM]
                                 for h in heads_of[grp]], axis=0) for grp in groups]

    kst = stacked(GDN_QK_WIDTH, rep)
    qst = stacked(0, rep)
    vst = stacked(2 * GDN_QK_WIDTH, 1)
    gcb = [jnp.concatenate([jnp.broadcast_to(gc_col[:, h:h + 1], (c_len, HEAD_DIM)) for h in heads_of[grp]], axis=0)
           for grp in groups]
    bcb = [jnp.concatenate([jnp.broadcast_to(beta_col[:, GDN_V_HEADS + h:GDN_V_HEADS + h + 1], (c_len, HEAD_DIM))
                            for h in heads_of[grp]], axis=0) for grp in groups]
    kst_b = [t.astype(BF16) for t in kst]
    gram = [_dot_nt(kst_b[grp], kst_b[grp]) for grp in groups]
    qk = [_dot_nt(qst[grp].astype(BF16), kst_b[grp]) for grp in groups]
    n_rep = r_len // HEAD_DIM
    aqk, tp, xpow = [], [], []
    for grp in groups:
        gcb2 = jnp.concatenate([gcb[grp]] * n_rep, axis=1)
        nbeta = -bcb[grp]
        nbeta2 = jnp.concatenate([nbeta] * n_rep, axis=1) * smask
        decay = jnp.exp(jnp.minimum(gcb2 - gc_row[grp:grp + 1, :], 0.0)) * cmask
        aqk.append((qk[grp] * decay).astype(BF16))
        nm = (gram[grp] * decay) * nbeta2
        tp.append(nm)
        xpow.append(nm)
    n_steps = int(math.log2(c_len))
    for step in range(n_steps):
        for grp in groups:
            xb = xpow[grp].astype(BF16)
            tp[grp] = tp[grp] + _dot(xb, tp[grp].astype(BF16))
            if step + 1 < n_steps:
                xpow[grp] = _dot(xb, xb)
    eg = [jnp.exp(gcb[grp]) for grp in groups]
    sol = []
    for grp in groups:
        rhs = jnp.concatenate([vst[grp] * bcb[grp], kst[grp] * (bcb[grp] * eg[grp])], axis=1)
        sol.append(rhs + _dot(tp[grp].astype(BF16), rhs.astype(BF16)))
    qss = {}
    ust = []
    for grp in groups:
        qg = qst[grp] * eg[grp]
        us = []
        for hb, h in enumerate(heads_of[grp]):
            rows = slice(hb * c_len, (hb + 1) * c_len)
            s_old = s_ref[0, h]
            wq = jnp.concatenate([sol[grp][rows, HEAD_DIM:2 * HEAD_DIM], qg[rows]], axis=0).astype(BF16)
            r1 = _dot(wq, s_old.astype(BF16))
            u = sol[grp][rows, 0:HEAD_DIM] - r1[0:c_len]
            qss[h] = r1[c_len:2 * c_len]
            gl = gcb[grp][(hb + 1) * c_len - 1:(hb + 1) * c_len, :]
            kd = kst[grp][rows] * jnp.exp(gl - gcb[grp][rows])
            s_ref[0, h] = jnp.exp(gl) * s_old + _dot_tn(kd.astype(BF16), u.astype(BF16))
            us.append(u)
        ust.append(jnp.concatenate(us, axis=0).astype(BF16))
    o_intra = [_dot(aqk[grp], ust[grp]) for grp in groups]
    for grp in groups:
        for hb, h in enumerate(heads_of[grp]):
            rows = slice(hb * c_len, (hb + 1) * c_len)
            o = qss[h] + o_intra[grp][rows]
            o = o * lax.rsqrt(jnp.mean(o * o, axis=-1, keepdims=True) + EPS) * gnw
            zz = z_ref[0, :, h * HEAD_DIM:(h + 1) * HEAD_DIM]
            o_ref[0, :, h * HEAD_DIM:(h + 1) * HEAD_DIM] = (o * (zz * _sigmoid(zz))).astype(o_ref.dtype)


def _gdn(proj_a, ab, conv_past, s0, conv_w, a_log, dt_bias, gnw, chunk):
    b, l, _ = proj_a.shape
    n_chunks = l // chunk
    group = V7X_MXU_DIM // chunk
    n_groups = GDN_V_HEADS // group
    r_len = chunk * group
    a_row = ab[:, :, :GDN_V_HEADS].reshape(b, n_chunks, chunk, n_groups, group)
    a_row = a_row.transpose(0, 1, 3, 4, 2).reshape(b, n_chunks, n_groups, r_len)
    past8 = jnp.pad(conv_past.astype(F32), ((0, 0), (SUBLANES - (CONV_WIDTH - 1), 0), (0, 0)))
    pad16 = HEAD_DIM - GDN_V_HEADS
    alc = jnp.pad(a_log.astype(F32), (0, pad16)).reshape(1, HEAD_DIM)
    dtc = jnp.pad(dt_bias.astype(F32), (0, pad16)).reshape(1, HEAD_DIM)
    alr = jnp.repeat(a_log.astype(F32), chunk).reshape(n_groups, r_len)
    dtr = jnp.repeat(dt_bias.astype(F32), chunk).reshape(n_groups, r_len)
    const = lambda *shape: pl.BlockSpec(shape, lambda i, c: (0,) * len(shape))
    return pl.pallas_call(
        functools.partial(_gdn_kernel, chunk=chunk, group=group),
        grid=(b, n_chunks),
        in_specs=[
            pl.BlockSpec((1, chunk, CONV_CH), lambda i, c: (i, c, 0)),
            pl.BlockSpec((1, chunk, GDN_V_WIDTH), lambda i, c: (i, c, CONV_CH // GDN_V_WIDTH)),
            pl.BlockSpec((1, chunk, HEAD_DIM), lambda i, c: (i, c, 0)),
            pl.BlockSpec((1, 1, n_groups, r_len), lambda i, c: (i, c, 0, 0)),
            pl.BlockSpec((1, SUBLANES, CONV_CH), lambda i, c: (i, 0, 0)),
            pl.BlockSpec((1, GDN_V_HEADS, HEAD_DIM, HEAD_DIM), lambda i, c: (i, 0, 0, 0)),
            const(CONV_WIDTH, CONV_CH), const(1, HEAD_DIM), const(1, HEAD_DIM),
            const(n_groups, r_len), const(n_groups, r_len), const(1, HEAD_DIM),
        ],
        out_specs=[
            pl.BlockSpec((1, chunk, GDN_V_WIDTH), lambda i, c: (i, c, 0)),
            pl.BlockSpec((1, GDN_V_HEADS, HEAD_DIM, HEAD_DIM), lambda i, c: (i, 0, 0, 0)),
        ],
        out_shape=[jax.ShapeDtypeStruct((b, l, GDN_V_WIDTH), BF16),
                   jax.ShapeDtypeStruct((b, GDN_V_HEADS, HEAD_DIM, HEAD_DIM), F32)],
        scratch_shapes=[
            pltpu.VMEM((chunk + SUBLANES, CONV_CH), F32),
            pltpu.VMEM((chunk, CONV_CH), F32),
            pltpu.VMEM((chunk, chunk), BF16),
            pltpu.VMEM((r_len, r_len), BF16),
            pltpu.VMEM((r_len, r_len), F32),
            pltpu.VMEM((r_len, r_len), F32),
        ],
        compiler_params=_cparams("parallel", "arbitrary"),
        name="gated_delta_rule",
    )(proj_a, proj_a, ab, a_row, past8, s0.astype(F32), conv_w.astype(F32), alc, dtc, alr, dtr,
      gnw.astype(F32).reshape(1, HEAD_DIM))


def _sb_kernel(q_ref, kt_ref, v_ref, o_ref, *, tq, q_pos0):
    i = pl.program_id(2)
    q = q_ref[0]
    qpos0 = q_pos0 + i * tq
    row = lax.broadcasted_iota(jnp.int32, (tq, SB_BLOCK), 0)
    col = lax.broadcasted_iota(jnp.int32, (tq, SB_BLOCK), 1)
    ri = lax.broadcasted_iota(jnp.int32, (SB_BLOCK, SB_BLOCK), 0)
    ci = lax.broadcasted_iota(jnp.int32, (SB_BLOCK, SB_BLOCK), 1)
    ustrict = jnp.where(ri > ci, 1.0, 0.0).astype(BF16)
    n_masked = max(tq // SB_BLOCK, 1)
    j_full = qpos0 // SB_BLOCK

    def tile(j, later, acc, masked):
        kt = kt_ref[0, 0, j]
        vj = v_ref[0, pl.ds(pl.multiple_of(j * SB_BLOCK, SB_BLOCK), SB_BLOCK), :]
        z = _dot(q, kt) * (HEAD_DIM ** -0.5)
        ls = -_softplus(z)
        if masked:
            vis = (j * SB_BLOCK + col) < (qpos0 + row)
            lf = jnp.where(vis, ls, 0.0)
        else:
            lf = ls
        within = _dot(lf.astype(BF16), ustrict)
        p = jnp.exp((z + ls) + within + later)
        if masked:
            p = jnp.where(vis, p, 0.0)
        acc = acc + _dot(p.astype(BF16), vj)
        return later + jnp.sum(lf, axis=-1, keepdims=True), acc

    later = jnp.zeros((tq, 1), F32)
    acc = jnp.zeros((tq, HEAD_DIM), F32)
    for m in range(n_masked):
        later, acc = tile(j_full + (n_masked - 1 - m), later, acc, True)

    def cond(carry):
        return jnp.logical_and(carry[0] >= 0, jnp.max(carry[1]) > -SB_UNDERFLOW)

    def body(carry):
        later, acc = tile(carry[0], carry[1], carry[2], False)
        return carry[0] - 1, later, acc

    _, later, acc = lax.while_loop(cond, body, (j_full - 1, later, acc))
    o_ref[0] = acc.astype(o_ref.dtype)


def _stick_breaking(q, k_bf, v_bf, q_pos0, tq):
    b, lq, _ = q.shape
    lk = k_bf.shape[1]
    nkb = lk // SB_BLOCK
    kt = k_bf.reshape(b, nkb, SB_BLOCK, SB_HEADS, HEAD_DIM).transpose(0, 3, 1, 4, 2)
    return pl.pallas_call(
        functools.partial(_sb_kernel, tq=tq, q_pos0=q_pos0),
        grid=(b, SB_HEADS, lq // tq),
        in_specs=[
            pl.BlockSpec((1, tq, HEAD_DIM), lambda s, h, i: (s, i, h)),
            pl.BlockSpec((1, 1, nkb, HEAD_DIM, SB_BLOCK), lambda s, h, i: (s, h, 0, 0, 0)),
            pl.BlockSpec((1, lk, HEAD_DIM), lambda s, h, i: (s, 0, h)),
        ],
        out_specs=pl.BlockSpec((1, tq, HEAD_DIM), lambda s, h, i: (s, i, h)),
        out_shape=jax.ShapeDtypeStruct((b, lq, SB_WIDTH), BF16),
        compiler_params=_cparams("parallel", "parallel", "arbitrary"),
        name="stick_breaking",
    )(q, kt, v_bf)


def _sb_window_kernel(q_ref, k0_ref, k1_ref, k2_ref, v0_ref, v1_ref, v2_ref, o_ref, need_ref, *, tq, base, older):
    i = pl.program_id(1)
    row = lax.broadcasted_iota(jnp.int32, (tq, SB_WINDOW * SB_BLOCK), 0)
    col = lax.broadcasted_iota(jnp.int32, (tq, SB_WINDOW * SB_BLOCK), 1)
    assert SB_WINDOW == 3 and SB_HEAD_BATCH % 2 == 0
    ri = lax.broadcasted_iota(jnp.int32, (2 * SB_BLOCK, 2 * SB_BLOCK), 0)
    ci = lax.broadcasted_iota(jnp.int32, (2 * SB_BLOCK, 2 * SB_BLOCK), 1)
    same = (ri < SB_BLOCK) == (ci < SB_BLOCK)
    upair = jnp.where(same & (ri > ci), 1.0, 0.0).astype(BF16)
    k_refs = (k0_ref, k1_ref, k2_ref)
    v_refs = (v0_ref, v1_ref, v2_ref)
    n_win = SB_WINDOW * SB_BLOCK
    limit = row
    for w in range(1, SB_WINDOW):
        limit = jnp.where(col < w * SB_BLOCK, limit, jnp.where(base + i - w >= 0, n_win, 0))
    vis = col < limit
    worst = jnp.full((tq, 1), -jnp.inf, F32)
    for h0 in range(0, SB_HEADS, SB_HEAD_BATCH):
        batch = range(h0, h0 + SB_HEAD_BATCH)
        cols = {h: slice(h * HEAD_DIM, (h + 1) * HEAD_DIM) for h in batch}
        z = {h: _dot_nt(q_ref[0, :, cols[h]], jnp.concatenate([r[0, :, cols[h]] for r in k_refs], axis=0))
             * (HEAD_DIM ** -0.5) for h in batch}
        ls = {h: -_softplus(z[h]) for h in batch}
        lf = {h: jnp.where(vis, ls[h], 0.0) for h in batch}
        lf_b = {h: lf[h].astype(BF16) for h in batch}
        within = {h: [] for h in batch}
        for h in batch:
            both = _dot(lf_b[h][:, 0:2 * SB_BLOCK], upair)
            within[h] += [both[:, 0:SB_BLOCK], both[:, SB_BLOCK:2 * SB_BLOCK]]
        for h in range(h0, h0 + SB_HEAD_BATCH, 2):
            both = _dot(jnp.concatenate([lf_b[h][:, 2 * SB_BLOCK:], lf_b[h + 1][:, 2 * SB_BLOCK:]], axis=1), upair)
            within[h].append(both[:, 0:SB_BLOCK])
            within[h + 1].append(both[:, SB_BLOCK:2 * SB_BLOCK])
        p = {}
        for h in batch:
            later = jnp.zeros((tq, 1), F32)
            shifted = []
            for w in range(SB_WINDOW):
                shifted.append(within[h][w] + later)
                later = later + jnp.sum(lf[h][:, w * SB_BLOCK:(w + 1) * SB_BLOCK], axis=-1, keepdims=True)
            worst = jnp.maximum(worst, later)
            p[h] = jnp.where(vis, jnp.exp((z[h] + ls[h]) + jnp.concatenate(shifted, axis=1)), 0.0).astype(BF16)
        for h in batch:
            acc = _dot(p[h], jnp.concatenate([r[0, :, cols[h]] for r in v_refs], axis=0))
            o_ref[0, :, cols[h]] = acc.astype(o_ref.dtype)
    has_older = base + i + older >= SB_WINDOW
    need_ref[...] = jnp.broadcast_to(jnp.where(has_older, jnp.max(worst), -jnp.inf), need_ref.shape)


def _stick_breaking_window(q, k_bf, v_bf, tq, base, older):
    b, lq, _ = q.shape
    nq = lq // tq
    qspec = pl.BlockSpec((1, tq, SB_WIDTH), lambda s, i: (s, i, 0))
    kspecs = [pl.BlockSpec((1, SB_BLOCK, SB_WIDTH), functools.partial(
        lambda s, i, w: (s, jnp.maximum(base + i - w, 0), 0), w=w)) for w in range(SB_WINDOW)]
    return pl.pallas_call(
        functools.partial(_sb_window_kernel, tq=tq, base=base, older=older),
        grid=(b, nq),
        in_specs=[qspec] + kspecs + kspecs,
        out_specs=[qspec, pl.BlockSpec((1, 1, SUBLANES, HEAD_DIM), lambda s, i: (s, i, 0, 0))],
        out_shape=[jax.ShapeDtypeStruct((b, lq, SB_WIDTH), BF16),
                   jax.ShapeDtypeStruct((b, nq, SUBLANES, HEAD_DIM), F32)],
        compiler_params=_cparams("parallel", "arbitrary"),
        name="stick_breaking_window",
    )(q, k_bf, k_bf, k_bf, v_bf, v_bf, v_bf)


def _layer(x, conv_past, s0, past_k, past_v, chunk, wts):
    b, l, d = x.shape
    m = b * l
    x2 = x.reshape(m, d)
    xn = _rmsnorm(x2, wts["norm1_w"], BF16)
    w_in = wts["w_in"]
    c_q = OFF_A
    c_g = c_q + 3 * SB_WIDTH
    c_ab = c_g + 2 * d
    (ab,) = _matmul(xn, wts["w_ab"], (F32,))
    if m <= 256:
        (proj,) = _matmul(xn, w_in, (F32,), 0, c_ab)
        proj_a, q_sb, gates = proj[:, :OFF_A], proj[:, c_q:c_q + SB_WIDTH].astype(BF16), proj[:, c_g:]
        k_sb, v_sb = proj[:, c_q + SB_WIDTH:c_q + 2 * SB_WIDTH], proj[:, c_q + 2 * SB_WIDTH:c_g]
        k_bf, v_bf = k_sb.astype(BF16), v_sb.astype(BF16)
    else:
        (proj_a,) = _matmul(xn, w_in, (F32,), 0, OFF_A)
        (q_sb,) = _matmul(xn, w_in, (BF16,), c_q, SB_WIDTH)
        k_sb, k_bf = _matmul(xn, w_in, (F32, BF16), c_q + SB_WIDTH, SB_WIDTH)
        v_sb, v_bf = _matmul(xn, w_in, (F32, BF16), c_q + 2 * SB_WIDTH, SB_WIDTH)
        (gates,) = _matmul(xn, w_in, (F32,), c_g, 2 * d)

    o_a, s_new = _gdn(proj_a.reshape(b, l, -1), ab.reshape(b, l, -1), conv_past, s0, wts["conv_w"], wts["A_log"],
                      wts["dt_bias"], wts["gdn_norm_w"], chunk)
    conv_state = proj_a.reshape(b, l, -1)[:, l - (CONV_WIDTH - 1):, :CONV_CH]

    k_bf = k_bf.reshape(b, l, SB_WIDTH)
    v_bf = v_bf.reshape(b, l, SB_WIDTH)
    q3 = q_sb.reshape(b, l, SB_WIDTH)
    if past_k is None:
        o_win, need = _stick_breaking_window(q3, k_bf, v_bf, min(l, SB_BLOCK), 0, 0)

        def full_sweep():
            return _stick_breaking(q3, k_bf, v_bf, 0, min(l, 256))
    else:
        p = past_k.shape[1]
        n_cached = (SB_WINDOW - 1) * SB_BLOCK
        assert p % SB_BLOCK == 0 and p >= n_cached and l <= SB_BLOCK

        def with_cache(new, past, keep):
            return jnp.concatenate([past[:, p - keep:].reshape(b, keep, SB_WIDTH).astype(BF16), new,
                                    jnp.zeros((b, SB_BLOCK - l, SB_WIDTH), BF16)], axis=1)

        o_win, need = _stick_breaking_window(q3, with_cache(k_bf, past_k, n_cached), with_cache(v_bf, past_v, n_cached),
                                             l, SB_WINDOW - 1, (p - n_cached) // SB_BLOCK)

        def full_sweep():
            return _stick_breaking(q3, with_cache(k_bf, past_k, p), with_cache(v_bf, past_v, p), p, l)
    o_b = lax.cond(jnp.max(need) > -SB_UNDERFLOW, full_sweep, lambda: o_win)

    merged = _merge(o_a.reshape(m, GDN_V_WIDTH), o_b.reshape(m, SB_WIDTH), wts["w_gdn_o"], wts["w_sb_o"], gates)
    h = _matmul_residual(merged, wts["w_out"], x2)
    hn = _rmsnorm(h, wts["norm2_w"], BF16)
    hid = _matmul_relu2(hn, wts["w_up"])
    h = _matmul_residual_ktiled(hid, wts["w_down"], h)
    return (h, conv_state, s_new, k_sb.reshape(b, l, SB_HEADS, HEAD_DIM), v_sb.reshape(b, l, SB_HEADS, HEAD_DIM))


def kernel(x_prompt, x_sample, cache_sb_k, cache_sb_v, state_gdn_S, state_gdn_conv, norm1_w, w_in, conv_w, A_log,
           dt_bias, gdn_norm_w, w_gdn_o, w_sb_o, w_out, norm2_w, w_up, w_down, final_norm_w):
    depth = w_in.shape[0]
    assert depth == 1
    w_in0 = w_in[0]
    ab_cols = jnp.pad(w_in0[:, OFF_A:OFF_SB], ((0, 0), (0, HEAD_DIM - 2 * GDN_V_HEADS)))
    wts = {
        "norm1_w": norm1_w[0], "norm2_w": norm2_w[0], "conv_w": conv_w[0], "A_log": A_log[0], "dt_bias": dt_bias[0],
        "gdn_norm_w": gdn_norm_w[0],
        "w_in": _repack_w_in(w_in0), "w_ab": ab_cols.astype(BF16),
        "w_gdn_o": w_gdn_o[0].astype(BF16), "w_sb_o": w_sb_o[0].astype(BF16), "w_out": w_out[0].astype(BF16),
        "w_up": w_up[0].astype(BF16), "w_down": w_down[0].astype(BF16),
    }
    bp, lp, d = x_prompt.shape
    bs, ls, _ = x_sample.shape
    conv0 = jnp.zeros((bp, CONV_WIDTH - 1, CONV_CH), F32)
    s_zero = jnp.zeros((bp, GDN_V_HEADS, HEAD_DIM, HEAD_DIM), F32)
    hp, c_p, s_p, k_p, v_p = _layer(x_prompt, conv0, s_zero, None, None, min(PROMPT_CHUNK, lp), wts)
    hs, c_s, s_s, k_s, v_s = _layer(x_sample, state_gdn_conv[0], state_gdn_S[0], cache_sb_k[0], cache_sb_v[0], ls, wts)
    y_prompt = _rmsnorm(hp, final_norm_w, F32).reshape(bp, lp, d)
    y_sample = _rmsnorm(hs, final_norm_w, F32).reshape(bs, ls, d)
    return (y_prompt, y_sample, k_p[None], v_p[None], s_p[None], c_p[None],
            k_s[None], v_s[None], s_s[None].astype(state_gdn_S.dtype), c_s[None])
```

```python
import functools
import math

import jax
import jax.numpy as jnp
from jax import lax
from jax.experimental import pallas as pl
from jax.experimental.pallas import tpu as pltpu

F32 = jnp.float32
BF16 = jnp.bfloat16

EPS = 1e-6
HEAD_DIM = 128
GDN_QK_HEADS = 8
GDN_V_HEADS = 16
GDN_QK_WIDTH = GDN_QK_HEADS * HEAD_DIM
GDN_V_WIDTH = GDN_V_HEADS * HEAD_DIM
CONV_WIDTH = 4
CONV_CH = 2 * GDN_QK_WIDTH + GDN_V_WIDTH
SB_HEADS = 16
SB_WIDTH = SB_HEADS * HEAD_DIM
SB_BLOCK = 128
SB_UNDERFLOW = 106.0
SB_WINDOW = 3
SB_HEAD_BATCH = 4
PROMPT_CHUNK = 64

V7X_MXU_DIM = 256
V7X_VMEM_LIMIT_BYTES = 56 * 1024 * 1024
SUBLANES = 8

OFF_Z = CONV_CH
OFF_A = OFF_Z + GDN_V_WIDTH
OFF_B = OFF_A + GDN_V_HEADS
OFF_SB = OFF_B + GDN_V_HEADS
OFF_GATE = OFF_SB + 3 * SB_WIDTH


def _cparams(*sem):
    return pltpu.CompilerParams(dimension_semantics=sem, vmem_limit_bytes=V7X_VMEM_LIMIT_BYTES)


def _dot(a, b):
    return jnp.dot(a, b, preferred_element_type=F32)


def _dot_nt(a, b):
    return lax.dot_general(a, b, (((1,), (1,)), ((), ())), preferred_element_type=F32)


def _dot_tn(a, b):
    return lax.dot_general(a, b, (((0,), (0,)), ((), ())), preferred_element_type=F32)


def _sigmoid(x):
    return 1.0 / (1.0 + jnp.exp(-x))


def _softplus(x):
    return jnp.maximum(x, 0.0) + jnp.log(1.0 + jnp.exp(-jnp.abs(x)))


def _split3(x):
    x1 = x.astype(BF16)
    r1 = x - x1.astype(F32)
    x2 = r1.astype(BF16)
    x3 = (r1 - x2.astype(F32)).astype(BF16)
    return x1, x2, x3


def _rmsnorm_kernel(x_ref, w_ref, o_ref):
    x = x_ref[...]
    var = jnp.mean(x * x, axis=-1, keepdims=True)
    o_ref[...] = (x * lax.rsqrt(var + EPS) * w_ref[...]).astype(o_ref.dtype)


def _rmsnorm(x, w, out_dtype):
    m, d = x.shape
    tm = min(m, 256)
    return pl.pallas_call(
        _rmsnorm_kernel,
        grid=(m // tm,),
        in_specs=[pl.BlockSpec((tm, d), lambda i: (i, 0)), pl.BlockSpec((1, d), lambda i: (0, 0))],
        out_specs=pl.BlockSpec((tm, d), lambda i: (i, 0)),
        out_shape=jax.ShapeDtypeStruct((m, d), out_dtype),
        compiler_params=_cparams("parallel"),
        name="rmsnorm",
    )(x, w.reshape(1, d).astype(F32))


def _repack_kernel(a_ref, b_ref, o_ref, *, first_shifted, shift):
    j = pl.program_id(0)

    @pl.when(j < first_shifted)
    def _():
        o_ref[...] = a_ref[...].T.astype(o_ref.dtype)

    @pl.when(j >= first_shifted)
    def _():
        src = jnp.concatenate([a_ref[shift:, :], b_ref[...]], axis=0)
        o_ref[...] = src.T.astype(o_ref.dtype)


def _repack_w_in(wt):
    n_src, k = wt.shape
    tn, tk = 1024, 512
    shift = OFF_SB - OFF_A
    n_out = n_src - shift
    assert OFF_A % tn == 0 and n_out % tn == 0 and k % tk == 0 and tn % shift == 0 and shift % SUBLANES == 0
    return pl.pallas_call(
        functools.partial(_repack_kernel, first_shifted=OFF_A // tn, shift=shift),
        grid=(n_out // tn, k // tk),
        in_specs=[pl.BlockSpec((tn, tk), lambda j, i: (j, i)),
                  pl.BlockSpec((shift, tk), lambda j, i: ((j + 1) * (tn // shift), i))],
        out_specs=pl.BlockSpec((tk, tn), lambda j, i: (i, j)),
        out_shape=jax.ShapeDtypeStruct((k, n_out), BF16),
        compiler_params=_cparams("parallel", "arbitrary"),
        name="repack_w_in",
    )(wt, wt)


def _ab_cols_kernel(a_ref, o_ref, *, n_valid):
    t = a_ref[...].T
    lane = lax.broadcasted_iota(jnp.int32, t.shape, 1)
    o_ref[...] = jnp.where(lane < n_valid, t, 0.0).astype(o_ref.dtype)


def _ab_cols(wt):
    k = wt.shape[1]
    tk = 512
    assert OFF_A % HEAD_DIM == 0 and k % tk == 0
    return pl.pallas_call(
        functools.partial(_ab_cols_kernel, n_valid=OFF_SB - OFF_A),
        grid=(k // tk,),
        in_specs=[pl.BlockSpec((HEAD_DIM, tk), lambda i: (OFF_A // HEAD_DIM, i))],
        out_specs=pl.BlockSpec((tk, HEAD_DIM), lambda i: (i, 0)),
        out_shape=jax.ShapeDtypeStruct((k, HEAD_DIM), BF16),
        compiler_params=_cparams("arbitrary"),
        name="ab_cols",
    )(wt)


def _mm_tiles(m, n, tn_max=1024):
    tm = min(m, 1024)
    tn = min(n, tn_max)
    assert m % tm == 0 and n % tn == 0, (m, n)
    return tm, tn


def _mm_kernel(a_ref, w_ref, *o_refs):
    acc = _dot(a_ref[...], w_ref[...])
    for o_ref in o_refs:
        o_ref[...] = acc.astype(o_ref.dtype)


def _matmul(a, w, out_dtypes, col0=0, n=None):
    m, k = a.shape
    n = w.shape[1] if n is None else n
    tm, tn = _mm_tiles(m, n)
    assert col0 % tn == 0
    jb = col0 // tn
    outs = pl.pallas_call(
        _mm_kernel,
        grid=(m // tm, n // tn),
        in_specs=[pl.BlockSpec((tm, k), lambda i, j: (i, 0)), pl.BlockSpec((k, tn), lambda i, j: (0, j + jb))],
        out_specs=[pl.BlockSpec((tm, tn), lambda i, j: (i, j)) for _ in out_dtypes],
        out_shape=[jax.ShapeDtypeStruct((m, n), dt) for dt in out_dtypes],
        compiler_params=_cparams("parallel", "arbitrary"),
        name="matmul",
    )(a, w)
    return outs


def _mm_relu2_kernel(a_ref, w_ref, o_ref):
    acc = _dot(a_ref[...], w_ref[...])
    o_ref[...] = jnp.square(jnp.maximum(acc, 0.0)).astype(o_ref.dtype)


def _matmul_relu2(a, w):
    m, k = a.shape
    n = w.shape[1]
    tm, tn = _mm_tiles(m, n)
    return pl.pallas_call(
        _mm_relu2_kernel,
        grid=(m // tm, n // tn),
        in_specs=[pl.BlockSpec((tm, k), lambda i, j: (i, 0)), pl.BlockSpec((k, tn), lambda i, j: (0, j))],
        out_specs=pl.BlockSpec((tm, tn), lambda i, j: (i, j)),
        out_shape=jax.ShapeDtypeStruct((m, n), BF16),
        compiler_params=_cparams("parallel", "arbitrary"),
        name="matmul_relu2",
    )(a, w)


def _mm_res_kernel(a_ref, w_ref, r_ref, o_ref):
    o_ref[...] = r_ref[...] + _dot(a_ref[...], w_ref[...])


def _matmul_residual(a, w, res):
    m, k = a.shape
    n = w.shape[1]
    tm, tn = _mm_tiles(m, n)
    return pl.pallas_call(
        _mm_res_kernel,
        grid=(m // tm, n // tn),
        in_specs=[pl.BlockSpec((tm, k), lambda i, j: (i, 0)), pl.BlockSpec((k, tn), lambda i, j: (0, j)),
                  pl.BlockSpec((tm, tn), lambda i, j: (i, j))],
        out_specs=pl.BlockSpec((tm, tn), lambda i, j: (i, j)),
        out_shape=jax.ShapeDtypeStruct((m, n), F32),
        compiler_params=_cparams("parallel", "arbitrary"),
        name="matmul_residual",
    )(a, w, res)


def _mm_kacc_kernel(a_ref, w_ref, r_ref, o_ref):
    @pl.when(pl.program_id(2) == 0)
    def _():
        o_ref[...] = r_ref[...]

    o_ref[...] += _dot(a_ref[...], w_ref[...])


def _matmul_residual_ktiled(a, w, res):
    m, k = a.shape
    n = w.shape[1]
    tm = min(m, 1024)
    tn = min(n, 1024)
    tk = min(k, 2048)
    return pl.pallas_call(
        _mm_kacc_kernel,
        grid=(m // tm, n // tn, k // tk),
        in_specs=[pl.BlockSpec((tm, tk), lambda i, j, l: (i, l)), pl.BlockSpec((tk, tn), lambda i, j, l: (l, j)),
                  pl.BlockSpec((tm, tn), lambda i, j, l: (i, j))],
        out_specs=pl.BlockSpec((tm, tn), lambda i, j, l: (i, j)),
        out_shape=jax.ShapeDtypeStruct((m, n), F32),
        compiler_params=_cparams("parallel", "parallel", "arbitrary"),
        name="matmul_residual_ktiled",
    )(a, w, res)


def _merge_kernel(oa_ref, ob_ref, wa_ref, wb_ref, ga_ref, gb_ref, o_ref):
    ya = _dot(oa_ref[...], wa_ref[...])
    yb = _dot(ob_ref[...], wb_ref[...])
    o_ref[...] = (_sigmoid(ga_ref[...]) * ya + _sigmoid(gb_ref[...]) * yb).astype(o_ref.dtype)


def _merge(oa, ob, wa, wb, gates):
    m, k = oa.shape
    n = wa.shape[1]
    tm, tn = _mm_tiles(m, n, 512)
    nb = n // tn
    return pl.pallas_call(
        _merge_kernel,
        grid=(m // tm, nb),
        in_specs=[pl.BlockSpec((tm, k), lambda i, j: (i, 0)), pl.BlockSpec((tm, k), lambda i, j: (i, 0)),
                  pl.BlockSpec((k, tn), lambda i, j: (0, j)), pl.BlockSpec((k, tn), lambda i, j: (0, j)),
                  pl.BlockSpec((tm, tn), lambda i, j: (i, j)), pl.BlockSpec((tm, tn), lambda i, j: (i, j + nb))],
        out_specs=pl.BlockSpec((tm, tn), lambda i, j: (i, j)),
        out_shape=jax.ShapeDtypeStruct((m, n), BF16),
        compiler_params=_cparams("parallel", "arbitrary"),
        name="merge",
    )(oa, ob, wa, wb, gates, gates)


def _gdn_kernel(qkv_ref, z_ref, abc_ref, ar_ref, past_ref, s0_ref, cw_ref, alc_ref, dtc_ref, alr_ref, dtr_ref,
                gnw_ref, o_ref, s_ref, xp_ref, act_ref, ltri_ref, ubd_ref, cmask_ref, smask_ref, *, chunk, group):
    c_len, g_sz = chunk, group
    r_len = c_len * g_sz
    n_groups = GDN_V_HEADS // g_sz
    rep = GDN_V_HEADS // GDN_QK_HEADS
    c_idx = pl.program_id(1)

    @pl.when(c_idx == 0)
    def _init():
        s_ref[...] = s0_ref[...]
        xp_ref[0:SUBLANES, :] = past_ref[0]
        ri = lax.broadcasted_iota(jnp.int32, (r_len, r_len), 0)
        ci = lax.broadcasted_iota(jnp.int32, (r_len, r_len), 1)
        shift = int(math.log2(c_len))
        same = (ri >> shift) == (ci >> shift)
        cmask_ref[...] = jnp.where(same & (ri >= ci), 1.0, 0.0)
        smask_ref[...] = jnp.where(same & (ri > ci), 1.0, 0.0)
        ubd_ref[...] = jnp.where(same & (ri <= ci), 1.0, 0.0).astype(BF16)
        rc = lax.broadcasted_iota(jnp.int32, (c_len, c_len), 0)
        cc = lax.broadcasted_iota(jnp.int32, (c_len, c_len), 1)
        ltri_ref[...] = jnp.where(rc >= cc, 1.0, 0.0).astype(BF16)

    @pl.when(c_idx > 0)
    def _carry():
        xp_ref[0:SUBLANES, :] = xp_ref[c_len:c_len + SUBLANES, :]

    xp_ref[SUBLANES:SUBLANES + c_len, :] = qkv_ref[0]

    for s in range(CONV_CH // HEAD_DIM):
        cols = slice(s * HEAD_DIM, (s + 1) * HEAD_DIM)
        y = cw_ref[0:1, cols] * xp_ref[SUBLANES - 3:SUBLANES - 3 + c_len, cols]
        for i in range(1, CONV_WIDTH):
            y = y + cw_ref[i:i + 1, cols] * xp_ref[SUBLANES - 3 + i:SUBLANES - 3 + i + c_len, cols]
        y = y * _sigmoid(y)
        if s < 2 * GDN_QK_HEADS:
            y = y * lax.rsqrt(jnp.sum(y * y, axis=-1, keepdims=True) + EPS)
            if s < GDN_QK_HEADS:
                y = y * (HEAD_DIM ** -0.5)
        act_ref[:, cols] = y

    ab = abc_ref[0]
    g_col = -jnp.exp(alc_ref[...]) * _softplus(ab + dtc_ref[...])
    beta_col = _sigmoid(ab)
    ltri = ltri_ref[...]
    gc_col = sum(_dot(ltri, t) for t in _split3(g_col))
    g_row = -jnp.exp(alr_ref[...]) * _softplus(ar_ref[0, 0] + dtr_ref[...])
    ubd = ubd_ref[...]
    gc_row = sum(_dot(t, ubd) for t in _split3(g_row))

    cmask = cmask_ref[...]
    smask = smask_ref[...]
    gnw = gnw_ref[...]

    groups = range(n_groups)
    heads_of = [[grp * g_sz + hb for hb in range(g_sz)] for grp in groups]

    def stacked(off, div):
        return [jnp.concatenate([act_ref[:, off + (h // div) * HEAD_DIM:off + (h // div + 1) * HEAD_DIM]
                                 for h in heads_of[grp]], axis=0) for grp in groups]

    kst = stacked(GDN_QK_WIDTH, rep)
    qst = stacked(0, rep)
    vst = stacked(2 * GDN_QK_WIDTH, 1)
    gcb = [jnp.concatenate([jnp.broadcast_to(gc_col[:, h:h + 1], (c_len, HEAD_DIM)) for h in heads_of[grp]], axis=0)
           for grp in groups]
    bcb = [jnp.concatenate([jnp.broadcast_to(beta_col[:, GDN_V_HEADS + h:GDN_V_HEADS + h + 1], (c_len, HEAD_DIM))
                            for h in heads_of[grp]], axis=0) for grp in groups]
    kst_b = [t.astype(BF16) for t in kst]
    gram = [_dot_nt(kst_b[grp], kst_b[grp]) for grp in groups]
    qk = [_dot_nt(qst[grp].astype(BF16), kst_b[grp]) for grp in groups]
    n_rep = r_len // HEAD_DIM
    aqk, tp, xpow = [], [], []
    for grp in groups:
        gcb2 = jnp.concatenate([gcb[grp]] * n_rep, axis=1)
        nbeta = -bcb[grp]
        nbeta2 = jnp.concatenate([nbeta] * n_rep, axis=1) * smask
        decay = jnp.exp(jnp.minimum(gcb2 - gc_row[grp:grp + 1, :], 0.0)) * cmask
        aqk.append((qk[grp] * decay).astype(BF16))
        nm = (gram[grp] * decay) * nbeta2
        tp.append(nm)
        xpow.append(nm)
    n_steps = int(math.log2(c_len))
    for step in range(n_steps):
        for grp in groups:
            xb = xpow[grp].astype(BF16)
            tp[grp] = tp[grp] + _dot(xb, tp[grp].astype(BF16))
            if step + 1 < n_steps:
                xpow[grp] = _dot(xb, xb)
    eg = [jnp.exp(gcb[grp]) for grp in groups]
    sol = []
    for grp in groups:
        rhs = jnp.concatenate([vst[grp] * bcb[grp], kst[grp] * (bcb[grp] * eg[grp])], axis=1)
        sol.append(rhs + _dot(tp[grp].astype(BF16), rhs.astype(BF16)))
    qss = {}
    ust = []
    for grp in groups:
        qg = qst[grp] * eg[grp]
        us = []
        for hb, h in enumerate(heads_of[grp]):
            rows = slice(hb * c_len, (hb + 1) * c_len)
            s_old = s_ref[0, h]
            wq = jnp.concatenate([sol[grp][rows, HEAD_DIM:2 * HEAD_DIM], qg[rows]], axis=0).astype(BF16)
            r1 = _dot(wq, s_old.astype(BF16))
            u = sol[grp][rows, 0:HEAD_DIM] - r1[0:c_len]
            qss[h] = r1[c_len:2 * c_len]
            gl = gcb[grp][(hb + 1) * c_len - 1:(hb + 1) * c_len, :]
            kd = kst[grp][rows] * jnp.exp(gl - gcb[grp][rows])
            s_ref[0, h] = jnp.exp(gl) * s_old + _dot_tn(kd.astype(BF16), u.astype(BF16))
            us.append(u)
        ust.append(jnp.concatenate(us, axis=0).astype(BF16))
    o_intra = [_dot(aqk[grp], ust[grp]) for grp in groups]
    for grp in groups:
        for hb, h in enumerate(heads_of[grp]):
            rows = slice(hb * c_len, (hb + 1) * c_len)
            o = qss[h] + o_intra[grp][rows]
            o = o * lax.rsqrt(jnp.mean(o * o, axis=-1, keepdims=True) + EPS) * gnw
            zz = z_ref[0, :, h * HEAD_DIM:(h + 1) * HEAD_DIM]
            o_ref[0, :, h * HEAD_DIM:(h + 1) * HEAD_DIM] = (o * (zz * _sigmoid(zz))).astype(o_ref.dtype)


def _gdn(proj_a, ab, conv_past, s0, conv_w, a_log, dt_bias, gnw, chunk):
    b, l, _ = proj_a.shape
    n_chunks = l // chunk
    group = V7X_MXU_DIM // chunk
    n_groups = GDN_V_HEADS // group
    r_len = chunk * group
    a_row = ab[:, :, :GDN_V_HEADS].reshape(b, n_chunks, chunk, n_groups, group)
    a_row = a_row.transpose(0, 1, 3, 4, 2).reshape(b, n_chunks, n_groups, r_len)
    past8 = jnp.pad(conv_past.astype(F32), ((0, 0), (SUBLANES - (CONV_WIDTH - 1), 0), (0, 0)))
    pad16 = HEAD_DIM - GDN_V_HEADS
    alc = jnp.pad(a_log.astype(F32), (0, pad16)).reshape(1, HEAD_DIM)
    dtc = jnp.pad(dt_bias.astype(F32), (0, pad16)).reshape(1, HEAD_DIM)
    alr = jnp.repeat(a_log.astype(F32), chunk).reshape(n_groups, r_len)
    dtr = jnp.repeat(dt_bias.astype(F32), chunk).reshape(n_groups, r_len)
    const = lambda *shape: pl.BlockSpec(shape, lambda i, c: (0,) * len(shape))
    return pl.pallas_call(
        functools.partial(_gdn_kernel, chunk=chunk, group=group),
        grid=(b, n_chunks),
        in_specs=[
            pl.BlockSpec((1, chunk, CONV_CH), lambda i, c: (i, c, 0)),
            pl.BlockSpec((1, chunk, GDN_V_WIDTH), lambda i, c: (i, c, CONV_CH // GDN_V_WIDTH)),
            pl.BlockSpec((1, chunk, HEAD_DIM), lambda i, c: (i, c, 0)),
            pl.BlockSpec((1, 1, n_groups, r_len), lambda i, c: (i, c, 0, 0)),
            pl.BlockSpec((1, SUBLANES, CONV_CH), lambda i, c: (i, 0, 0)),
            pl.BlockSpec((1, GDN_V_HEADS, HEAD_DIM, HEAD_DIM), lambda i, c: (i, 0, 0, 0)),
            const(CONV_WIDTH, CONV_CH), const(1, HEAD_DIM), const(1, HEAD_DIM),
            const(n_groups, r_len), const(n_groups, r_len), const(1, HEAD_DIM),
        ],
        out_specs=[
            pl.BlockSpec((1, chunk, GDN_V_WIDTH), lambda i, c: (i, c, 0)),
            pl.BlockSpec((1, GDN_V_HEADS, HEAD_DIM, HEAD_DIM), lambda i, c: (i, 0, 0, 0)),
        ],
        out_shape=[jax.ShapeDtypeStruct((b, l, GDN_V_WIDTH), BF16),
                   jax.ShapeDtypeStruct((b, GDN_V_HEADS, HEAD_DIM, HEAD_DIM), F32)],
        scratch_shapes=[
            pltpu.VMEM((chunk + SUBLANES, CONV_CH), F32),
            pltpu.VMEM((chunk, CONV_CH), F32),
            pltpu.VMEM((chunk, chunk), BF16),
            pltpu.VMEM((r_len, r_len), BF16),
            pltpu.VMEM((r_len, r_len), F32),
            pltpu.VMEM((r_len, r_len), F32),
        ],
        compiler_params=_cparams("parallel", "arbitrary"),
        name="gated_delta_rule",
    )(proj_a, proj_a, ab, a_row, past8, s0.astype(F32), conv_w.astype(F32), alc, dtc, alr, dtr,
      gnw.astype(F32).reshape(1, HEAD_DIM))


def _sb_kernel(q_ref, kt_ref, v_ref, o_ref, *, tq, q_pos0):
    i = pl.program_id(2)
    q = q_ref[0]
    qpos0 = q_pos0 + i * tq
    row = lax.broadcasted_iota(jnp.int32, (tq, SB_BLOCK), 0)
    col = lax.broadcasted_iota(jnp.int32, (tq, SB_BLOCK), 1)
    ri = lax.broadcasted_iota(jnp.int32, (SB_BLOCK, SB_BLOCK), 0)
    ci = lax.broadcasted_iota(jnp.int32, (SB_BLOCK, SB_BLOCK), 1)
    ustrict = jnp.where(ri > ci, 1.0, 0.0).astype(BF16)
    n_masked = max(tq // SB_BLOCK, 1)
    j_full = qpos0 // SB_BLOCK

    def tile(j, later, acc, masked):
        kt = kt_ref[0, 0, j]
        vj = v_ref[0, pl.ds(pl.multiple_of(j * SB_BLOCK, SB_BLOCK), SB_BLOCK), :]
        z = _dot(q, kt) * (HEAD_DIM ** -0.5)
        ls = -_softplus(z)
        if masked:
            vis = (j * SB_BLOCK + col) < (qpos0 + row)
            lf = jnp.where(vis, ls, 0.0)
        else:
            lf = ls
        within = _dot(lf.astype(BF16), ustrict)
        p = jnp.exp((z + ls) + within + later)
        if masked:
            p = jnp.where(vis, p, 0.0)
        acc = acc + _dot(p.astype(BF16), vj)
        return later + jnp.sum(lf, axis=-1, keepdims=True), acc

    later = jnp.zeros((tq, 1), F32)
    acc = jnp.zeros((tq, HEAD_DIM), F32)
    for m in range(n_masked):
        later, acc = tile(j_full + (n_masked - 1 - m), later, acc, True)

    def cond(carry):
        return jnp.logical_and(carry[0] >= 0, jnp.max(carry[1]) > -SB_UNDERFLOW)

    def body(carry):
        later, acc = tile(carry[0], carry[1], carry[2], False)
        return carry[0] - 1, later, acc

    _, later, acc = lax.while_loop(cond, body, (j_full - 1, later, acc))
    o_ref[0] = acc.astype(o_ref.dtype)


def _stick_breaking(q, k_bf, v_bf, q_pos0, tq):
    b, lq, _ = q.shape
    lk = k_bf.shape[1]
    nkb = lk // SB_BLOCK
    kt = k_bf.reshape(b, nkb, SB_BLOCK, SB_HEADS, HEAD_DIM).transpose(0, 3, 1, 4, 2)
    return pl.pallas_call(
        functools.partial(_sb_kernel, tq=tq, q_pos0=q_pos0),
        grid=(b, SB_HEADS, lq // tq),
        in_specs=[
            pl.BlockSpec((1, tq, HEAD_DIM), lambda s, h, i: (s, i, h)),
            pl.BlockSpec((1, 1, nkb, HEAD_DIM, SB_BLOCK), lambda s, h, i: (s, h, 0, 0, 0)),
            pl.BlockSpec((1, lk, HEAD_DIM), lambda s, h, i: (s, 0, h)),
        ],
        out_specs=pl.BlockSpec((1, tq, HEAD_DIM), lambda s, h, i: (s, i, h)),
        out_shape=jax.ShapeDtypeStruct((b, lq, SB_WIDTH), BF16),
        compiler_params=_cparams("parallel", "parallel", "arbitrary"),
        name="stick_breaking",
    )(q, kt, v_bf)


def _sb_window_kernel(q_ref, k0_ref, k1_ref, k2_ref, v0_ref, v1_ref, v2_ref, o_ref, need_ref, *, tq, base, older):
    i = pl.program_id(1)
    row = lax.broadcasted_iota(jnp.int32, (tq, SB_WINDOW * SB_BLOCK), 0)
    col = lax.broadcasted_iota(jnp.int32, (tq, SB_WINDOW * SB_BLOCK), 1)
    assert SB_WINDOW == 3 and SB_HEAD_BATCH % 2 == 0
    ri = lax.broadcasted_iota(jnp.int32, (2 * SB_BLOCK, 2 * SB_BLOCK), 0)
    ci = lax.broadcasted_iota(jnp.int32, (2 * SB_BLOCK, 2 * SB_BLOCK), 1)
    same = (ri < SB_BLOCK) == (ci < SB_BLOCK)
    upair = jnp.where(same & (ri > ci), 1.0, 0.0).astype(BF16)
    k_refs = (k0_ref, k1_ref, k2_ref)
    v_refs = (v0_ref, v1_ref, v2_ref)
    n_win = SB_WINDOW * SB_BLOCK
    limit = row
    for w in range(1, SB_WINDOW):
        limit = jnp.where(col < w * SB_BLOCK, limit, jnp.where(base + i - w >= 0, n_win, 0))
    vis = col < limit
    worst = jnp.full((tq, 1), -jnp.inf, F32)
    for h0 in range(0, SB_HEADS, SB_HEAD_BATCH):
        batch = range(h0, h0 + SB_HEAD_BATCH)
        cols = {h: slice(h * HEAD_DIM, (h + 1) * HEAD_DIM) for h in batch}
        z = {h: _dot_nt(q_ref[0, :, cols[h]], jnp.concatenate([r[0, :, cols[h]] for r in k_refs], axis=0))
             * (HEAD_DIM ** -0.5) for h in batch}
        ls = {h: -_softplus(z[h]) for h in batch}
        lf = {h: jnp.where(vis, ls[h], 0.0) for h in batch}
        lf_b = {h: lf[h].astype(BF16) for h in batch}
        within = {h: [] for h in batch}
        for h in batch:
            both = _dot(lf_b[h][:, 0:2 * SB_BLOCK], upair)
            within[h] += [both[:, 0:SB_BLOCK], both[:, SB_BLOCK:2 * SB_BLOCK]]
        for h in range(h0, h0 + SB_HEAD_BATCH, 2):
            both = _dot(jnp.concatenate([lf_b[h][:, 2 * SB_BLOCK:], lf_b[h + 1][:, 2 * SB_BLOCK:]], axis=1), upair)
            within[h].append(both[:, 0:SB_BLOCK])
            within[h + 1].append(both[:, SB_BLOCK:2 * SB_BLOCK])
        p = {}
        for h in batch:
            later = jnp.zeros((tq, 1), F32)
            shifted = []
            for w in range(SB_WINDOW):
                shifted.append(within[h][w] + later)
                later = later + jnp.sum(lf[h][:, w * SB_BLOCK:(w + 1) * SB_BLOCK], axis=-1, keepdims=True)
            worst = jnp.maximum(worst, later)
            p[h] = jnp.where(vis, jnp.exp((z[h] + ls[h]) + jnp.concatenate(shifted, axis=1)), 0.0).astype(BF16)
        for h in batch:
            acc = _dot(p[h], jnp.concatenate([r[0, :, cols[h]] for r in v_refs], axis=0))
            o_ref[0, :, cols[h]] = acc.astype(o_ref.dtype)
    has_older = base + i + older >= SB_WINDOW
    need_ref[...] = jnp.broadcast_to(jnp.where(has_older, jnp.max(worst), -jnp.inf), need_ref.shape)


def _stick_breaking_window(q, k_bf, v_bf, tq, base, older):
    b, lq, _ = q.shape
    nq = lq // tq
    qspec = pl.BlockSpec((1, tq, SB_WIDTH), lambda s, i: (s, i, 0))
    kspecs = [pl.BlockSpec((1, SB_BLOCK, SB_WIDTH), functools.partial(
        lambda s, i, w: (s, jnp.maximum(base + i - w, 0), 0), w=w)) for w in range(SB_WINDOW)]
    return pl.pallas_call(
        functools.partial(_sb_window_kernel, tq=tq, base=base, older=older),
        grid=(b, nq),
        in_specs=[qspec] + kspecs + kspecs,
        out_specs=[qspec, pl.BlockSpec((1, 1, SUBLANES, HEAD_DIM), lambda s, i: (s, i, 0, 0))],
        out_shape=[jax.ShapeDtypeStruct((b, lq, SB_WIDTH), BF16),
                   jax.ShapeDtypeStruct((b, nq, SUBLANES, HEAD_DIM), F32)],
        compiler_params=_cparams("parallel", "arbitrary"),
        name="stick_breaking_window",
    )(q, k_bf, k_bf, k_bf, v_bf, v_bf, v_bf)


def _layer(x, conv_past, s0, past_k, past_v, chunk, wts):
    b, l, d = x.shape
    m = b * l
    x2 = x.reshape(m, d)
    xn = _rmsnorm(x2, wts["norm1_w"], BF16)
    w_in = wts["w_in"]
    c_q = OFF_A
    c_g = c_q + 3 * SB_WIDTH
    c_ab = c_g + 2 * d
    (ab,) = _matmul(xn, wts["w_ab"], (F32,))
    if m <= 256:
        (proj,) = _matmul(xn, w_in, (F32,), 0, c_ab)
        proj_a, q_sb, gates = proj[:, :OFF_A], proj[:, c_q:c_q + SB_WIDTH].astype(BF16), proj[:, c_g:]
        k_sb, v_sb = proj[:, c_q + SB_WIDTH:c_q + 2 * SB_WIDTH], proj[:, c_q + 2 * SB_WIDTH:c_g]
        k_bf, v_bf = k_sb.astype(BF16), v_sb.astype(BF16)
    else:
        (proj_a,) = _matmul(xn, w_in, (F32,), 0, OFF_A)
        (q_sb,) = _matmul(xn, w_in, (BF16,), c_q, SB_WIDTH)
        k_sb, k_bf = _matmul(xn, w_in, (F32, BF16), c_q + SB_WIDTH, SB_WIDTH)
        v_sb, v_bf = _matmul(xn, w_in, (F32, BF16), c_q + 2 * SB_WIDTH, SB_WIDTH)
        (gates,) = _matmul(xn, w_in, (F32,), c_g, 2 * d)

    o_a, s_new = _gdn(proj_a.reshape(b, l, -1), ab.reshape(b, l, -1), conv_past, s0, wts["conv_w"], wts["A_log"],
                      wts["dt_bias"], wts["gdn_norm_w"], chunk)
    conv_state = proj_a.reshape(b, l, -1)[:, l - (CONV_WIDTH - 1):, :CONV_CH]

    k_bf = k_bf.reshape(b, l, SB_WIDTH)
    v_bf = v_bf.reshape(b, l, SB_WIDTH)
    q3 = q_sb.reshape(b, l, SB_WIDTH)
    if past_k is None:
        o_win, need = _stick_breaking_window(q3, k_bf, v_bf, min(l, SB_BLOCK), 0, 0)

        def full_sweep():
            return _stick_breaking(q3, k_bf, v_bf, 0, min(l, 256))
    else:
        p = past_k.shape[1]
        n_cached = (SB_WINDOW - 1) * SB_BLOCK
        assert p % SB_BLOCK == 0 and p >= n_cached and l <= SB_BLOCK

        def with_cache(new, past, keep):
            return jnp.concatenate([past[:, p - keep:].reshape(b, keep, SB_WIDTH).astype(BF16), new,
                                    jnp.zeros((b, SB_BLOCK - l, SB_WIDTH), BF16)], axis=1)

        o_win, need = _stick_breaking_window(q3, with_cache(k_bf, past_k, n_cached), with_cache(v_bf, past_v, n_cached),
                                             l, SB_WINDOW - 1, (p - n_cached) // SB_BLOCK)

        def full_sweep():
            return _stick_breaking(q3, with_cache(k_bf, past_k, p), with_cache(v_bf, past_v, p), p, l)
    o_b = lax.cond(jnp.max(need) > -SB_UNDERFLOW, full_sweep, lambda: o_win)

    merged = _merge(o_a.reshape(m, GDN_V_WIDTH), o_b.reshape(m, SB_WIDTH), wts["w_gdn_o"], wts["w_sb_o"], gates)
    h = _matmul_residual(merged, wts["w_out"], x2)
    hn = _rmsnorm(h, wts["norm2_w"], BF16)
    hid = _matmul_relu2(hn, wts["w_up"])
    h = _matmul_residual_ktiled(hid, wts["w_down"], h)
    return (h, conv_state, s_new, k_sb.reshape(b, l, SB_HEADS, HEAD_DIM), v_sb.reshape(b, l, SB_HEADS, HEAD_DIM))


def kernel(x_prompt, x_sample, cache_sb_k, cache_sb_v, state_gdn_S, state_gdn_conv, norm1_w, w_in, conv_w, A_log,
           dt_bias, gdn_norm_w, w_gdn_o, w_sb_o, w_out, norm2_w, w_up, w_down, final_norm_w):
    depth = w_in.shape[0]
    assert depth == 1
    w_in_t = jnp.swapaxes(w_in[0], 0, 1)
    wts = {
        "norm1_w": norm1_w[0], "norm2_w": norm2_w[0], "conv_w": conv_w[0], "A_log": A_log[0], "dt_bias": dt_bias[0],
        "gdn_norm_w": gdn_norm_w[0],
        "w_in": _repack_w_in(w_in_t), "w_ab": _ab_cols(w_in_t),
        "w_gdn_o": w_gdn_o[0].astype(BF16), "w_sb_o": w_sb_o[0].astype(BF16), "w_out": w_out[0].astype(BF16),
        "w_up": w_up[0].astype(BF16), "w_down": w_down[0].astype(BF16),
    }
    bp, lp, d = x_prompt.shape
    bs, ls, _ = x_sample.shape
    conv0 = jnp.zeros((bp, CONV_WIDTH - 1, CONV_CH), F32)
    s_zero = jnp.zeros((bp, GDN_V_HEADS, HEAD_DIM, HEAD_DIM), F32)
    hp, c_p, s_p, k_p, v_p = _layer(x_prompt, conv0, s_zero, None, None, min(PROMPT_CHUNK, lp), wts)
    hs, c_s, s_s, k_s, v_s = _layer(x_sample, state_gdn_conv[0], state_gdn_S[0], cache_sb_k[0], cache_sb_v[0], ls, wts)
    y_prompt = _rmsnorm(hp, final_norm_w, F32).reshape(bp, lp, d)
    y_sample = _rmsnorm(hs, final_norm_w, F32).reshape(bs, ls, d)
    return (y_prompt, y_sample, k_p[None], v_p[None], s_p[None], c_p[None],
            k_s[None], v_s[None], s_s[None].astype(state_gdn_S.dtype), c_s[None])
```

```python
import functools
import math

import jax
import jax.numpy as jnp
from jax import lax
from jax.experimental import pallas as pl
from jax.experimental.pallas import tpu as pltpu

F32 = jnp.float32
BF16 = jnp.bfloat16

EPS = 1e-6
HEAD_DIM = 128
GDN_QK_HEADS = 8
GDN_V_HEADS = 16
GDN_QK_WIDTH = GDN_QK_HEADS * HEAD_DIM
GDN_V_WIDTH = GDN_V_HEADS * HEAD_DIM
CONV_WIDTH = 4
CONV_CH = 2 * GDN_QK_WIDTH + GDN_V_WIDTH
SB_HEADS = 16
SB_WIDTH = SB_HEADS * HEAD_DIM
SB_BLOCK = 128
SB_UNDERFLOW = 106.0
SB_WINDOW = 3
SB_HEAD_BATCH = 4
PROMPT_CHUNK = 64

V7X_MXU_DIM = 256
V7X_VMEM_LIMIT_BYTES = 56 * 1024 * 1024
SUBLANES = 8

OFF_Z = CONV_CH
OFF_A = OFF_Z + GDN_V_WIDTH
OFF_B = OFF_A + GDN_V_HEADS
OFF_SB = OFF_B + GDN_V_HEADS
OFF_GATE = OFF_SB + 3 * SB_WIDTH


def _cparams(*sem):
    return pltpu.CompilerParams(dimension_semantics=sem, vmem_limit_bytes=V7X_VMEM_LIMIT_BYTES)


def _dot(a, b):
    return jnp.dot(a, b, preferred_element_type=F32)


def _dot_nt(a, b):
    return lax.dot_general(a, b, (((1,), (1,)), ((), ())), preferred_element_type=F32)


def _dot_tn(a, b):
    return lax.dot_general(a, b, (((0,), (0,)), ((), ())), preferred_element_type=F32)


def _sigmoid(x):
    return 1.0 / (1.0 + jnp.exp(-x))


def _softplus(x):
    return jnp.maximum(x, 0.0) + jnp.log(1.0 + jnp.exp(-jnp.abs(x)))


def _split3(x):
    x1 = x.astype(BF16)
    r1 = x - x1.astype(F32)
    x2 = r1.astype(BF16)
    x3 = (r1 - x2.astype(F32)).astype(BF16)
    return x1, x2, x3


def _rmsnorm_kernel(x_ref, w_ref, o_ref):
    x = x_ref[...]
    var = jnp.mean(x * x, axis=-1, keepdims=True)
    o_ref[...] = (x * lax.rsqrt(var + EPS) * w_ref[...]).astype(o_ref.dtype)


def _rmsnorm(x, w, out_dtype):
    m, d = x.shape
    tm = min(m, 256)
    return pl.pallas_call(
        _rmsnorm_kernel,
        grid=(m // tm,),
        in_specs=[pl.BlockSpec((tm, d), lambda i: (i, 0)), pl.BlockSpec((1, d), lambda i: (0, 0))],
        out_specs=pl.BlockSpec((tm, d), lambda i: (i, 0)),
        out_shape=jax.ShapeDtypeStruct((m, d), out_dtype),
        compiler_params=_cparams("parallel"),
        name="rmsnorm",
    )(x, w.reshape(1, d).astype(F32))


def _repack_kernel(a_ref, b_ref, o_ref, *, first_shifted, shift):
    j = pl.program_id(0)

    @pl.when(j < first_shifted)
    def _():
        o_ref[...] = a_ref[...].T.astype(o_ref.dtype)

    @pl.when(j >= first_shifted)
    def _():
        src = jnp.concatenate([a_ref[shift:, :], b_ref[...]], axis=0)
        o_ref[...] = src.T.astype(o_ref.dtype)


def _repack_w_in(wt):
    n_src, k = wt.shape
    tn, tk = 1024, 512
    shift = OFF_SB - OFF_A
    n_out = n_src - shift
    assert OFF_A % tn == 0 and n_out % tn == 0 and k % tk == 0 and tn % shift == 0 and shift % SUBLANES == 0
    return pl.pallas_call(
        functools.partial(_repack_kernel, first_shifted=OFF_A // tn, shift=shift),
        grid=(n_out // tn, k // tk),
        in_specs=[pl.BlockSpec((tn, tk), lambda j, i: (j, i)),
                  pl.BlockSpec((shift, tk), lambda j, i: ((j + 1) * (tn // shift), i))],
        out_specs=pl.BlockSpec((tk, tn), lambda j, i: (i, j)),
        out_shape=jax.ShapeDtypeStruct((k, n_out), BF16),
        compiler_params=_cparams("parallel", "arbitrary"),
        name="repack_w_in",
    )(wt, wt)


def _ab_cols_kernel(a_ref, o_ref, *, n_valid):
    t = a_ref[...].T
    lane = lax.broadcasted_iota(jnp.int32, t.shape, 1)
    o_ref[...] = jnp.where(lane < n_valid, t, 0.0).astype(o_ref.dtype)


def _ab_cols(wt):
    k = wt.shape[1]
    tk = 512
    assert OFF_A % HEAD_DIM == 0 and k % tk == 0
    return pl.pallas_call(
        functools.partial(_ab_cols_kernel, n_valid=OFF_SB - OFF_A),
        grid=(k // tk,),
        in_specs=[pl.BlockSpec((HEAD_DIM, tk), lambda i: (OFF_A // HEAD_DIM, i))],
        out_specs=pl.BlockSpec((tk, HEAD_DIM), lambda i: (i, 0)),
        out_shape=jax.ShapeDtypeStruct((k, HEAD_DIM), BF16),
        compiler_params=_cparams("arbitrary"),
        name="ab_cols",
    )(wt)


def _mm_tiles(m, n, tn_max=1024):
    tm = min(m, 1024)
    tn = min(n, tn_max)
    assert m % tm == 0 and n % tn == 0, (m, n)
    return tm, tn


def _mm_kernel(a_ref, w_ref, *o_refs):
    acc = _dot(a_ref[...], w_ref[...])
    for o_ref in o_refs:
        o_ref[...] = acc.astype(o_ref.dtype)


def _matmul(a, w, out_dtypes, col0=0, n=None):
    m, k = a.shape
    n = w.shape[1] if n is None else n
    tm, tn = _mm_tiles(m, n)
    assert col0 % tn == 0
    jb = col0 // tn
    outs = pl.pallas_call(
        _mm_kernel,
        grid=(m // tm, n // tn),
        in_specs=[pl.BlockSpec((tm, k), lambda i, j: (i, 0)), pl.BlockSpec((k, tn), lambda i, j: (0, j + jb))],
        out_specs=[pl.BlockSpec((tm, tn), lambda i, j: (i, j)) for _ in out_dtypes],
        out_shape=[jax.ShapeDtypeStruct((m, n), dt) for dt in out_dtypes],
        compiler_params=_cparams("parallel", "arbitrary"),
        name="matmul",
    )(a, w)
    return outs


def _mm_relu2_kernel(a_ref, w_ref, ss_ref, o_ref, *, n_norm):
    tn = o_ref.shape[1]
    sumsq = ss_ref[:, 0:HEAD_DIM]
    for c in range(1, ss_ref.shape[1] // HEAD_DIM):
        sumsq = sumsq + ss_ref[:, c * HEAD_DIM:(c + 1) * HEAD_DIM]
    inv_rms = lax.rsqrt(sumsq * (1.0 / n_norm) + EPS)
    acc = _dot(a_ref[...], w_ref[...]) * jnp.concatenate([inv_rms] * (tn // HEAD_DIM), axis=1)
    o_ref[...] = jnp.square(jnp.maximum(acc, 0.0)).astype(o_ref.dtype)


def _matmul_relu2(a, w, sumsq):
    m, k = a.shape
    n = w.shape[1]
    tm, tn = _mm_tiles(m, n)
    return pl.pallas_call(
        functools.partial(_mm_relu2_kernel, n_norm=k),
        grid=(m // tm, n // tn),
        in_specs=[pl.BlockSpec((tm, k), lambda i, j: (i, 0)), pl.BlockSpec((k, tn), lambda i, j: (0, j)),
                  pl.BlockSpec((tm, sumsq.shape[1]), lambda i, j: (i, 0))],
        out_specs=pl.BlockSpec((tm, tn), lambda i, j: (i, j)),
        out_shape=jax.ShapeDtypeStruct((m, n), BF16),
        compiler_params=_cparams("parallel", "arbitrary"),
        name="matmul_relu2",
    )(a, w, sumsq)


def _mm_res_kernel(a_ref, w_ref, r_ref, g_ref, o_ref, ob_ref, ss_ref):
    h = r_ref[...] + _dot(a_ref[...], w_ref[...])
    o_ref[...] = h
    ob_ref[...] = (h * g_ref[...]).astype(ob_ref.dtype)
    ss_ref[...] = jnp.broadcast_to(jnp.sum(h * h, axis=-1, keepdims=True), ss_ref.shape)


def _matmul_residual(a, w, res, gain):
    m, k = a.shape
    n = w.shape[1]
    tm, tn = _mm_tiles(m, n, 512)
    tile = pl.BlockSpec((tm, tn), lambda i, j: (i, j))
    return pl.pallas_call(
        _mm_res_kernel,
        grid=(m // tm, n // tn),
        in_specs=[pl.BlockSpec((tm, k), lambda i, j: (i, 0)), pl.BlockSpec((k, tn), lambda i, j: (0, j)), tile,
                  pl.BlockSpec((1, tn), lambda i, j: (0, j))],
        out_specs=[tile, tile, pl.BlockSpec((tm, HEAD_DIM), lambda i, j: (i, j))],
        out_shape=[jax.ShapeDtypeStruct((m, n), F32), jax.ShapeDtypeStruct((m, n), BF16),
                   jax.ShapeDtypeStruct((m, (n // tn) * HEAD_DIM), F32)],
        compiler_params=_cparams("parallel", "arbitrary"),
        name="matmul_residual",
    )(a, w, res, gain.reshape(1, n).astype(F32))


def _mm_kacc_kernel(a_ref, w_ref, r_ref, o_ref):
    @pl.when(pl.program_id(2) == 0)
    def _():
        o_ref[...] = r_ref[...]

    o_ref[...] += _dot(a_ref[...], w_ref[...])


def _matmul_residual_ktiled(a, w, res):
    m, k = a.shape
    n = w.shape[1]
    tm = min(m, 1024)
    tn = min(n, 1024)
    tk = min(k, 2048)
    return pl.pallas_call(
        _mm_kacc_kernel,
        grid=(m // tm, n // tn, k // tk),
        in_specs=[pl.BlockSpec((tm, tk), lambda i, j, l: (i, l)), pl.BlockSpec((tk, tn), lambda i, j, l: (l, j)),
                  pl.BlockSpec((tm, tn), lambda i, j, l: (i, j))],
        out_specs=pl.BlockSpec((tm, tn), lambda i, j, l: (i, j)),
        out_shape=jax.ShapeDtypeStruct((m, n), F32),
        compiler_params=_cparams("parallel", "parallel", "arbitrary"),
        name="matmul_residual_ktiled",
    )(a, w, res)


def _merge_kernel(oa_ref, ob_ref, wa_ref, wb_ref, ga_ref, gb_ref, o_ref):
    ya = _dot(oa_ref[...], wa_ref[...])
    yb = _dot(ob_ref[...], wb_ref[...])
    o_ref[...] = (_sigmoid(ga_ref[...]) * ya + _sigmoid(gb_ref[...]) * yb).astype(o_ref.dtype)


def _merge(oa, ob, wa, wb, gates):
    m, k = oa.shape
    n = wa.shape[1]
    tm, tn = _mm_tiles(m, n, 512)
    nb = n // tn
    return pl.pallas_call(
        _merge_kernel,
        grid=(m // tm, nb),
        in_specs=[pl.BlockSpec((tm, k), lambda i, j: (i, 0)), pl.BlockSpec((tm, k), lambda i, j: (i, 0)),
                  pl.BlockSpec((k, tn), lambda i, j: (0, j)), pl.BlockSpec((k, tn), lambda i, j: (0, j)),
                  pl.BlockSpec((tm, tn), lambda i, j: (i, j)), pl.BlockSpec((tm, tn), lambda i, j: (i, j + nb))],
        out_specs=pl.BlockSpec((tm, tn), lambda i, j: (i, j)),
        out_shape=jax.ShapeDtypeStruct((m, n), BF16),
        compiler_params=_cparams("parallel", "arbitrary"),
        name="merge",
    )(oa, ob, wa, wb, gates, gates)


def _gdn_kernel(qkv_ref, z_ref, abc_ref, ar_ref, past_ref, s0_ref, cw_ref, alc_ref, dtc_ref, alr_ref, dtr_ref,
                gnw_ref, o_ref, s_ref, xp_ref, act_ref, ltri_ref, ubd_ref, cmask_ref, smask_ref, *, chunk, group):
    c_len, g_sz = chunk, group
    r_len = c_len * g_sz
    n_groups = GDN_V_HEADS // g_sz
    rep = GDN_V_HEADS // GDN_QK_HEADS
    c_idx = pl.program_id(1)

    @pl.when(c_idx == 0)
    def _init():
        s_ref[...] = s0_ref[...]
        xp_ref[0:SUBLANES, :] = past_ref[0]
        ri = lax.broadcasted_iota(jnp.int32, (r_len, r_len), 0)
        ci = lax.broadcasted_iota(jnp.int32, (r_len, r_len), 1)
        shift = int(math.log2(c_len))
        same = (ri >> shift) == (ci >> shift)
        cmask_ref[...] = jnp.where(same & (ri >= ci), 1.0, 0.0)
        smask_ref[...] = jnp.where(same & (ri > ci), 1.0, 0.0)
        ubd_ref[...] = jnp.where(same & (ri <= ci), 1.0, 0.0).astype(BF16)
        rc = lax.broadcasted_iota(jnp.int32, (c_len, c_len), 0)
        cc = lax.broadcasted_iota(jnp.int32, (c_len, c_len), 1)
        ltri_ref[...] = jnp.where(rc >= cc, 1.0, 0.0).astype(BF16)

    @pl.when(c_idx > 0)
    def _carry():
        xp_ref[0:SUBLANES, :] = xp_ref[c_len:c_len + SUBLANES, :]

    xp_ref[SUBLANES:SUBLANES + c_len, :] = qkv_ref[0]

    for s in range(CONV_CH // HEAD_DIM):
        cols = slice(s * HEAD_DIM, (s + 1) * HEAD_DIM)
        y = cw_ref[0:1, cols] * xp_ref[SUBLANES - 3:SUBLANES - 3 + c_len, cols]
        for i in range(1, CONV_WIDTH):
            y = y + cw_ref[i:i + 1, cols] * xp_ref[SUBLANES - 3 + i:SUBLANES - 3 + i + c_len, cols]
        y = y * _sigmoid(y)
        if s < 2 * GDN_QK_HEADS:
            y = y * lax.rsqrt(jnp.sum(y * y, axis=-1, keepdims=True) + EPS)
            if s < GDN_QK_HEADS:
                y = y * (HEAD_DIM ** -0.5)
        act_ref[:, cols] = y

    ab = abc_ref[0]
    g_col = -jnp.exp(alc_ref[...]) * _softplus(ab + dtc_ref[...])
    beta_col = _sigmoid(ab)
    ltri = ltri_ref[...]
    gc_col = sum(_dot(ltri, t) for t in _split3(g_col))
    g_row = -jnp.exp(alr_ref[...]) * _softplus(ar_ref[0, 0] + dtr_ref[...])
    ubd = ubd_ref[...]
    gc_row = sum(_dot(t, ubd) for t in _split3(g_row))

    cmask = cmask_ref[...]
    smask = smask_ref[...]
    gnw = gnw_ref[...]

    groups = range(n_groups)
    heads_of = [[grp * g_sz + hb for hb in range(g_sz)] for grp in groups]

    def stacked(off, div):
        return [jnp.concatenate([act_ref[:, off + (h // div) * HEAD_DIM:off + (h // div + 1) * HEAD_DIM]
                                 for h in heads_of[grp]], axis=0) for grp in groups]

    kst = stacked(GDN_QK_WIDTH, rep)
    qst = stacked(0, rep)
    vst = stacked(2 * GDN_QK_WIDTH, 1)
    gcb = [jnp.concatenate([jnp.broadcast_to(gc_col[:, h:h + 1], (c_len, HEAD_DIM)) for h in heads_of[grp]], axis=0)
           for grp in groups]
    bcb = [jnp.concatenate([jnp.broadcast_to(beta_col[:, GDN_V_HEADS + h:GDN_V_HEADS + h + 1], (c_len, HEAD_DIM))
                            for h in heads_of[grp]], axis=0) for grp in groups]
    kst_b = [t.astype(BF16) for t in kst]
    gram = [_dot_nt(kst_b[grp], kst_b[grp]) for grp in groups]
    qk = [_dot_nt(qst[grp].astype(BF16), kst_b[grp]) for grp in groups]
    n_rep = r_len // HEAD_DIM
    aqk, tp, xpow = [], [], []
    for grp in groups:
        gcb2 = jnp.concatenate([gcb[grp]] * n_rep, axis=1)
        nbeta = -bcb[grp]
        nbeta2 = jnp.concatenate([nbeta] * n_rep, axis=1) * smask
        decay = jnp.exp(jnp.minimum(gcb2 - gc_row[grp:grp + 1, :], 0.0)) * cmask
        aqk.append((qk[grp] * decay).astype(BF16))
        nm = (gram[grp] * decay) * nbeta2
        tp.append(nm)
        xpow.append(nm)
    n_steps = int(math.log2(c_len))
    for step in range(n_steps):
        for grp in groups:
            xb = xpow[grp].astype(BF16)
            tp[grp] = tp[grp] + _dot(xb, tp[grp].astype(BF16))
            if step + 1 < n_steps:
                xpow[grp] = _dot(xb, xb)
    eg = [jnp.exp(gcb[grp]) for grp in groups]
    sol = []
    for grp in groups:
        rhs = jnp.concatenate([vst[grp] * bcb[grp], kst[grp] * (bcb[grp] * eg[grp])], axis=1)
        sol.append(rhs + _dot(tp[grp].astype(BF16), rhs.astype(BF16)))
    qss = {}
    ust = []
    for grp in groups:
        qg = qst[grp] * eg[grp]
        us = []
        for hb, h in enumerate(heads_of[grp]):
            rows = slice(hb * c_len, (hb + 1) * c_len)
            s_old = s_ref[0, h]
            wq = jnp.concatenate([sol[grp][rows, HEAD_DIM:2 * HEAD_DIM], qg[rows]], axis=0).astype(BF16)
            r1 = _dot(wq, s_old.astype(BF16))
            u = sol[grp][rows, 0:HEAD_DIM] - r1[0:c_len]
            qss[h] = r1[c_len:2 * c_len]
            gl = gcb[grp][(hb + 1) * c_len - 1:(hb + 1) * c_len, :]
            kd = kst[grp][rows] * jnp.exp(gl - gcb[grp][rows])
            s_ref[0, h] = jnp.exp(gl) * s_old + _dot_tn(kd.astype(BF16), u.astype(BF16))
            us.append(u)
        ust.append(jnp.concatenate(us, axis=0).astype(BF16))
    o_intra = [_dot(aqk[grp], ust[grp]) for grp in groups]
    for grp in groups:
        for hb, h in enumerate(heads_of[grp]):
            rows = slice(hb * c_len, (hb + 1) * c_len)
            o = qss[h] + o_intra[grp][rows]
            o = o * lax.rsqrt(jnp.mean(o * o, axis=-1, keepdims=True) + EPS) * gnw
            zz = z_ref[0, :, h * HEAD_DIM:(h + 1) * HEAD_DIM]
            o_ref[0, :, h * HEAD_DIM:(h + 1) * HEAD_DIM] = (o * (zz * _sigmoid(zz))).astype(o_ref.dtype)


def _gdn(proj_a, ab, conv_past, s0, conv_w, a_log, dt_bias, gnw, chunk):
    b, l, _ = proj_a.shape
    n_chunks = l // chunk
    group = V7X_MXU_DIM // chunk
    n_groups = GDN_V_HEADS // group
    r_len = chunk * group
    a_row = ab[:, :, :GDN_V_HEADS].reshape(b, n_chunks, chunk, n_groups, group)
    a_row = a_row.transpose(0, 1, 3, 4, 2).reshape(b, n_chunks, n_groups, r_len)
    past8 = jnp.pad(conv_past.astype(F32), ((0, 0), (SUBLANES - (CONV_WIDTH - 1), 0), (0, 0)))
    pad16 = HEAD_DIM - GDN_V_HEADS
    alc = jnp.pad(a_log.astype(F32), (0, pad16)).reshape(1, HEAD_DIM)
    dtc = jnp.pad(dt_bias.astype(F32), (0, pad16)).reshape(1, HEAD_DIM)
    alr = jnp.repeat(a_log.astype(F32), chunk).reshape(n_groups, r_len)
    dtr = jnp.repeat(dt_bias.astype(F32), chunk).reshape(n_groups, r_len)
    const = lambda *shape: pl.BlockSpec(shape, lambda i, c: (0,) * len(shape))
    return pl.pallas_call(
        functools.partial(_gdn_kernel, chunk=chunk, group=group),
        grid=(b, n_chunks),
        in_specs=[
            pl.BlockSpec((1, chunk, CONV_CH), lambda i, c: (i, c, 0)),
            pl.BlockSpec((1, chunk, GDN_V_WIDTH), lambda i, c: (i, c, CONV_CH // GDN_V_WIDTH)),
            pl.BlockSpec((1, chunk, HEAD_DIM), lambda i, c: (i, c, 0)),
            pl.BlockSpec((1, 1, n_groups, r_len), lambda i, c: (i, c, 0, 0)),
            pl.BlockSpec((1, SUBLANES, CONV_CH), lambda i, c: (i, 0, 0)),
            pl.BlockSpec((1, GDN_V_HEADS, HEAD_DIM, HEAD_DIM), lambda i, c: (i, 0, 0, 0)),
            const(CONV_WIDTH, CONV_CH), const(1, HEAD_DIM), const(1, HEAD_DIM),
            const(n_groups, r_len), const(n_groups, r_len), const(1, HEAD_DIM),
        ],
        out_specs=[
            pl.BlockSpec((1, chunk, GDN_V_WIDTH), lambda i, c: (i, c, 0)),
            pl.BlockSpec((1, GDN_V_HEADS, HEAD_DIM, HEAD_DIM), lambda i, c: (i, 0, 0, 0)),
        ],
        out_shape=[jax.ShapeDtypeStruct((b, l, GDN_V_WIDTH), BF16),
                   jax.ShapeDtypeStruct((b, GDN_V_HEADS, HEAD_DIM, HEAD_DIM), F32)],
        scratch_shapes=[
            pltpu.VMEM((chunk + SUBLANES, CONV_CH), F32),
            pltpu.VMEM((chunk, CONV_CH), F32),
            pltpu.VMEM((chunk, chunk), BF16),
            pltpu.VMEM((r_len, r_len), BF16),
            pltpu.VMEM((r_len, r_len), F32),
            pltpu.VMEM((r_len, r_len), F32),
        ],
        compiler_params=_cparams("parallel", "arbitrary"),
        name="gated_delta_rule",
    )(proj_a, proj_a, ab, a_row, past8, s0.astype(F32), conv_w.astype(F32), alc, dtc, alr, dtr,
      gnw.astype(F32).reshape(1, HEAD_DIM))


def _sb_kernel(q_ref, kt_ref, v_ref, o_ref, *, tq, q_pos0):
    i = pl.program_id(2)
    q = q_ref[0]
    qpos0 = q_pos0 + i * tq
    row = lax.broadcasted_iota(jnp.int32, (tq, SB_BLOCK), 0)
    col = lax.broadcasted_iota(jnp.int32, (tq, SB_BLOCK), 1)
    ri = lax.broadcasted_iota(jnp.int32, (SB_BLOCK, SB_BLOCK), 0)
    ci = lax.broadcasted_iota(jnp.int32, (SB_BLOCK, SB_BLOCK), 1)
    ustrict = jnp.where(ri > ci, 1.0, 0.0).astype(BF16)
    n_masked = max(tq // SB_BLOCK, 1)
    j_full = qpos0 // SB_BLOCK

    def tile(j, later, acc, masked):
        kt = kt_ref[0, 0, j]
        vj = v_ref[0, pl.ds(pl.multiple_of(j * SB_BLOCK, SB_BLOCK), SB_BLOCK), :]
        z = _dot(q, kt) * (HEAD_DIM ** -0.5)
        ls = -_softplus(z)
        if masked:
            vis = (j * SB_BLOCK + col) < (qpos0 + row)
            lf = jnp.where(vis, ls, 0.0)
        else:
            lf = ls
        within = _dot(lf.astype(BF16), ustrict)
        p = jnp.exp((z + ls) + within + later)
        if masked:
            p = jnp.where(vis, p, 0.0)
        acc = acc + _dot(p.astype(BF16), vj)
        return later + jnp.sum(lf, axis=-1, keepdims=True), acc

    later = jnp.zeros((tq, 1), F32)
    acc = jnp.zeros((tq, HEAD_DIM), F32)
    for m in range(n_masked):
        later, acc = tile(j_full + (n_masked - 1 - m), later, acc, True)

    def cond(carry):
        return jnp.logical_and(carry[0] >= 0, jnp.max(carry[1]) > -SB_UNDERFLOW)

    def body(carry):
        later, acc = tile(carry[0], carry[1], carry[2], False)
        return carry[0] - 1, later, acc

    _, later, acc = lax.while_loop(cond, body, (j_full - 1, later, acc))
    o_ref[0] = acc.astype(o_ref.dtype)


def _stick_breaking(q, k_bf, v_bf, q_pos0, tq):
    b, lq, _ = q.shape
    lk = k_bf.shape[1]
    nkb = lk // SB_BLOCK
    kt = k_bf.reshape(b, nkb, SB_BLOCK, SB_HEADS, HEAD_DIM).transpose(0, 3, 1, 4, 2)
    return pl.pallas_call(
        functools.partial(_sb_kernel, tq=tq, q_pos0=q_pos0),
        grid=(b, SB_HEADS, lq // tq),
        in_specs=[
            pl.BlockSpec((1, tq, HEAD_DIM), lambda s, h, i: (s, i, h)),
            pl.BlockSpec((1, 1, nkb, HEAD_DIM, SB_BLOCK), lambda s, h, i: (s, h, 0, 0, 0)),
            pl.BlockSpec((1, lk, HEAD_DIM), lambda s, h, i: (s, 0, h)),
        ],
        out_specs=pl.BlockSpec((1, tq, HEAD_DIM), lambda s, h, i: (s, i, h)),
        out_shape=jax.ShapeDtypeStruct((b, lq, SB_WIDTH), BF16),
        compiler_params=_cparams("parallel", "parallel", "arbitrary"),
        name="stick_breaking",
    )(q, kt, v_bf)


def _sb_window_kernel(q_ref, k0_ref, k1_ref, k2_ref, v0_ref, v1_ref, v2_ref, o_ref, need_ref, *, tq, base, older):
    i = pl.program_id(1)
    row = lax.broadcasted_iota(jnp.int32, (tq, SB_WINDOW * SB_BLOCK), 0)
    col = lax.broadcasted_iota(jnp.int32, (tq, SB_WINDOW * SB_BLOCK), 1)
    assert SB_WINDOW == 3 and SB_HEAD_BATCH % 2 == 0
    ri = lax.broadcasted_iota(jnp.int32, (2 * SB_BLOCK, 2 * SB_BLOCK), 0)
    ci = lax.broadcasted_iota(jnp.int32, (2 * SB_BLOCK, 2 * SB_BLOCK), 1)
    same = (ri < SB_BLOCK) == (ci < SB_BLOCK)
    upair = jnp.where(same & (ri > ci), 1.0, 0.0).astype(BF16)
    k_refs = (k0_ref, k1_ref, k2_ref)
    v_refs = (v0_ref, v1_ref, v2_ref)
    n_win = SB_WINDOW * SB_BLOCK
    limit = row
    for w in range(1, SB_WINDOW):
        limit = jnp.where(col < w * SB_BLOCK, limit, jnp.where(base + i - w >= 0, n_win, 0))
    vis = col < limit
    worst = jnp.full((tq, 1), -jnp.inf, F32)
    for h0 in range(0, SB_HEADS, SB_HEAD_BATCH):
        batch = range(h0, h0 + SB_HEAD_BATCH)
        cols = {h: slice(h * HEAD_DIM, (h + 1) * HEAD_DIM) for h in batch}
        z = {h: _dot_nt(q_ref[0, :, cols[h]], jnp.concatenate([r[0, :, cols[h]] for r in k_refs], axis=0))
             * (HEAD_DIM ** -0.5) for h in batch}
        ls = {h: -_softplus(z[h]) for h in batch}
        lf = {h: jnp.where(vis, ls[h], 0.0) for h in batch}
        lf_b = {h: lf[h].astype(BF16) for h in batch}
        within = {h: [] for h in batch}
        for h in batch:
            both = _dot(lf_b[h][:, 0:2 * SB_BLOCK], upair)
            within[h] += [both[:, 0:SB_BLOCK], both[:, SB_BLOCK:2 * SB_BLOCK]]
        for h in range(h0, h0 + SB_HEAD_BATCH, 2):
            both = _dot(jnp.concatenate([lf_b[h][:, 2 * SB_BLOCK:], lf_b[h + 1][:, 2 * SB_BLOCK:]], axis=1), upair)
            within[h].append(both[:, 0:SB_BLOCK])
            within[h + 1].append(both[:, SB_BLOCK:2 * SB_BLOCK])
        p = {}
        for h in batch:
            later = jnp.zeros((tq, 1), F32)
            shifted = []
            for w in range(SB_WINDOW):
                shifted.append(within[h][w] + later)
                later = later + jnp.sum(lf[h][:, w * SB_BLOCK:(w + 1) * SB_BLOCK], axis=-1, keepdims=True)
            worst = jnp.maximum(worst, later)
            p[h] = jnp.where(vis, jnp.exp((z[h] + ls[h]) + jnp.concatenate(shifted, axis=1)), 0.0).astype(BF16)
        for h in batch:
            acc = _dot(p[h], jnp.concatenate([r[0, :, cols[h]] for r in v_refs], axis=0))
            o_ref[0, :, cols[h]] = acc.astype(o_ref.dtype)
    has_older = base + i + older >= SB_WINDOW
    need_ref[...] = jnp.broadcast_to(jnp.where(has_older, jnp.max(worst), -jnp.inf), need_ref.shape)


def _stick_breaking_window(q, k_bf, v_bf, tq, base, older):
    b, lq, _ = q.shape
    nq = lq // tq
    qspec = pl.BlockSpec((1, tq, SB_WIDTH), lambda s, i: (s, i, 0))
    kspecs = [pl.BlockSpec((1, SB_BLOCK, SB_WIDTH), functools.partial(
        lambda s, i, w: (s, jnp.maximum(base + i - w, 0), 0), w=w)) for w in range(SB_WINDOW)]
    return pl.pallas_call(
        functools.partial(_sb_window_kernel, tq=tq, base=base, older=older),
        grid=(b, nq),
        in_specs=[qspec] + kspecs + kspecs,
        out_specs=[qspec, pl.BlockSpec((1, 1, SUBLANES, HEAD_DIM), lambda s, i: (s, i, 0, 0))],
        out_shape=[jax.ShapeDtypeStruct((b, lq, SB_WIDTH), BF16),
                   jax.ShapeDtypeStruct((b, nq, SUBLANES, HEAD_DIM), F32)],
        compiler_params=_cparams("parallel", "arbitrary"),
        name="stick_breaking_window",
    )(q, k_bf, k_bf, k_bf, v_bf, v_bf, v_bf)


def _layer(x, conv_past, s0, past_k, past_v, chunk, wts):
    b, l, d = x.shape
    m = b * l
    x2 = x.reshape(m, d)
    xn = _rmsnorm(x2, wts["norm1_w"], BF16)
    w_in = wts["w_in"]
    c_q = OFF_A
    c_g = c_q + 3 * SB_WIDTH
    c_ab = c_g + 2 * d
    (ab,) = _matmul(xn, wts["w_ab"], (F32,))
    if m <= 256:
        (proj,) = _matmul(xn, w_in, (F32,), 0, c_ab)
        proj_a, q_sb, gates = proj[:, :OFF_A], proj[:, c_q:c_q + SB_WIDTH].astype(BF16), proj[:, c_g:]
        k_sb, v_sb = proj[:, c_q + SB_WIDTH:c_q + 2 * SB_WIDTH], proj[:, c_q + 2 * SB_WIDTH:c_g]
        k_bf, v_bf = k_sb.astype(BF16), v_sb.astype(BF16)
    else:
        (proj_a,) = _matmul(xn, w_in, (F32,), 0, OFF_A)
        (q_sb,) = _matmul(xn, w_in, (BF16,), c_q, SB_WIDTH)
        k_sb, k_bf = _matmul(xn, w_in, (F32, BF16), c_q + SB_WIDTH, SB_WIDTH)
        v_sb, v_bf = _matmul(xn, w_in, (F32, BF16), c_q + 2 * SB_WIDTH, SB_WIDTH)
        (gates,) = _matmul(xn, w_in, (F32,), c_g, 2 * d)

    o_a, s_new = _gdn(proj_a.reshape(b, l, -1), ab.reshape(b, l, -1), conv_past, s0, wts["conv_w"], wts["A_log"],
                      wts["dt_bias"], wts["gdn_norm_w"], chunk)
    conv_state = proj_a.reshape(b, l, -1)[:, l - (CONV_WIDTH - 1):, :CONV_CH]

    k_bf = k_bf.reshape(b, l, SB_WIDTH)
    v_bf = v_bf.reshape(b, l, SB_WIDTH)
    q3 = q_sb.reshape(b, l, SB_WIDTH)
    if past_k is None:
        o_win, need = _stick_breaking_window(q3, k_bf, v_bf, min(l, SB_BLOCK), 0, 0)

        def full_sweep():
            return _stick_breaking(q3, k_bf, v_bf, 0, min(l, 256))
    else:
        p = past_k.shape[1]
        n_cached = (SB_WINDOW - 1) * SB_BLOCK
        assert p % SB_BLOCK == 0 and p >= n_cached and l <= SB_BLOCK

        def with_cache(new, past, keep):
            return jnp.concatenate([past[:, p - keep:].reshape(b, keep, SB_WIDTH).astype(BF16), new,
                                    jnp.zeros((b, SB_BLOCK - l, SB_WIDTH), BF16)], axis=1)

        o_win, need = _stick_breaking_window(q3, with_cache(k_bf, past_k, n_cached), with_cache(v_bf, past_v, n_cached),
                                             l, SB_WINDOW - 1, (p - n_cached) // SB_BLOCK)

        def full_sweep():
            return _stick_breaking(q3, with_cache(k_bf, past_k, p), with_cache(v_bf, past_v, p), p, l)
    o_b = lax.cond(jnp.max(need) > -SB_UNDERFLOW, full_sweep, lambda: o_win)

    merged = _merge(o_a.reshape(m, GDN_V_WIDTH), o_b.reshape(m, SB_WIDTH), wts["w_gdn_o"], wts["w_sb_o"], gates)
    h, h_gain, sumsq = _matmul_residual(merged, wts["w_out"], x2, wts["norm2_w"])
    hid = _matmul_relu2(h_gain, wts["w_up"], sumsq)
    h = _matmul_residual_ktiled(hid, wts["w_down"], h)
    return (h, conv_state, s_new, k_sb.reshape(b, l, SB_HEADS, HEAD_DIM), v_sb.reshape(b, l, SB_HEADS, HEAD_DIM))


def kernel(x_prompt, x_sample, cache_sb_k, cache_sb_v, state_gdn_S, state_gdn_conv, norm1_w, w_in, conv_w, A_log,
           dt_bias, gdn_norm_w, w_gdn_o, w_sb_o, w_out, norm2_w, w_up, w_down, final_norm_w):
    depth = w_in.shape[0]
    assert depth == 1
    w_in_t = jnp.swapaxes(w_in[0], 0, 1)
    wts = {
        "norm1_w": norm1_w[0], "norm2_w": norm2_w[0], "conv_w": conv_w[0], "A_log": A_log[0], "dt_bias": dt_bias[0],
        "gdn_norm_w": gdn_norm_w[0],
        "w_in": _repack_w_in(w_in_t), "w_ab": _ab_cols(w_in_t),
        "w_gdn_o": w_gdn_o[0].astype(BF16), "w_sb_o": w_sb_o[0].astype(BF16), "w_out": w_out[0].astype(BF16),
        "w_up": w_up[0].astype(BF16), "w_down": w_down[0].astype(BF16),
    }
    bp, lp, d = x_prompt.shape
    bs, ls, _ = x_sample.shape
    conv0 = jnp.zeros((bp, CONV_WIDTH - 1, CONV_CH), F32)
    s_zero = jnp.zeros((bp, GDN_V_HEADS, HEAD_DIM, HEAD_DIM), F32)
    hp, c_p, s_p, k_p, v_p = _layer(x_prompt, conv0, s_zero, None, None, min(PROMPT_CHUNK, lp), wts)
    hs, c_s, s_s, k_s, v_s = _layer(x_sample, state_gdn_conv[0], state_gdn_S[0], cache_sb_k[0], cache_sb_v[0], ls, wts)
    y_prompt = _rmsnorm(hp, final_norm_w, F32).reshape(bp, lp, d)
    y_sample = _rmsnorm(hs, final_norm_w, F32).reshape(bs, ls, d)
    return (y_prompt, y_sample, k_p[None], v_p[None], s_p[None], c_p[None],
            k_s[None], v_s[None], s_s[None].astype(state_gdn_S.dtype), c_s[None])
```

```python
import functools
import math

import jax
import jax.numpy as jnp
from jax import lax
from jax.experimental import pallas as pl
from jax.experimental.pallas import tpu as pltpu

F32 = jnp.float32
BF16 = jnp.bfloat16

EPS = 1e-6
HEAD_DIM = 128
GDN_QK_HEADS = 8
GDN_V_HEADS = 16
GDN_QK_WIDTH = GDN_QK_HEADS * HEAD_DIM
GDN_V_WIDTH = GDN_V_HEADS * HEAD_DIM
CONV_WIDTH = 4
CONV_CH = 2 * GDN_QK_WIDTH + GDN_V_WIDTH
SB_HEADS = 16
SB_WIDTH = SB_HEADS * HEAD_DIM
SB_BLOCK = 128
SB_UNDERFLOW = 106.0
SB_WINDOW = 3
SB_HEAD_BATCH = 4
PROMPT_CHUNK = 64
GDN_CHUNKS_PER_STEP = 2

V7X_MXU_DIM = 256
V7X_VMEM_LIMIT_BYTES = 56 * 1024 * 1024
SUBLANES = 8

OFF_Z = CONV_CH
OFF_A = OFF_Z + GDN_V_WIDTH
OFF_B = OFF_A + GDN_V_HEADS
OFF_SB = OFF_B + GDN_V_HEADS
OFF_GATE = OFF_SB + 3 * SB_WIDTH


def _cparams(*sem):
    return pltpu.CompilerParams(dimension_semantics=sem, vmem_limit_bytes=V7X_VMEM_LIMIT_BYTES)


def _dot(a, b):
    return jnp.dot(a, b, preferred_element_type=F32)


def _dot_nt(a, b):
    return lax.dot_general(a, b, (((1,), (1,)), ((), ())), preferred_element_type=F32)


def _dot_tn(a, b):
    return lax.dot_general(a, b, (((0,), (0,)), ((), ())), preferred_element_type=F32)


def _sigmoid(x):
    return 1.0 / (1.0 + jnp.exp(-x))


def _softplus(x):
    return jnp.maximum(x, 0.0) + jnp.log(1.0 + jnp.exp(-jnp.abs(x)))


def _split3(x):
    x1 = x.astype(BF16)
    r1 = x - x1.astype(F32)
    x2 = r1.astype(BF16)
    x3 = (r1 - x2.astype(F32)).astype(BF16)
    return x1, x2, x3


def _rmsnorm_kernel(x_ref, w_ref, o_ref):
    x = x_ref[...]
    var = jnp.mean(x * x, axis=-1, keepdims=True)
    o_ref[...] = (x * lax.rsqrt(var + EPS) * w_ref[...]).astype(o_ref.dtype)


def _rmsnorm(x, w, out_dtype):
    m, d = x.shape
    tm = min(m, 256)
    return pl.pallas_call(
        _rmsnorm_kernel,
        grid=(m // tm,),
        in_specs=[pl.BlockSpec((tm, d), lambda i: (i, 0)), pl.BlockSpec((1, d), lambda i: (0, 0))],
        out_specs=pl.BlockSpec((tm, d), lambda i: (i, 0)),
        out_shape=jax.ShapeDtypeStruct((m, d), out_dtype),
        compiler_params=_cparams("parallel"),
        name="rmsnorm",
    )(x, w.reshape(1, d).astype(F32))


def _repack_kernel(a_ref, b_ref, o_ref, *, first_shifted, shift):
    j = pl.program_id(0)

    @pl.when(j < first_shifted)
    def _():
        o_ref[...] = a_ref[...].T.astype(o_ref.dtype)

    @pl.when(j >= first_shifted)
    def _():
        src = jnp.concatenate([a_ref[shift:, :], b_ref[...]], axis=0)
        o_ref[...] = src.T.astype(o_ref.dtype)


def _repack_w_in(wt):
    n_src, k = wt.shape
    tn, tk = 1024, 512
    shift = OFF_SB - OFF_A
    n_out = n_src - shift
    assert OFF_A % tn == 0 and n_out % tn == 0 and k % tk == 0 and tn % shift == 0 and shift % SUBLANES == 0
    return pl.pallas_call(
        functools.partial(_repack_kernel, first_shifted=OFF_A // tn, shift=shift),
        grid=(n_out // tn, k // tk),
        in_specs=[pl.BlockSpec((tn, tk), lambda j, i: (j, i)),
                  pl.BlockSpec((shift, tk), lambda j, i: ((j + 1) * (tn // shift), i))],
        out_specs=pl.BlockSpec((tk, tn), lambda j, i: (i, j)),
        out_shape=jax.ShapeDtypeStruct((k, n_out), BF16),
        compiler_params=_cparams("parallel", "arbitrary"),
        name="repack_w_in",
    )(wt, wt)


def _ab_cols_kernel(a_ref, o_ref, *, n_valid):
    t = a_ref[...].T
    lane = lax.broadcasted_iota(jnp.int32, t.shape, 1)
    o_ref[...] = jnp.where(lane < n_valid, t, 0.0).astype(o_ref.dtype)


def _ab_cols(wt):
    k = wt.shape[1]
    tk = 512
    assert OFF_A % HEAD_DIM == 0 and k % tk == 0
    return pl.pallas_call(
        functools.partial(_ab_cols_kernel, n_valid=OFF_SB - OFF_A),
        grid=(k // tk,),
        in_specs=[pl.BlockSpec((HEAD_DIM, tk), lambda i: (OFF_A // HEAD_DIM, i))],
        out_specs=pl.BlockSpec((tk, HEAD_DIM), lambda i: (i, 0)),
        out_shape=jax.ShapeDtypeStruct((k, HEAD_DIM), BF16),
        compiler_params=_cparams("arbitrary"),
        name="ab_cols",
    )(wt)


def _mm_tiles(m, n, tn_max=1024):
    tm = min(m, 1024)
    tn = min(n, tn_max)
    assert m % tm == 0 and n % tn == 0, (m, n)
    return tm, tn


def _mm_kernel(a_ref, w_ref, *o_refs):
    acc = _dot(a_ref[...], w_ref[...])
    for o_ref in o_refs:
        o_ref[...] = acc.astype(o_ref.dtype)


def _matmul(a, w, out_dtypes, col0=0, n=None):
    m, k = a.shape
    n = w.shape[1] if n is None else n
    tm, tn = _mm_tiles(m, n)
    assert col0 % tn == 0
    jb = col0 // tn
    outs = pl.pallas_call(
        _mm_kernel,
        grid=(m // tm, n // tn),
        in_specs=[pl.BlockSpec((tm, k), lambda i, j: (i, 0)), pl.BlockSpec((k, tn), lambda i, j: (0, j + jb))],
        out_specs=[pl.BlockSpec((tm, tn), lambda i, j: (i, j)) for _ in out_dtypes],
        out_shape=[jax.ShapeDtypeStruct((m, n), dt) for dt in out_dtypes],
        compiler_params=_cparams("parallel", "arbitrary"),
        name="matmul",
    )(a, w)
    return outs


def _mm_relu2_kernel(a_ref, w_ref, ss_ref, o_ref, *, n_norm):
    tn = o_ref.shape[1]
    sumsq = ss_ref[:, 0:HEAD_DIM]
    for c in range(1, ss_ref.shape[1] // HEAD_DIM):
        sumsq = sumsq + ss_ref[:, c * HEAD_DIM:(c + 1) * HEAD_DIM]
    inv_rms = lax.rsqrt(sumsq * (1.0 / n_norm) + EPS)
    acc = _dot(a_ref[...], w_ref[...]) * jnp.concatenate([inv_rms] * (tn // HEAD_DIM), axis=1)
    o_ref[...] = jnp.square(jnp.maximum(acc, 0.0)).astype(o_ref.dtype)


def _matmul_relu2(a, w, sumsq):
    m, k = a.shape
    n = w.shape[1]
    tm, tn = _mm_tiles(m, n)
    return pl.pallas_call(
        functools.partial(_mm_relu2_kernel, n_norm=k),
        grid=(m // tm, n // tn),
        in_specs=[pl.BlockSpec((tm, k), lambda i, j: (i, 0)), pl.BlockSpec((k, tn), lambda i, j: (0, j)),
                  pl.BlockSpec((tm, sumsq.shape[1]), lambda i, j: (i, 0))],
        out_specs=pl.BlockSpec((tm, tn), lambda i, j: (i, j)),
        out_shape=jax.ShapeDtypeStruct((m, n), BF16),
        compiler_params=_cparams("parallel", "arbitrary"),
        name="matmul_relu2",
    )(a, w, sumsq)


def _mm_res_kernel(a_ref, w_ref, r_ref, g_ref, o_ref, ob_ref, ss_ref):
    h = r_ref[...] + _dot(a_ref[...], w_ref[...])
    o_ref[...] = h
    ob_ref[...] = (h * g_ref[...]).astype(ob_ref.dtype)
    ss_ref[...] = jnp.broadcast_to(jnp.sum(h * h, axis=-1, keepdims=True), ss_ref.shape)


def _matmul_residual(a, w, res, gain):
    m, k = a.shape
    n = w.shape[1]
    tm, tn = _mm_tiles(m, n, 512)
    tile = pl.BlockSpec((tm, tn), lambda i, j: (i, j))
    return pl.pallas_call(
        _mm_res_kernel,
        grid=(m // tm, n // tn),
        in_specs=[pl.BlockSpec((tm, k), lambda i, j: (i, 0)), pl.BlockSpec((k, tn), lambda i, j: (0, j)), tile,
                  pl.BlockSpec((1, tn), lambda i, j: (0, j))],
        out_specs=[tile, tile, pl.BlockSpec((tm, HEAD_DIM), lambda i, j: (i, j))],
        out_shape=[jax.ShapeDtypeStruct((m, n), F32), jax.ShapeDtypeStruct((m, n), BF16),
                   jax.ShapeDtypeStruct((m, (n // tn) * HEAD_DIM), F32)],
        compiler_params=_cparams("parallel", "arbitrary"),
        name="matmul_residual",
    )(a, w, res, gain.reshape(1, n).astype(F32))


def _mm_kacc_kernel(a_ref, w_ref, r_ref, o_ref):
    @pl.when(pl.program_id(2) == 0)
    def _():
        o_ref[...] = r_ref[...]

    o_ref[...] += _dot(a_ref[...], w_ref[...])


def _matmul_residual_ktiled(a, w, res):
    m, k = a.shape
    n = w.shape[1]
    tm = min(m, 1024)
    tn = min(n, 1024)
    tk = min(k, 2048)
    return pl.pallas_call(
        _mm_kacc_kernel,
        grid=(m // tm, n // tn, k // tk),
        in_specs=[pl.BlockSpec((tm, tk), lambda i, j, l: (i, l)), pl.BlockSpec((tk, tn), lambda i, j, l: (l, j)),
                  pl.BlockSpec((tm, tn), lambda i, j, l: (i, j))],
        out_specs=pl.BlockSpec((tm, tn), lambda i, j, l: (i, j)),
        out_shape=jax.ShapeDtypeStruct((m, n), F32),
        compiler_params=_cparams("parallel", "parallel", "arbitrary"),
        name="matmul_residual_ktiled",
    )(a, w, res)


def _merge_kernel(oa_ref, ob_ref, wa_ref, wb_ref, ga_ref, gb_ref, o_ref):
    ya = _dot(oa_ref[...], wa_ref[...])
    yb = _dot(ob_ref[...], wb_ref[...])
    o_ref[...] = (_sigmoid(ga_ref[...]) * ya + _sigmoid(gb_ref[...]) * yb).astype(o_ref.dtype)


def _merge(oa, ob, wa, wb, gates):
    m, k = oa.shape
    n = wa.shape[1]
    tm, tn = _mm_tiles(m, n, 512)
    nb = n // tn
    return pl.pallas_call(
        _merge_kernel,
        grid=(m // tm, nb),
        in_specs=[pl.BlockSpec((tm, k), lambda i, j: (i, 0)), pl.BlockSpec((tm, k), lambda i, j: (i, 0)),
                  pl.BlockSpec((k, tn), lambda i, j: (0, j)), pl.BlockSpec((k, tn), lambda i, j: (0, j)),
                  pl.BlockSpec((tm, tn), lambda i, j: (i, j)), pl.BlockSpec((tm, tn), lambda i, j: (i, j + nb))],
        out_specs=pl.BlockSpec((tm, tn), lambda i, j: (i, j)),
        out_shape=jax.ShapeDtypeStruct((m, n), BF16),
        compiler_params=_cparams("parallel", "arbitrary"),
        name="merge",
    )(oa, ob, wa, wb, gates, gates)


def _gdn_kernel(qkv_ref, z_ref, abc_ref, ar_ref, *rest, chunk, group, n_sub):
    o_ref, act_ref = rest[8], rest[11]
    for sc in range(n_sub):
        rows = slice(sc * chunk, (sc + 1) * chunk)
        views = (qkv_ref.at[:, rows], z_ref.at[:, rows], abc_ref.at[:, rows], ar_ref.at[:, sc:sc + 1])
        _gdn_chunk(sc, *views, *rest[:8], o_ref.at[:, rows], *rest[9:11], act_ref.at[rows], *rest[12:],
                   chunk=chunk, group=group)


def _gdn_chunk(sub, qkv_ref, z_ref, abc_ref, ar_ref, past_ref, s0_ref, cw_ref, alc_ref, dtc_ref, alr_ref, dtr_ref,
               gnw_ref, o_ref, s_ref, xp_ref, act_ref, ltri_ref, ubd_ref, cmask_ref, smask_ref, *, chunk, group):
    c_len, g_sz = chunk, group
    r_len = c_len * g_sz
    n_groups = GDN_V_HEADS // g_sz
    rep = GDN_V_HEADS // GDN_QK_HEADS
    first_of_stream = jnp.logical_and(pl.program_id(1) == 0, sub == 0)

    @pl.when(first_of_stream)
    def _init():
        s_ref[...] = s0_ref[...]
        xp_ref[0:SUBLANES, :] = past_ref[0]
        ri = lax.broadcasted_iota(jnp.int32, (r_len, r_len), 0)
        ci = lax.broadcasted_iota(jnp.int32, (r_len, r_len), 1)
        shift = int(math.log2(c_len))
        same = (ri >> shift) == (ci >> shift)
        cmask_ref[...] = jnp.where(same & (ri >= ci), 1.0, 0.0)
        smask_ref[...] = jnp.where(same & (ri > ci), 1.0, 0.0)
        ubd_ref[...] = jnp.where(same & (ri <= ci), 1.0, 0.0).astype(BF16)
        rc = lax.broadcasted_iota(jnp.int32, (c_len, c_len), 0)
        cc = lax.broadcasted_iota(jnp.int32, (c_len, c_len), 1)
        ltri_ref[...] = jnp.where(rc >= cc, 1.0, 0.0).astype(BF16)

    @pl.when(jnp.logical_not(first_of_stream))
    def _carry():
        xp_ref[0:SUBLANES, :] = xp_ref[c_len:c_len + SUBLANES, :]

    xp_ref[SUBLANES:SUBLANES + c_len, :] = qkv_ref[0]

    for s in range(CONV_CH // HEAD_DIM):
        cols = slice(s * HEAD_DIM, (s + 1) * HEAD_DIM)
        y = cw_ref[0:1, cols] * xp_ref[SUBLANES - 3:SUBLANES - 3 + c_len, cols]
        for i in range(1, CONV_WIDTH):
            y = y + cw_ref[i:i + 1, cols] * xp_ref[SUBLANES - 3 + i:SUBLANES - 3 + i + c_len, cols]
        y = y * _sigmoid(y)
        if s < 2 * GDN_QK_HEADS:
            y = y * lax.rsqrt(jnp.sum(y * y, axis=-1, keepdims=True) + EPS)
            if s < GDN_QK_HEADS:
                y = y * (HEAD_DIM ** -0.5)
        act_ref[:, cols] = y

    ab = abc_ref[0]
    g_col = -jnp.exp(alc_ref[...]) * _softplus(ab + dtc_ref[...])
    beta_col = _sigmoid(ab)
    ltri = ltri_ref[...]
    gc_col = sum(_dot(ltri, t) for t in _split3(g_col))
    g_row = -jnp.exp(alr_ref[...]) * _softplus(ar_ref[0, 0] + dtr_ref[...])
    ubd = ubd_ref[...]
    gc_row = sum(_dot(t, ubd) for t in _split3(g_row))

    cmask = cmask_ref[...]
    smask = smask_ref[...]
    gnw = gnw_ref[...]

    groups = range(n_groups)
    heads_of = [[grp * g_sz + hb for hb in range(g_sz)] for grp in groups]

    def stacked(off, div):
        return [jnp.concatenate([act_ref[:, off + (h // div) * HEAD_DIM:off + (h // div + 1) * HEAD_DIM]
                                 for h in heads_of[grp]], axis=0) for grp in groups]

    kst = stacked(GDN_QK_WIDTH, rep)
    qst = stacked(0, rep)
    vst = stacked(2 * GDN_QK_WIDTH, 1)
    gcb = [jnp.concatenate([jnp.broadcast_to(gc_col[:, h:h + 1], (c_len, HEAD_DIM)) for h in heads_of[grp]], axis=0)
           for grp in groups]
    bcb = [jnp.concatenate([jnp.broadcast_to(beta_col[:, GDN_V_HEADS + h:GDN_V_HEADS + h + 1], (c_len, HEAD_DIM))
                            for h in heads_of[grp]], axis=0) for grp in groups]
    kst_b = [t.astype(BF16) for t in kst]
    gram = [_dot_nt(kst_b[grp], kst_b[grp]) for grp in groups]
    qk = [_dot_nt(qst[grp].astype(BF16), kst_b[grp]) for grp in groups]
    n_rep = r_len // HEAD_DIM
    aqk, tp, xpow = [], [], []
    for grp in groups:
        gcb2 = jnp.concatenate([gcb[grp]] * n_rep, axis=1)
        nbeta = -bcb[grp]
        nbeta2 = jnp.concatenate([nbeta] * n_rep, axis=1) * smask
        decay = jnp.exp(jnp.minimum(gcb2 - gc_row[grp:grp + 1, :], 0.0)) * cmask
        aqk.append((qk[grp] * decay).astype(BF16))
        nm = (gram[grp] * decay) * nbeta2
        tp.append(nm)
        xpow.append(nm)
    n_steps = int(math.log2(c_len))
    for step in range(n_steps):
        for grp in groups:
            xb = xpow[grp].astype(BF16)
            tp[grp] = tp[grp] + _dot(xb, tp[grp].astype(BF16))
            if step + 1 < n_steps:
                xpow[grp] = _dot(xb, xb)
    eg = [jnp.exp(gcb[grp]) for grp in groups]
    sol = []
    for grp in groups:
        rhs = jnp.concatenate([vst[grp] * bcb[grp], kst[grp] * (bcb[grp] * eg[grp])], axis=1)
        sol.append(rhs + _dot(tp[grp].astype(BF16), rhs.astype(BF16)))
    qss = {}
    ust = []
    for grp in groups:
        qg = qst[grp] * eg[grp]
        us = []
        for hb, h in enumerate(heads_of[grp]):
            rows = slice(hb * c_len, (hb + 1) * c_len)
            s_old = s_ref[0, h]
            wq = jnp.concatenate([sol[grp][rows, HEAD_DIM:2 * HEAD_DIM], qg[rows]], axis=0).astype(BF16)
            r1 = _dot(wq, s_old.astype(BF16))
            u = sol[grp][rows, 0:HEAD_DIM] - r1[0:c_len]
            qss[h] = r1[c_len:2 * c_len]
            gl = gcb[grp][(hb + 1) * c_len - 1:(hb + 1) * c_len, :]
            kd = kst[grp][rows] * jnp.exp(gl - gcb[grp][rows])
            s_ref[0, h] = jnp.exp(gl) * s_old + _dot_tn(kd.astype(BF16), u.astype(BF16))
            us.append(u)
        ust.append(jnp.concatenate(us, axis=0).astype(BF16))
    o_intra = [_dot(aqk[grp], ust[grp]) for grp in groups]
    for grp in groups:
        for hb, h in enumerate(heads_of[grp]):
            rows = slice(hb * c_len, (hb + 1) * c_len)
            o = qss[h] + o_intra[grp][rows]
            o = o * lax.rsqrt(jnp.mean(o * o, axis=-1, keepdims=True) + EPS) * gnw
            zz = z_ref[0, :, h * HEAD_DIM:(h + 1) * HEAD_DIM]
            o_ref[0, :, h * HEAD_DIM:(h + 1) * HEAD_DIM] = (o * (zz * _sigmoid(zz))).astype(o_ref.dtype)


def _gdn(proj_a, ab, conv_past, s0, conv_w, a_log, dt_bias, gnw, chunk):
    b, l, _ = proj_a.shape
    n_chunks = l // chunk
    group = V7X_MXU_DIM // chunk
    n_groups = GDN_V_HEADS // group
    r_len = chunk * group
    a_row = ab[:, :, :GDN_V_HEADS].reshape(b, n_chunks, chunk, n_groups, group)
    a_row = a_row.transpose(0, 1, 3, 4, 2).reshape(b, n_chunks, n_groups, r_len)
    past8 = jnp.pad(conv_past.astype(F32), ((0, 0), (SUBLANES - (CONV_WIDTH - 1), 0), (0, 0)))
    pad16 = HEAD_DIM - GDN_V_HEADS
    alc = jnp.pad(a_log.astype(F32), (0, pad16)).reshape(1, HEAD_DIM)
    dtc = jnp.pad(dt_bias.astype(F32), (0, pad16)).reshape(1, HEAD_DIM)
    alr = jnp.repeat(a_log.astype(F32), chunk).reshape(n_groups, r_len)
    dtr = jnp.repeat(dt_bias.astype(F32), chunk).reshape(n_groups, r_len)
    const = lambda *shape: pl.BlockSpec(shape, lambda i, c: (0,) * len(shape))
    n_sub = GDN_CHUNKS_PER_STEP if n_chunks % GDN_CHUNKS_PER_STEP == 0 else 1
    rows = n_sub * chunk
    return pl.pallas_call(
        functools.partial(_gdn_kernel, chunk=chunk, group=group, n_sub=n_sub),
        grid=(b, n_chunks // n_sub),
        in_specs=[
            pl.BlockSpec((1, rows, CONV_CH), lambda i, c: (i, c, 0)),
            pl.BlockSpec((1, rows, GDN_V_WIDTH), lambda i, c: (i, c, CONV_CH // GDN_V_WIDTH)),
            pl.BlockSpec((1, rows, HEAD_DIM), lambda i, c: (i, c, 0)),
            pl.BlockSpec((1, n_sub, n_groups, r_len), lambda i, c: (i, c, 0, 0)),
            pl.BlockSpec((1, SUBLANES, CONV_CH), lambda i, c: (i, 0, 0)),
            pl.BlockSpec((1, GDN_V_HEADS, HEAD_DIM, HEAD_DIM), lambda i, c: (i, 0, 0, 0)),
            const(CONV_WIDTH, CONV_CH), const(1, HEAD_DIM), const(1, HEAD_DIM),
            const(n_groups, r_len), const(n_groups, r_len), const(1, HEAD_DIM),
        ],
        out_specs=[
            pl.BlockSpec((1, rows, GDN_V_WIDTH), lambda i, c: (i, c, 0)),
            pl.BlockSpec((1, GDN_V_HEADS, HEAD_DIM, HEAD_DIM), lambda i, c: (i, 0, 0, 0)),
        ],
        out_shape=[jax.ShapeDtypeStruct((b, l, GDN_V_WIDTH), BF16),
                   jax.ShapeDtypeStruct((b, GDN_V_HEADS, HEAD_DIM, HEAD_DIM), F32)],
        scratch_shapes=[
            pltpu.VMEM((chunk + SUBLANES, CONV_CH), F32),
            pltpu.VMEM((rows, CONV_CH), F32),
            pltpu.VMEM((chunk, chunk), BF16),
            pltpu.VMEM((r_len, r_len), BF16),
            pltpu.VMEM((r_len, r_len), F32),
            pltpu.VMEM((r_len, r_len), F32),
        ],
        compiler_params=_cparams("parallel", "arbitrary"),
        name="gated_delta_rule",
    )(proj_a, proj_a, ab, a_row, past8, s0.astype(F32), conv_w.astype(F32), alc, dtc, alr, dtr,
      gnw.astype(F32).reshape(1, HEAD_DIM))


def _sb_kernel(q_ref, kt_ref, v_ref, o_ref, *, tq, q_pos0):
    i = pl.program_id(2)
    q = q_ref[0]
    qpos0 = q_pos0 + i * tq
    row = lax.broadcasted_iota(jnp.int32, (tq, SB_BLOCK), 0)
    col = lax.broadcasted_iota(jnp.int32, (tq, SB_BLOCK), 1)
    ri = lax.broadcasted_iota(jnp.int32, (SB_BLOCK, SB_BLOCK), 0)
    ci = lax.broadcasted_iota(jnp.int32, (SB_BLOCK, SB_BLOCK), 1)
    ustrict = jnp.where(ri > ci, 1.0, 0.0).astype(BF16)
    n_masked = max(tq // SB_BLOCK, 1)
    j_full = qpos0 // SB_BLOCK

    def tile(j, later, acc, masked):
        kt = kt_ref[0, 0, j]
        vj = v_ref[0, pl.ds(pl.multiple_of(j * SB_BLOCK, SB_BLOCK), SB_BLOCK), :]
        z = _dot(q, kt) * (HEAD_DIM ** -0.5)
        ls = -_softplus(z)
        if masked:
            vis = (j * SB_BLOCK + col) < (qpos0 + row)
            lf = jnp.where(vis, ls, 0.0)
        else:
            lf = ls
        within = _dot(lf.astype(BF16), ustrict)
        p = jnp.exp((z + ls) + within + later)
        if masked:
            p = jnp.where(vis, p, 0.0)
        acc = acc + _dot(p.astype(BF16), vj)
        return later + jnp.sum(lf, axis=-1, keepdims=True), acc

    later = jnp.zeros((tq, 1), F32)
    acc = jnp.zeros((tq, HEAD_DIM), F32)
    for m in range(n_masked):
        later, acc = tile(j_full + (n_masked - 1 - m), later, acc, True)

    def cond(carry):
        return jnp.logical_and(carry[0] >= 0, jnp.max(carry[1]) > -SB_UNDERFLOW)

    def body(carry):
        later, acc = tile(carry[0], carry[1], carry[2], False)
        return carry[0] - 1, later, acc

    _, later, acc = lax.while_loop(cond, body, (j_full - 1, later, acc))
    o_ref[0] = acc.astype(o_ref.dtype)


def _stick_breaking(q, k_bf, v_bf, q_pos0, tq):
    b, lq, _ = q.shape
    lk = k_bf.shape[1]
    nkb = lk // SB_BLOCK
    kt = k_bf.reshape(b, nkb, SB_BLOCK, SB_HEADS, HEAD_DIM).transpose(0, 3, 1, 4, 2)
    return pl.pallas_call(
        functools.partial(_sb_kernel, tq=tq, q_pos0=q_pos0),
        grid=(b, SB_HEADS, lq // tq),
        in_specs=[
            pl.BlockSpec((1, tq, HEAD_DIM), lambda s, h, i: (s, i, h)),
            pl.BlockSpec((1, 1, nkb, HEAD_DIM, SB_BLOCK), lambda s, h, i: (s, h, 0, 0, 0)),
            pl.BlockSpec((1, lk, HEAD_DIM), lambda s, h, i: (s, 0, h)),
        ],
        out_specs=pl.BlockSpec((1, tq, HEAD_DIM), lambda s, h, i: (s, i, h)),
        out_shape=jax.ShapeDtypeStruct((b, lq, SB_WIDTH), BF16),
        compiler_params=_cparams("parallel", "parallel", "arbitrary"),
        name="stick_breaking",
    )(q, kt, v_bf)


def _sb_window_kernel(q_ref, k0_ref, k1_ref, k2_ref, v0_ref, v1_ref, v2_ref, o_ref, need_ref, *, tq, base, older):
    i = pl.program_id(1)
    row = lax.broadcasted_iota(jnp.int32, (tq, SB_WINDOW * SB_BLOCK), 0)
    col = lax.broadcasted_iota(jnp.int32, (tq, SB_WINDOW * SB_BLOCK), 1)
    assert SB_WINDOW == 3 and SB_HEAD_BATCH % 2 == 0
    ri = lax.broadcasted_iota(jnp.int32, (2 * SB_BLOCK, 2 * SB_BLOCK), 0)
    ci = lax.broadcasted_iota(jnp.int32, (2 * SB_BLOCK, 2 * SB_BLOCK), 1)
    same = (ri < SB_BLOCK) == (ci < SB_BLOCK)
    upair = jnp.where(same & (ri > ci), 1.0, 0.0).astype(BF16)
    k_refs = (k0_ref, k1_ref, k2_ref)
    v_refs = (v0_ref, v1_ref, v2_ref)
    n_win = SB_WINDOW * SB_BLOCK
    limit = row
    for w in range(1, SB_WINDOW):
        limit = jnp.where(col < w * SB_BLOCK, limit, jnp.where(base + i - w >= 0, n_win, 0))
    vis = col < limit
    worst = jnp.full((tq, 1), -jnp.inf, F32)
    for h0 in range(0, SB_HEADS, SB_HEAD_BATCH):
        batch = range(h0, h0 + SB_HEAD_BATCH)
        cols = {h: slice(h * HEAD_DIM, (h + 1) * HEAD_DIM) for h in batch}
        z = {h: _dot_nt(q_ref[0, :, cols[h]], jnp.concatenate([r[0, :, cols[h]] for r in k_refs], axis=0))
             * (HEAD_DIM ** -0.5) for h in batch}
        ls = {h: -_softplus(z[h]) for h in batch}
        lf = {h: jnp.where(vis, ls[h], 0.0) for h in batch}
        lf_b = {h: lf[h].astype(BF16) for h in batch}
        within = {h: [] for h in batch}
        for h in batch:
            both = _dot(lf_b[h][:, 0:2 * SB_BLOCK], upair)
            within[h] += [both[:, 0:SB_BLOCK], both[:, SB_BLOCK:2 * SB_BLOCK]]
        for h in range(h0, h0 + SB_HEAD_BATCH, 2):
            both = _dot(jnp.concatenate([lf_b[h][:, 2 * SB_BLOCK:], lf_b[h + 1][:, 2 * SB_BLOCK:]], axis=1), upair)
            within[h].append(both[:, 0:SB_BLOCK])
            within[h + 1].append(both[:, SB_BLOCK:2 * SB_BLOCK])
        p = {}
        for h in batch:
            later = jnp.zeros((tq, 1), F32)
            shifted = []
            for w in range(SB_WINDOW):
                shifted.append(within[h][w] + later)
                later = later + jnp.sum(lf[h][:, w * SB_BLOCK:(w + 1) * SB_BLOCK], axis=-1, keepdims=True)
            worst = jnp.maximum(worst, later)
            p[h] = jnp.where(vis, jnp.exp((z[h] + ls[h]) + jnp.concatenate(shifted, axis=1)), 0.0).astype(BF16)
        for h in batch:
            acc = _dot(p[h], jnp.concatenate([r[0, :, cols[h]] for r in v_refs], axis=0))
            o_ref[0, :, cols[h]] = acc.astype(o_ref.dtype)
    has_older = base + i + older >= SB_WINDOW
    need_ref[...] = jnp.broadcast_to(jnp.where(has_older, jnp.max(worst), -jnp.inf), need_ref.shape)


def _stick_breaking_window(q, k_bf, v_bf, tq, base, older):
    b, lq, _ = q.shape
    nq = lq // tq
    qspec = pl.BlockSpec((1, tq, SB_WIDTH), lambda s, i: (s, i, 0))
    kspecs = [pl.BlockSpec((1, SB_BLOCK, SB_WIDTH), functools.partial(
        lambda s, i, w: (s, jnp.maximum(base + i - w, 0), 0), w=w)) for w in range(SB_WINDOW)]
    return pl.pallas_call(
        functools.partial(_sb_window_kernel, tq=tq, base=base, older=older),
        grid=(b, nq),
        in_specs=[qspec] + kspecs + kspecs,
        out_specs=[qspec, pl.BlockSpec((1, 1, SUBLANES, HEAD_DIM), lambda s, i: (s, i, 0, 0))],
        out_shape=[jax.ShapeDtypeStruct((b, lq, SB_WIDTH), BF16),
                   jax.ShapeDtypeStruct((b, nq, SUBLANES, HEAD_DIM), F32)],
        compiler_params=_cparams("parallel", "arbitrary"),
        name="stick_breaking_window",
    )(q, k_bf, k_bf, k_bf, v_bf, v_bf, v_bf)


def _layer(x, conv_past, s0, past_k, past_v, chunk, wts):
    b, l, d = x.shape
    m = b * l
    x2 = x.reshape(m, d)
    xn = _rmsnorm(x2, wts["norm1_w"], BF16)
    w_in = wts["w_in"]
    c_q = OFF_A
    c_g = c_q + 3 * SB_WIDTH
    c_ab = c_g + 2 * d
    (ab,) = _matmul(xn, wts["w_ab"], (F32,))
    if m <= 256:
        (proj,) = _matmul(xn, w_in, (F32,), 0, c_ab)
        proj_a, q_sb, gates = proj[:, :OFF_A], proj[:, c_q:c_q + SB_WIDTH].astype(BF16), proj[:, c_g:]
        k_sb, v_sb = proj[:, c_q + SB_WIDTH:c_q + 2 * SB_WIDTH], proj[:, c_q + 2 * SB_WIDTH:c_g]
        k_bf, v_bf = k_sb.astype(BF16), v_sb.astype(BF16)
    else:
        (proj_a,) = _matmul(xn, w_in, (F32,), 0, OFF_A)
        (q_sb,) = _matmul(xn, w_in, (BF16,), c_q, SB_WIDTH)
        k_sb, k_bf = _matmul(xn, w_in, (F32, BF16), c_q + SB_WIDTH, SB_WIDTH)
        v_sb, v_bf = _matmul(xn, w_in, (F32, BF16), c_q + 2 * SB_WIDTH, SB_WIDTH)
        (gates,) = _matmul(xn, w_in, (F32,), c_g, 2 * d)

    o_a, s_new = _gdn(proj_a.reshape(b, l, -1), ab.reshape(b, l, -1), conv_past, s0, wts["conv_w"], wts["A_log"],
                      wts["dt_bias"], wts["gdn_norm_w"], chunk)
    conv_state = proj_a.reshape(b, l, -1)[:, l - (CONV_WIDTH - 1):, :CONV_CH]

    k_bf = k_bf.reshape(b, l, SB_WIDTH)
    v_bf = v_bf.reshape(b, l, SB_WIDTH)
    q3 = q_sb.reshape(b, l, SB_WIDTH)
    if past_k is None:
        o_win, need = _stick_breaking_window(q3, k_bf, v_bf, min(l, SB_BLOCK), 0, 0)

        def full_sweep():
            return _stick_breaking(q3, k_bf, v_bf, 0, min(l, 256))
    else:
        p = past_k.shape[1]
        n_cached = (SB_WINDOW - 1) * SB_BLOCK
        assert p % SB_BLOCK == 0 and p >= n_cached and l <= SB_BLOCK

        def with_cache(new, past, keep):
            return jnp.concatenate([past[:, p - keep:].reshape(b, keep, SB_WIDTH).astype(BF16), new,
                                    jnp.zeros((b, SB_BLOCK - l, SB_WIDTH), BF16)], axis=1)

        o_win, need = _stick_breaking_window(q3, with_cache(k_bf, past_k, n_cached), with_cache(v_bf, past_v, n_cached),
                                             l, SB_WINDOW - 1, (p - n_cached) // SB_BLOCK)

        def full_sweep():
            return _stick_breaking(q3, with_cache(k_bf, past_k, p), with_cache(v_bf, past_v, p), p, l)
    o_b = lax.cond(jnp.max(need) > -SB_UNDERFLOW, full_sweep, lambda: o_win)

    merged = _merge(o_a.reshape(m, GDN_V_WIDTH), o_b.reshape(m, SB_WIDTH), wts["w_gdn_o"], wts["w_sb_o"], gates)
    h, h_gain, sumsq = _matmul_residual(merged, wts["w_out"], x2, wts["norm2_w"])
    hid = _matmul_relu2(h_gain, wts["w_up"], sumsq)
    h = _matmul_residual_ktiled(hid, wts["w_down"], h)
    return (h, conv_state, s_new, k_sb.reshape(b, l, SB_HEADS, HEAD_DIM), v_sb.reshape(b, l, SB_HEADS, HEAD_DIM))


def kernel(x_prompt, x_sample, cache_sb_k, cache_sb_v, state_gdn_S, state_gdn_conv, norm1_w, w_in, conv_w, A_log,
           dt_bias, gdn_norm_w, w_gdn_o, w_sb_o, w_out, norm2_w, w_up, w_down, final_norm_w):
    depth = w_in.shape[0]
    assert depth == 1
    w_in_t = jnp.swapaxes(w_in[0], 0, 1)
    wts = {
        "norm1_w": norm1_w[0], "norm2_w": norm2_w[0], "conv_w": conv_w[0], "A_log": A_log[0], "dt_bias": dt_bias[0],
        "gdn_norm_w": gdn_norm_w[0],
        "w_in": _repack_w_in(w_in_t), "w_ab": _ab_cols(w_in_t),
        "w_gdn_o": w_gdn_o[0].astype(BF16), "w_sb_o": w_sb_o[0].astype(BF16), "w_out": w_out[0].astype(BF16),
        "w_up": w_up[0].astype(BF16), "w_down": w_down[0].astype(BF16),
    }
    bp, lp, d = x_prompt.shape
    bs, ls, _ = x_sample.shape
    conv0 = jnp.zeros((bp, CONV_WIDTH - 1, CONV_CH), F32)
    s_zero = jnp.zeros((bp, GDN_V_HEADS, HEAD_DIM, HEAD_DIM), F32)
    hp, c_p, s_p, k_p, v_p = _layer(x_prompt, conv0, s_zero, None, None, min(PROMPT_CHUNK, lp), wts)
    hs, c_s, s_s, k_s, v_s = _layer(x_sample, state_gdn_conv[0], state_gdn_S[0], cache_sb_k[0], cache_sb_v[0], ls, wts)
    y_prompt = _rmsnorm(hp, final_norm_w, F32).reshape(bp, lp, d)
    y_sample = _rmsnorm(hs, final_norm_w, F32).reshape(bs, ls, d)
    return (y_prompt, y_sample, k_p[None], v_p[None], s_p[None], c_p[None],
            k_s[None], v_s[None], s_s[None].astype(state_gdn_S.dtype), c_s[None])
```

```python
import functools
import math

import jax
import jax.numpy as jnp
from jax import lax
from jax.experimental import pallas as pl
from jax.experimental.pallas import tpu as pltpu

F32 = jnp.float32
BF16 = jnp.bfloat16

EPS = 1e-6
HEAD_DIM = 128
GDN_QK_HEADS = 8
GDN_V_HEADS = 16
GDN_QK_WIDTH = GDN_QK_HEADS * HEAD_DIM
GDN_V_WIDTH = GDN_V_HEADS * HEAD_DIM
CONV_WIDTH = 4
CONV_CH = 2 * GDN_QK_WIDTH + GDN_V_WIDTH
SB_HEADS = 16
SB_WIDTH = SB_HEADS * HEAD_DIM
SB_BLOCK = 128
SB_UNDERFLOW = 106.0
SB_WINDOW = 3
SB_HEAD_BATCH = 4
PROMPT_CHUNK = 64
GDN_CHUNKS_PER_STEP = 2

V7X_MXU_DIM = 256
V7X_VMEM_LIMIT_BYTES = 56 * 1024 * 1024
SUBLANES = 8

OFF_Z = CONV_CH
OFF_A = OFF_Z + GDN_V_WIDTH
OFF_B = OFF_A + GDN_V_HEADS
OFF_SB = OFF_B + GDN_V_HEADS
OFF_GATE = OFF_SB + 3 * SB_WIDTH


def _cparams(*sem):
    return pltpu.CompilerParams(dimension_semantics=sem, vmem_limit_bytes=V7X_VMEM_LIMIT_BYTES)


def _dot(a, b):
    return jnp.dot(a, b, preferred_element_type=F32)


def _dot_nt(a, b):
    return lax.dot_general(a, b, (((1,), (1,)), ((), ())), preferred_element_type=F32)


def _dot_tn(a, b):
    return lax.dot_general(a, b, (((0,), (0,)), ((), ())), preferred_element_type=F32)


def _sigmoid(x):
    return 1.0 / (1.0 + jnp.exp(-x))


def _softplus(x):
    return jnp.maximum(x, 0.0) + jnp.log(1.0 + jnp.exp(-jnp.abs(x)))


def _split3(x):
    x1 = x.astype(BF16)
    r1 = x - x1.astype(F32)
    x2 = r1.astype(BF16)
    x3 = (r1 - x2.astype(F32)).astype(BF16)
    return x1, x2, x3


def _rmsnorm_kernel(x_ref, w_ref, o_ref):
    x = x_ref[...]
    var = jnp.mean(x * x, axis=-1, keepdims=True)
    o_ref[...] = (x * lax.rsqrt(var + EPS) * w_ref[...]).astype(o_ref.dtype)


def _rmsnorm(x, w, out_dtype):
    m, d = x.shape
    tm = min(m, 256)
    return pl.pallas_call(
        _rmsnorm_kernel,
        grid=(m // tm,),
        in_specs=[pl.BlockSpec((tm, d), lambda i: (i, 0)), pl.BlockSpec((1, d), lambda i: (0, 0))],
        out_specs=pl.BlockSpec((tm, d), lambda i: (i, 0)),
        out_shape=jax.ShapeDtypeStruct((m, d), out_dtype),
        compiler_params=_cparams("parallel"),
        name="rmsnorm",
    )(x, w.reshape(1, d).astype(F32))


def _repack_kernel(a_ref, b_ref, o_ref, *, first_shifted, shift):
    j = pl.program_id(0)

    @pl.when(j < first_shifted)
    def _():
        o_ref[...] = a_ref[...].T.astype(o_ref.dtype)

    @pl.when(j >= first_shifted)
    def _():
        src = jnp.concatenate([a_ref[shift:, :], b_ref[...]], axis=0)
        o_ref[...] = src.T.astype(o_ref.dtype)


def _repack_w_in(wt):
    n_src, k = wt.shape
    tn, tk = 1024, 1024
    shift = OFF_SB - OFF_A
    n_out = n_src - shift
    assert OFF_A % tn == 0 and n_out % tn == 0 and k % tk == 0 and tn % shift == 0 and shift % SUBLANES == 0
    return pl.pallas_call(
        functools.partial(_repack_kernel, first_shifted=OFF_A // tn, shift=shift),
        grid=(n_out // tn, k // tk),
        in_specs=[pl.BlockSpec((tn, tk), lambda j, i: (j, i)),
                  pl.BlockSpec((shift, tk), lambda j, i: ((j + 1) * (tn // shift), i))],
        out_specs=pl.BlockSpec((tk, tn), lambda j, i: (i, j)),
        out_shape=jax.ShapeDtypeStruct((k, n_out), BF16),
        compiler_params=_cparams("parallel", "arbitrary"),
        name="repack_w_in",
    )(wt, wt)


def _ab_cols_kernel(a_ref, o_ref, *, n_valid):
    t = a_ref[...].T
    lane = lax.broadcasted_iota(jnp.int32, t.shape, 1)
    o_ref[...] = jnp.where(lane < n_valid, t, 0.0).astype(o_ref.dtype)


def _ab_cols(wt):
    k = wt.shape[1]
    tk = 512
    assert OFF_A % HEAD_DIM == 0 and k % tk == 0
    return pl.pallas_call(
        functools.partial(_ab_cols_kernel, n_valid=OFF_SB - OFF_A),
        grid=(k // tk,),
        in_specs=[pl.BlockSpec((HEAD_DIM, tk), lambda i: (OFF_A // HEAD_DIM, i))],
        out_specs=pl.BlockSpec((tk, HEAD_DIM), lambda i: (i, 0)),
        out_shape=jax.ShapeDtypeStruct((k, HEAD_DIM), BF16),
        compiler_params=_cparams("arbitrary"),
        name="ab_cols",
    )(wt)


def _mm_tiles(m, n, tn_max=1024):
    tm = min(m, 1024)
    tn = min(n, tn_max)
    assert m % tm == 0 and n % tn == 0, (m, n)
    return tm, tn


def _mm_kernel(a_ref, w_ref, *o_refs):
    acc = _dot(a_ref[...], w_ref[...])
    for o_ref in o_refs:
        o_ref[...] = acc.astype(o_ref.dtype)


def _matmul(a, w, out_dtypes, col0=0, n=None):
    m, k = a.shape
    n = w.shape[1] if n is None else n
    tm, tn = _mm_tiles(m, n)
    assert col0 % tn == 0
    jb = col0 // tn
    outs = pl.pallas_call(
        _mm_kernel,
        grid=(m // tm, n // tn),
        in_specs=[pl.BlockSpec((tm, k), lambda i, j: (i, 0)), pl.BlockSpec((k, tn), lambda i, j: (0, j + jb))],
        out_specs=[pl.BlockSpec((tm, tn), lambda i, j: (i, j)) for _ in out_dtypes],
        out_shape=[jax.ShapeDtypeStruct((m, n), dt) for dt in out_dtypes],
        compiler_params=_cparams("parallel", "arbitrary"),
        name="matmul",
    )(a, w)
    return outs


def _mm_relu2_kernel(a_ref, w_ref, ss_ref, o_ref, *, n_norm):
    tn = o_ref.shape[1]
    sumsq = ss_ref[:, 0:HEAD_DIM]
    for c in range(1, ss_ref.shape[1] // HEAD_DIM):
        sumsq = sumsq + ss_ref[:, c * HEAD_DIM:(c + 1) * HEAD_DIM]
    inv_rms = lax.rsqrt(sumsq * (1.0 / n_norm) + EPS)
    acc = _dot(a_ref[...], w_ref[...]) * jnp.concatenate([inv_rms] * (tn // HEAD_DIM), axis=1)
    o_ref[...] = jnp.square(jnp.maximum(acc, 0.0)).astype(o_ref.dtype)


def _matmul_relu2(a, w, sumsq):
    m, k = a.shape
    n = w.shape[1]
    tm, tn = _mm_tiles(m, n)
    return pl.pallas_call(
        functools.partial(_mm_relu2_kernel, n_norm=k),
        grid=(m // tm, n // tn),
        in_specs=[pl.BlockSpec((tm, k), lambda i, j: (i, 0)), pl.BlockSpec((k, tn), lambda i, j: (0, j)),
                  pl.BlockSpec((tm, sumsq.shape[1]), lambda i, j: (i, 0))],
        out_specs=pl.BlockSpec((tm, tn), lambda i, j: (i, j)),
        out_shape=jax.ShapeDtypeStruct((m, n), BF16),
        compiler_params=_cparams("parallel", "arbitrary"),
        name="matmul_relu2",
    )(a, w, sumsq)


def _mm_res_kernel(a_ref, w_ref, r_ref, g_ref, o_ref, ob_ref, ss_ref):
    h = r_ref[...] + _dot(a_ref[...], w_ref[...])
    o_ref[...] = h
    ob_ref[...] = (h * g_ref[...]).astype(ob_ref.dtype)
    ss_ref[...] = jnp.broadcast_to(jnp.sum(h * h, axis=-1, keepdims=True), ss_ref.shape)


def _matmul_residual(a, w, res, gain):
    m, k = a.shape
    n = w.shape[1]
    tm, tn = _mm_tiles(m, n, 512)
    tile = pl.BlockSpec((tm, tn), lambda i, j: (i, j))
    return pl.pallas_call(
        _mm_res_kernel,
        grid=(m // tm, n // tn),
        in_specs=[pl.BlockSpec((tm, k), lambda i, j: (i, 0)), pl.BlockSpec((k, tn), lambda i, j: (0, j)), tile,
                  pl.BlockSpec((1, tn), lambda i, j: (0, j))],
        out_specs=[tile, tile, pl.BlockSpec((tm, HEAD_DIM), lambda i, j: (i, j))],
        out_shape=[jax.ShapeDtypeStruct((m, n), F32), jax.ShapeDtypeStruct((m, n), BF16),
                   jax.ShapeDtypeStruct((m, (n // tn) * HEAD_DIM), F32)],
        compiler_params=_cparams("parallel", "arbitrary"),
        name="matmul_residual",
    )(a, w, res, gain.reshape(1, n).astype(F32))


def _mm_kacc_kernel(a_ref, w_ref, r_ref, o_ref):
    @pl.when(pl.program_id(2) == 0)
    def _():
        o_ref[...] = r_ref[...]

    o_ref[...] += _dot(a_ref[...], w_ref[...])


def _matmul_residual_ktiled(a, w, res):
    m, k = a.shape
    n = w.shape[1]
    tm = min(m, 1024)
    tn = min(n, 1024)
    tk = min(k, 2048)
    return pl.pallas_call(
        _mm_kacc_kernel,
        grid=(m // tm, n // tn, k // tk),
        in_specs=[pl.BlockSpec((tm, tk), lambda i, j, l: (i, l)), pl.BlockSpec((tk, tn), lambda i, j, l: (l, j)),
                  pl.BlockSpec((tm, tn), lambda i, j, l: (i, j))],
        out_specs=pl.BlockSpec((tm, tn), lambda i, j, l: (i, j)),
        out_shape=jax.ShapeDtypeStruct((m, n), F32),
        compiler_params=_cparams("parallel", "parallel", "arbitrary"),
        name="matmul_residual_ktiled",
    )(a, w, res)


def _merge_kernel(oa_ref, ob_ref, wa_ref, wb_ref, ga_ref, gb_ref, o_ref):
    ya = _dot(oa_ref[...], wa_ref[...])
    yb = _dot(ob_ref[...], wb_ref[...])
    o_ref[...] = (_sigmoid(ga_ref[...]) * ya + _sigmoid(gb_ref[...]) * yb).astype(o_ref.dtype)


def _merge(oa, ob, wa, wb, gates):
    m, k = oa.shape
    n = wa.shape[1]
    tm, tn = _mm_tiles(m, n, 512)
    nb = n // tn
    return pl.pallas_call(
        _merge_kernel,
        grid=(m // tm, nb),
        in_specs=[pl.BlockSpec((tm, k), lambda i, j: (i, 0)), pl.BlockSpec((tm, k), lambda i, j: (i, 0)),
                  pl.BlockSpec((k, tn), lambda i, j: (0, j)), pl.BlockSpec((k, tn), lambda i, j: (0, j)),
                  pl.BlockSpec((tm, tn), lambda i, j: (i, j)), pl.BlockSpec((tm, tn), lambda i, j: (i, j + nb))],
        out_specs=pl.BlockSpec((tm, tn), lambda i, j: (i, j)),
        out_shape=jax.ShapeDtypeStruct((m, n), BF16),
        compiler_params=_cparams("parallel", "arbitrary"),
        name="merge",
    )(oa, ob, wa, wb, gates, gates)


def _gdn_kernel(qkv_ref, z_ref, abc_ref, ar_ref, *rest, chunk, group, n_sub):
    o_ref, act_ref = rest[8], rest[11]
    for sc in range(n_sub):
        rows = slice(sc * chunk, (sc + 1) * chunk)
        views = (qkv_ref.at[:, rows], z_ref.at[:, rows], abc_ref.at[:, rows], ar_ref.at[:, sc:sc + 1])
        _gdn_chunk(sc, *views, *rest[:8], o_ref.at[:, rows], *rest[9:11], act_ref.at[rows], *rest[12:],
                   chunk=chunk, group=group)


def _gdn_chunk(sub, qkv_ref, z_ref, abc_ref, ar_ref, past_ref, s0_ref, cw_ref, alc_ref, dtc_ref, alr_ref, dtr_ref,
               gnw_ref, o_ref, s_ref, xp_ref, act_ref, ltri_ref, ubd_ref, cmask_ref, smask_ref, *, chunk, group):
    c_len, g_sz = chunk, group
    r_len = c_len * g_sz
    n_groups = GDN_V_HEADS // g_sz
    rep = GDN_V_HEADS // GDN_QK_HEADS
    first_of_stream = jnp.logical_and(pl.program_id(1) == 0, sub == 0)

    @pl.when(first_of_stream)
    def _init():
        s_ref[...] = s0_ref[...]
        xp_ref[0:SUBLANES, :] = past_ref[0]
        ri = lax.broadcasted_iota(jnp.int32, (r_len, r_len), 0)
        ci = lax.broadcasted_iota(jnp.int32, (r_len, r_len), 1)
        shift = int(math.log2(c_len))
        same = (ri >> shift) == (ci >> shift)
        cmask_ref[...] = jnp.where(same & (ri >= ci), 1.0, 0.0)
        smask_ref[...] = jnp.where(same & (ri > ci), 1.0, 0.0)
        ubd_ref[...] = jnp.where(same & (ri <= ci), 1.0, 0.0).astype(BF16)
        rc = lax.broadcasted_iota(jnp.int32, (c_len, c_len), 0)
        cc = lax.broadcasted_iota(jnp.int32, (c_len, c_len), 1)
        ltri_ref[...] = jnp.where(rc >= cc, 1.0, 0.0).astype(BF16)

    @pl.when(jnp.logical_not(first_of_stream))
    def _carry():
        xp_ref[0:SUBLANES, :] = xp_ref[c_len:c_len + SUBLANES, :]

    xp_ref[SUBLANES:SUBLANES + c_len, :] = qkv_ref[0]

    for s in range(CONV_CH // HEAD_DIM):
        cols = slice(s * HEAD_DIM, (s + 1) * HEAD_DIM)
        y = cw_ref[0:1, cols] * xp_ref[SUBLANES - 3:SUBLANES - 3 + c_len, cols]
        for i in range(1, CONV_WIDTH):
            y = y + cw_ref[i:i + 1, cols] * xp_ref[SUBLANES - 3 + i:SUBLANES - 3 + i + c_len, cols]
        y = y * _sigmoid(y)
        if s < 2 * GDN_QK_HEADS:
            y = y * lax.rsqrt(jnp.sum(y * y, axis=-1, keepdims=True) + EPS)
            if s < GDN_QK_HEADS:
                y = y * (HEAD_DIM ** -0.5)
        act_ref[:, cols] = y

    ab = abc_ref[0]
    g_col = -jnp.exp(alc_ref[...]) * _softplus(ab + dtc_ref[...])
    beta_col = _sigmoid(ab)
    ltri = ltri_ref[...]
    gc_col = sum(_dot(ltri, t) for t in _split3(g_col))
    g_row = -jnp.exp(alr_ref[...]) * _softplus(ar_ref[0, 0] + dtr_ref[...])
    ubd = ubd_ref[...]
    gc_row = sum(_dot(t, ubd) for t in _split3(g_row))

    cmask = cmask_ref[...]
    smask = smask_ref[...]
    gnw = gnw_ref[...]

    groups = range(n_groups)
    heads_of = [[grp * g_sz + hb for hb in range(g_sz)] for grp in groups]

    def stacked(off, div):
        return [jnp.concatenate([act_ref[:, off + (h // div) * HEAD_DIM:off + (h // div + 1) * HEAD_DIM]
                                 for h in heads_of[grp]], axis=0) for grp in groups]

    kst = stacked(GDN_QK_WIDTH, rep)
    qst = stacked(0, rep)
    vst = stacked(2 * GDN_QK_WIDTH, 1)
    gcb = [jnp.concatenate([jnp.broadcast_to(gc_col[:, h:h + 1], (c_len, HEAD_DIM)) for h in heads_of[grp]], axis=0)
           for grp in groups]
    bcb = [jnp.concatenate([jnp.broadcast_to(beta_col[:, GDN_V_HEADS + h:GDN_V_HEADS + h + 1], (c_len, HEAD_DIM))
                            for h in heads_of[grp]], axis=0) for grp in groups]
    kst_b = [t.astype(BF16) for t in kst]
    gram = [_dot_nt(kst_b[grp], kst_b[grp]) for grp in groups]
    qk = [_dot_nt(qst[grp].astype(BF16), kst_b[grp]) for grp in groups]
    n_rep = r_len // HEAD_DIM
    aqk, tp, xpow = [], [], []
    for grp in groups:
        gcb2 = jnp.concatenate([gcb[grp]] * n_rep, axis=1)
        nbeta = -bcb[grp]
        nbeta2 = jnp.concatenate([nbeta] * n_rep, axis=1) * smask
        decay = jnp.exp(jnp.minimum(gcb2 - gc_row[grp:grp + 1, :], 0.0)) * cmask
        aqk.append((qk[grp] * decay).astype(BF16))
        nm = (gram[grp] * decay) * nbeta2
        tp.append(nm)
        xpow.append(nm)
    n_steps = int(math.log2(c_len))
    for step in range(n_steps):
        for grp in groups:
            xb = xpow[grp].astype(BF16)
            tp[grp] = tp[grp] + _dot(xb, tp[grp].astype(BF16))
            if step + 1 < n_steps:
                xpow[grp] = _dot(xb, xb)
    eg = [jnp.exp(gcb[grp]) for grp in groups]
    sol = []
    for grp in groups:
        rhs = jnp.concatenate([vst[grp] * bcb[grp], kst[grp] * (bcb[grp] * eg[grp])], axis=1)
        sol.append(rhs + _dot(tp[grp].astype(BF16), rhs.astype(BF16)))
    qss = {}
    ust = []
    for grp in groups:
        qg = qst[grp] * eg[grp]
        us = []
        for hb, h in enumerate(heads_of[grp]):
            rows = slice(hb * c_len, (hb + 1) * c_len)
            s_old = s_ref[0, h]
            wq = jnp.concatenate([sol[grp][rows, HEAD_DIM:2 * HEAD_DIM], qg[rows]], axis=0).astype(BF16)
            r1 = _dot(wq, s_old.astype(BF16))
            u = sol[grp][rows, 0:HEAD_DIM] - r1[0:c_len]
            qss[h] = r1[c_len:2 * c_len]
            gl = gcb[grp][(hb + 1) * c_len - 1:(hb + 1) * c_len, :]
            kd = kst[grp][rows] * jnp.exp(gl - gcb[grp][rows])
            s_ref[0, h] = jnp.exp(gl) * s_old + _dot_tn(kd.astype(BF16), u.astype(BF16))
            us.append(u)
        ust.append(jnp.concatenate(us, axis=0).astype(BF16))
    o_intra = [_dot(aqk[grp], ust[grp]) for grp in groups]
    for grp in groups:
        for hb, h in enumerate(heads_of[grp]):
            rows = slice(hb * c_len, (hb + 1) * c_len)
            o = qss[h] + o_intra[grp][rows]
            o = o * lax.rsqrt(jnp.mean(o * o, axis=-1, keepdims=True) + EPS) * gnw
            zz = z_ref[0, :, h * HEAD_DIM:(h + 1) * HEAD_DIM]
            o_ref[0, :, h * HEAD_DIM:(h + 1) * HEAD_DIM] = (o * (zz * _sigmoid(zz))).astype(o_ref.dtype)


def _gdn(proj_a, ab, conv_past, s0, conv_w, a_log, dt_bias, gnw, chunk):
    b, l, _ = proj_a.shape
    n_chunks = l // chunk
    group = V7X_MXU_DIM // chunk
    n_groups = GDN_V_HEADS // group
    r_len = chunk * group
    a_row = ab[:, :, :GDN_V_HEADS].reshape(b, n_chunks, chunk, n_groups, group)
    a_row = a_row.transpose(0, 1, 3, 4, 2).reshape(b, n_chunks, n_groups, r_len)
    past8 = jnp.pad(conv_past.astype(F32), ((0, 0), (SUBLANES - (CONV_WIDTH - 1), 0), (0, 0)))
    pad16 = HEAD_DIM - GDN_V_HEADS
    alc = jnp.pad(a_log.astype(F32), (0, pad16)).reshape(1, HEAD_DIM)
    dtc = jnp.pad(dt_bias.astype(F32), (0, pad16)).reshape(1, HEAD_DIM)
    alr = jnp.repeat(a_log.astype(F32), chunk).reshape(n_groups, r_len)
    dtr = jnp.repeat(dt_bias.astype(F32), chunk).reshape(n_groups, r_len)
    const = lambda *shape: pl.BlockSpec(shape, lambda i, c: (0,) * len(shape))
    n_sub = GDN_CHUNKS_PER_STEP if n_chunks % GDN_CHUNKS_PER_STEP == 0 else 1
    rows = n_sub * chunk
    return pl.pallas_call(
        functools.partial(_gdn_kernel, chunk=chunk, group=group, n_sub=n_sub),
        grid=(b, n_chunks // n_sub),
        in_specs=[
            pl.BlockSpec((1, rows, CONV_CH), lambda i, c: (i, c, 0)),
            pl.BlockSpec((1, rows, GDN_V_WIDTH), lambda i, c: (i, c, CONV_CH // GDN_V_WIDTH)),
            pl.BlockSpec((1, rows, HEAD_DIM), lambda i, c: (i, c, 0)),
            pl.BlockSpec((1, n_sub, n_groups, r_len), lambda i, c: (i, c, 0, 0)),
            pl.BlockSpec((1, SUBLANES, CONV_CH), lambda i, c: (i, 0, 0)),
            pl.BlockSpec((1, GDN_V_HEADS, HEAD_DIM, HEAD_DIM), lambda i, c: (i, 0, 0, 0)),
            const(CONV_WIDTH, CONV_CH), const(1, HEAD_DIM), const(1, HEAD_DIM),
            const(n_groups, r_len), const(n_groups, r_len), const(1, HEAD_DIM),
        ],
        out_specs=[
            pl.BlockSpec((1, rows, GDN_V_WIDTH), lambda i, c: (i, c, 0)),
            pl.BlockSpec((1, GDN_V_HEADS, HEAD_DIM, HEAD_DIM), lambda i, c: (i, 0, 0, 0)),
        ],
        out_shape=[jax.ShapeDtypeStruct((b, l, GDN_V_WIDTH), BF16),
                   jax.ShapeDtypeStruct((b, GDN_V_HEADS, HEAD_DIM, HEAD_DIM), F32)],
        scratch_shapes=[
            pltpu.VMEM((chunk + SUBLANES, CONV_CH), F32),
            pltpu.VMEM((rows, CONV_CH), F32),
            pltpu.VMEM((chunk, chunk), BF16),
            pltpu.VMEM((r_len, r_len), BF16),
            pltpu.VMEM((r_len, r_len), F32),
            pltpu.VMEM((r_len, r_len), F32),
        ],
        compiler_params=_cparams("parallel", "arbitrary"),
        name="gated_delta_rule",
    )(proj_a, proj_a, ab, a_row, past8, s0.astype(F32), conv_w.astype(F32), alc, dtc, alr, dtr,
      gnw.astype(F32).reshape(1, HEAD_DIM))


def _sb_kernel(q_ref, kt_ref, v_ref, o_ref, *, tq, q_pos0):
    i = pl.program_id(2)
    q = q_ref[0]
    qpos0 = q_pos0 + i * tq
    row = lax.broadcasted_iota(jnp.int32, (tq, SB_BLOCK), 0)
    col = lax.broadcasted_iota(jnp.int32, (tq, SB_BLOCK), 1)
    ri = lax.broadcasted_iota(jnp.int32, (SB_BLOCK, SB_BLOCK), 0)
    ci = lax.broadcasted_iota(jnp.int32, (SB_BLOCK, SB_BLOCK), 1)
    ustrict = jnp.where(ri > ci, 1.0, 0.0).astype(BF16)
    n_masked = max(tq // SB_BLOCK, 1)
    j_full = qpos0 // SB_BLOCK

    def tile(j, later, acc, masked):
        kt = kt_ref[0, 0, j]
        vj = v_ref[0, pl.ds(pl.multiple_of(j * SB_BLOCK, SB_BLOCK), SB_BLOCK), :]
        z = _dot(q, kt) * (HEAD_DIM ** -0.5)
        ls = -_softplus(z)
        if masked:
            vis = (j * SB_BLOCK + col) < (qpos0 + row)
            lf = jnp.where(vis, ls, 0.0)
        else:
            lf = ls
        within = _dot(lf.astype(BF16), ustrict)
        p = jnp.exp((z + ls) + within + later)
        if masked:
            p = jnp.where(vis, p, 0.0)
        acc = acc + _dot(p.astype(BF16), vj)
        return later + jnp.sum(lf, axis=-1, keepdims=True), acc

    later = jnp.zeros((tq, 1), F32)
    acc = jnp.zeros((tq, HEAD_DIM), F32)
    for m in range(n_masked):
        later, acc = tile(j_full + (n_masked - 1 - m), later, acc, True)

    def cond(carry):
        return jnp.logical_and(carry[0] >= 0, jnp.max(carry[1]) > -SB_UNDERFLOW)

    def body(carry):
        later, acc = tile(carry[0], carry[1], carry[2], False)
        return carry[0] - 1, later, acc

    _, later, acc = lax.while_loop(cond, body, (j_full - 1, later, acc))
    o_ref[0] = acc.astype(o_ref.dtype)


def _stick_breaking(q, k_bf, v_bf, q_pos0, tq):
    b, lq, _ = q.shape
    lk = k_bf.shape[1]
    nkb = lk // SB_BLOCK
    kt = k_bf.reshape(b, nkb, SB_BLOCK, SB_HEADS, HEAD_DIM).transpose(0, 3, 1, 4, 2)
    return pl.pallas_call(
        functools.partial(_sb_kernel, tq=tq, q_pos0=q_pos0),
        grid=(b, SB_HEADS, lq // tq),
        in_specs=[
            pl.BlockSpec((1, tq, HEAD_DIM), lambda s, h, i: (s, i, h)),
            pl.BlockSpec((1, 1, nkb, HEAD_DIM, SB_BLOCK), lambda s, h, i: (s, h, 0, 0, 0)),
            pl.BlockSpec((1, lk, HEAD_DIM), lambda s, h, i: (s, 0, h)),
        ],
        out_specs=pl.BlockSpec((1, tq, HEAD_DIM), lambda s, h, i: (s, i, h)),
        out_shape=jax.ShapeDtypeStruct((b, lq, SB_WIDTH), BF16),
        compiler_params=_cparams("parallel", "parallel", "arbitrary"),
        name="stick_breaking",
    )(q, kt, v_bf)


def _sb_window_kernel(q_ref, k0_ref, k1_ref, k2_ref, v0_ref, v1_ref, v2_ref, o_ref, need_ref, *, tq, base, older):
    i = pl.program_id(1)
    row = lax.broadcasted_iota(jnp.int32, (tq, SB_WINDOW * SB_BLOCK), 0)
    col = lax.broadcasted_iota(jnp.int32, (tq, SB_WINDOW * SB_BLOCK), 1)
    assert SB_WINDOW == 3 and SB_HEAD_BATCH % 2 == 0
    ri = lax.broadcasted_iota(jnp.int32, (2 * SB_BLOCK, 2 * SB_BLOCK), 0)
    ci = lax.broadcasted_iota(jnp.int32, (2 * SB_BLOCK, 2 * SB_BLOCK), 1)
    same = (ri < SB_BLOCK) == (ci < SB_BLOCK)
    upair = jnp.where(same & (ri > ci), 1.0, 0.0).astype(BF16)
    k_refs = (k0_ref, k1_ref, k2_ref)
    v_refs = (v0_ref, v1_ref, v2_ref)
    n_win = SB_WINDOW * SB_BLOCK
    limit = row
    for w in range(1, SB_WINDOW):
        limit = jnp.where(col < w * SB_BLOCK, limit, jnp.where(base + i - w >= 0, n_win, 0))
    vis = col < limit
    worst = jnp.full((tq, 1), -jnp.inf, F32)
    for h0 in range(0, SB_HEADS, SB_HEAD_BATCH):
        batch = range(h0, h0 + SB_HEAD_BATCH)
        cols = {h: slice(h * HEAD_DIM, (h + 1) * HEAD_DIM) for h in batch}
        z = {h: _dot_nt(q_ref[0, :, cols[h]], jnp.concatenate([r[0, :, cols[h]] for r in k_refs], axis=0))
             * (HEAD_DIM ** -0.5) for h in batch}
        ls = {h: -_softplus(z[h]) for h in batch}
        lf = {h: jnp.where(vis, ls[h], 0.0) for h in batch}
        lf_b = {h: lf[h].astype(BF16) for h in batch}
        within = {h: [] for h in batch}
        for h in batch:
            both = _dot(lf_b[h][:, 0:2 * SB_BLOCK], upair)
            within[h] += [both[:, 0:SB_BLOCK], both[:, SB_BLOCK:2 * SB_BLOCK]]
        for h in range(h0, h0 + SB_HEAD_BATCH, 2):
            both = _dot(jnp.concatenate([lf_b[h][:, 2 * SB_BLOCK:], lf_b[h + 1][:, 2 * SB_BLOCK:]], axis=1), upair)
            within[h].append(both[:, 0:SB_BLOCK])
            within[h + 1].append(both[:, SB_BLOCK:2 * SB_BLOCK])
        p = {}
        for h in batch:
            later = jnp.zeros((tq, 1), F32)
            shifted = []
            for w in range(SB_WINDOW):
                shifted.append(within[h][w] + later)
                later = later + jnp.sum(lf[h][:, w * SB_BLOCK:(w + 1) * SB_BLOCK], axis=-1, keepdims=True)
            worst = jnp.maximum(worst, later)
            p[h] = jnp.where(vis, jnp.exp((z[h] + ls[h]) + jnp.concatenate(shifted, axis=1)), 0.0).astype(BF16)
        for h in batch:
            acc = _dot(p[h], jnp.concatenate([r[0, :, cols[h]] for r in v_refs], axis=0))
            o_ref[0, :, cols[h]] = acc.astype(o_ref.dtype)
    has_older = base + i + older >= SB_WINDOW
    need_ref[...] = jnp.broadcast_to(jnp.where(has_older, jnp.max(worst), -jnp.inf), need_ref.shape)


def _stick_breaking_window(q, k_bf, v_bf, tq, base, older):
    b, lq, _ = q.shape
    nq = lq // tq
    qspec = pl.BlockSpec((1, tq, SB_WIDTH), lambda s, i: (s, i, 0))
    kspecs = [pl.BlockSpec((1, SB_BLOCK, SB_WIDTH), functools.partial(
        lambda s, i, w: (s, jnp.maximum(base + i - w, 0), 0), w=w)) for w in range(SB_WINDOW)]
    return pl.pallas_call(
        functools.partial(_sb_window_kernel, tq=tq, base=base, older=older),
        grid=(b, nq),
        in_specs=[qspec] + kspecs + kspecs,
        out_specs=[qspec, pl.BlockSpec((1, 1, SUBLANES, HEAD_DIM), lambda s, i: (s, i, 0, 0))],
        out_shape=[jax.ShapeDtypeStruct((b, lq, SB_WIDTH), BF16),
                   jax.ShapeDtypeStruct((b, nq, SUBLANES, HEAD_DIM), F32)],
        compiler_params=_cparams("parallel", "arbitrary"),
        name="stick_breaking_window",
    )(q, k_bf, k_bf, k_bf, v_bf, v_bf, v_bf)


def _layer(x, conv_past, s0, past_k, past_v, chunk, wts):
    b, l, d = x.shape
    m = b * l
    x2 = x.reshape(m, d)
    xn = _rmsnorm(x2, wts["norm1_w"], BF16)
    w_in = wts["w_in"]
    c_q = OFF_A
    c_g = c_q + 3 * SB_WIDTH
    c_ab = c_g + 2 * d
    (ab,) = _matmul(xn, wts["w_ab"], (F32,))
    if m <= 256:
        (proj,) = _matmul(xn, w_in, (F32,), 0, c_ab)
        proj_a, q_sb, gates = proj[:, :OFF_A], proj[:, c_q:c_q + SB_WIDTH].astype(BF16), proj[:, c_g:]
        k_sb, v_sb = proj[:, c_q + SB_WIDTH:c_q + 2 * SB_WIDTH], proj[:, c_q + 2 * SB_WIDTH:c_g]
        k_bf, v_bf = k_sb.astype(BF16), v_sb.astype(BF16)
    else:
        (proj_a,) = _matmul(xn, w_in, (F32,), 0, OFF_A)
        (q_sb,) = _matmul(xn, w_in, (BF16,), c_q, SB_WIDTH)
        k_sb, k_bf = _matmul(xn, w_in, (F32, BF16), c_q + SB_WIDTH, SB_WIDTH)
        v_sb, v_bf = _matmul(xn, w_in, (F32, BF16), c_q + 2 * SB_WIDTH, SB_WIDTH)
        (gates,) = _matmul(xn, w_in, (F32,), c_g, 2 * d)

    o_a, s_new = _gdn(proj_a.reshape(b, l, -1), ab.reshape(b, l, -1), conv_past, s0, wts["conv_w"], wts["A_log"],
                      wts["dt_bias"], wts["gdn_norm_w"], chunk)
    conv_state = proj_a.reshape(b, l, -1)[:, l - (CONV_WIDTH - 1):, :CONV_CH]

    k_bf = k_bf.reshape(b, l, SB_WIDTH)
    v_bf = v_bf.reshape(b, l, SB_WIDTH)
    q3 = q_sb.reshape(b, l, SB_WIDTH)
    if past_k is None:
        o_win, need = _stick_breaking_window(q3, k_bf, v_bf, min(l, SB_BLOCK), 0, 0)

        def full_sweep():
            return _stick_breaking(q3, k_bf, v_bf, 0, min(l, 256))
    else:
        p = past_k.shape[1]
        n_cached = (SB_WINDOW - 1) * SB_BLOCK
        assert p % SB_BLOCK == 0 and p >= n_cached and l <= SB_BLOCK

        def with_cache(new, past, keep):
            return jnp.concatenate([past[:, p - keep:].reshape(b, keep, SB_WIDTH).astype(BF16), new,
                                    jnp.zeros((b, SB_BLOCK - l, SB_WIDTH), BF16)], axis=1)

        o_win, need = _stick_breaking_window(q3, with_cache(k_bf, past_k, n_cached), with_cache(v_bf, past_v, n_cached),
                                             l, SB_WINDOW - 1, (p - n_cached) // SB_BLOCK)

        def full_sweep():
            return _stick_breaking(q3, with_cache(k_bf, past_k, p), with_cache(v_bf, past_v, p), p, l)
    o_b = lax.cond(jnp.max(need) > -SB_UNDERFLOW, full_sweep, lambda: o_win)

    merged = _merge(o_a.reshape(m, GDN_V_WIDTH), o_b.reshape(m, SB_WIDTH), wts["w_gdn_o"], wts["w_sb_o"], gates)
    h, h_gain, sumsq = _matmul_residual(merged, wts["w_out"], x2, wts["norm2_w"])
    hid = _matmul_relu2(h_gain, wts["w_up"], sumsq)
    h = _matmul_residual_ktiled(hid, wts["w_down"], h)
    return (h, conv_state, s_new, k_sb.reshape(b, l, SB_HEADS, HEAD_DIM), v_sb.reshape(b, l, SB_HEADS, HEAD_DIM))


def kernel(x_prompt, x_sample, cache_sb_k, cache_sb_v, state_gdn_S, state_gdn_conv, norm1_w, w_in, conv_w, A_log,
           dt_bias, gdn_norm_w, w_gdn_o, w_sb_o, w_out, norm2_w, w_up, w_down, final_norm_w):
    depth = w_in.shape[0]
    assert depth == 1
    w_in_t = jnp.swapaxes(w_in[0], 0, 1)
    wts = {
        "norm1_w": norm1_w[0], "norm2_w": norm2_w[0], "conv_w": conv_w[0], "A_log": A_log[0], "dt_bias": dt_bias[0],
        "gdn_norm_w": gdn_norm_w[0],
        "w_in": _repack_w_in(w_in_t), "w_ab": _ab_cols(w_in_t),
        "w_gdn_o": w_gdn_o[0].astype(BF16), "w_sb_o": w_sb_o[0].astype(BF16), "w_out": w_out[0].astype(BF16),
        "w_up": w_up[0].astype(BF16), "w_down": w_down[0].astype(BF16),
    }
    bp, lp, d = x_prompt.shape
    bs, ls, _ = x_sample.shape
    conv0 = jnp.zeros((bp, CONV_WIDTH - 1, CONV_CH), F32)
    s_zero = jnp.zeros((bp, GDN_V_HEADS, HEAD_DIM, HEAD_DIM), F32)
    hp, c_p, s_p, k_p, v_p = _layer(x_prompt, conv0, s_zero, None, None, min(PROMPT_CHUNK, lp), wts)
    hs, c_s, s_s, k_s, v_s = _layer(x_sample, state_gdn_conv[0], state_gdn_S[0], cache_sb_k[0], cache_sb_v[0], ls, wts)
    y_prompt = _rmsnorm(hp, final_norm_w, F32).reshape(bp, lp, d)
    y_sample = _rmsnorm(hs, final_norm_w, F32).reshape(bs, ls, d)
    return (y_prompt, y_sample, k_p[None], v_p[None], s_p[None], c_p[None],
            k_s[None], v_s[None], s_s[None].astype(state_gdn_S.dtype), c_s[None])
```

```python
import functools
import math

import jax
import jax.numpy as jnp
from jax import lax
from jax.experimental import pallas as pl
from jax.experimental.pallas import tpu as pltpu

F32 = jnp.float32
BF16 = jnp.bfloat16

EPS = 1e-6
HEAD_DIM = 128
GDN_QK_HEADS = 8
GDN_V_HEADS = 16
GDN_QK_WIDTH = GDN_QK_HEADS * HEAD_DIM
GDN_V_WIDTH = GDN_V_HEADS * HEAD_DIM
CONV_WIDTH = 4
CONV_CH = 2 * GDN_QK_WIDTH + GDN_V_WIDTH
SB_HEADS = 16
SB_WIDTH = SB_HEADS * HEAD_DIM
SB_BLOCK = 128
SB_UNDERFLOW = 106.0
SB_WINDOW = 3
SB_HEAD_BATCH = 4
PROMPT_CHUNK = 64
GDN_CHUNKS_PER_STEP = 2

V7X_MXU_DIM = 256
V7X_VMEM_LIMIT_BYTES = 56 * 1024 * 1024
SUBLANES = 8

OFF_Z = CONV_CH
OFF_A = OFF_Z + GDN_V_WIDTH
OFF_B = OFF_A + GDN_V_HEADS
OFF_SB = OFF_B + GDN_V_HEADS
OFF_GATE = OFF_SB + 3 * SB_WIDTH


def _cparams(*sem):
    return pltpu.CompilerParams(dimension_semantics=sem, vmem_limit_bytes=V7X_VMEM_LIMIT_BYTES)


def _dot(a, b):
    return jnp.dot(a, b, preferred_element_type=F32)


def _dot_nt(a, b):
    return lax.dot_general(a, b, (((1,), (1,)), ((), ())), preferred_element_type=F32)


def _dot_tn(a, b):
    return lax.dot_general(a, b, (((0,), (0,)), ((), ())), preferred_element_type=F32)


def _sigmoid(x):
    return 1.0 / (1.0 + jnp.exp(-x))


def _softplus(x):
    return jnp.maximum(x, 0.0) + jnp.log(1.0 + jnp.exp(-jnp.abs(x)))


def _split3(x):
    x1 = x.astype(BF16)
    r1 = x - x1.astype(F32)
    x2 = r1.astype(BF16)
    x3 = (r1 - x2.astype(F32)).astype(BF16)
    return x1, x2, x3


def _rmsnorm_kernel(x_ref, w_ref, o_ref):
    x = x_ref[...]
    var = jnp.mean(x * x, axis=-1, keepdims=True)
    o_ref[...] = (x * lax.rsqrt(var + EPS) * w_ref[...]).astype(o_ref.dtype)


def _rmsnorm(x, w, out_dtype):
    m, d = x.shape
    tm = min(m, 256)
    return pl.pallas_call(
        _rmsnorm_kernel,
        grid=(m // tm,),
        in_specs=[pl.BlockSpec((tm, d), lambda i: (i, 0)), pl.BlockSpec((1, d), lambda i: (0, 0))],
        out_specs=pl.BlockSpec((tm, d), lambda i: (i, 0)),
        out_shape=jax.ShapeDtypeStruct((m, d), out_dtype),
        compiler_params=_cparams("parallel"),
        name="rmsnorm",
    )(x, w.reshape(1, d).astype(F32))


def _repack_kernel(a_ref, b_ref, o_ref, *, first_shifted, shift):
    j = pl.program_id(0)

    @pl.when(j < first_shifted)
    def _():
        o_ref[...] = a_ref[...].T.astype(o_ref.dtype)

    @pl.when(j >= first_shifted)
    def _():
        src = jnp.concatenate([a_ref[shift:, :], b_ref[...]], axis=0)
        o_ref[...] = src.T.astype(o_ref.dtype)


def _repack_w_in(wt):
    n_src, k = wt.shape
    tn, tk = 1024, 1024
    shift = OFF_SB - OFF_A
    n_out = n_src - shift
    assert OFF_A % tn == 0 and n_out % tn == 0 and k % tk == 0 and tn % shift == 0 and shift % SUBLANES == 0
    return pl.pallas_call(
        functools.partial(_repack_kernel, first_shifted=OFF_A // tn, shift=shift),
        grid=(n_out // tn, k // tk),
        in_specs=[pl.BlockSpec((tn, tk), lambda j, i: (j, i)),
                  pl.BlockSpec((shift, tk), lambda j, i: ((j + 1) * (tn // shift), i))],
        out_specs=pl.BlockSpec((tk, tn), lambda j, i: (i, j)),
        out_shape=jax.ShapeDtypeStruct((k, n_out), BF16),
        compiler_params=_cparams("parallel", "arbitrary"),
        name="repack_w_in",
    )(wt, wt)


def _ab_cols_kernel(a_ref, o_ref, *, n_valid):
    t = a_ref[...].T
    lane = lax.broadcasted_iota(jnp.int32, t.shape, 1)
    o_ref[...] = jnp.where(lane < n_valid, t, 0.0).astype(o_ref.dtype)


def _ab_cols(wt):
    k = wt.shape[1]
    tk = 512
    assert OFF_A % HEAD_DIM == 0 and k % tk == 0
    return pl.pallas_call(
        functools.partial(_ab_cols_kernel, n_valid=OFF_SB - OFF_A),
        grid=(k // tk,),
        in_specs=[pl.BlockSpec((HEAD_DIM, tk), lambda i: (OFF_A // HEAD_DIM, i))],
        out_specs=pl.BlockSpec((tk, HEAD_DIM), lambda i: (i, 0)),
        out_shape=jax.ShapeDtypeStruct((k, HEAD_DIM), BF16),
        compiler_params=_cparams("arbitrary"),
        name="ab_cols",
    )(wt)


def _mm_tiles(m, n, tn_max=1024):
    tm = min(m, 1024)
    tn = min(n, tn_max)
    assert m % tm == 0 and n % tn == 0, (m, n)
    return tm, tn


def _mm_kernel(a_ref, w_ref, *o_refs):
    acc = _dot(a_ref[...], w_ref[...])
    for o_ref in o_refs:
        o_ref[...] = acc.astype(o_ref.dtype)


def _matmul(a, w, out_dtypes, col0=0, n=None):
    m, k = a.shape
    n = w.shape[1] if n is None else n
    tm, tn = _mm_tiles(m, n)
    assert col0 % tn == 0
    jb = col0 // tn
    outs = pl.pallas_call(
        _mm_kernel,
        grid=(m // tm, n // tn),
        in_specs=[pl.BlockSpec((tm, k), lambda i, j: (i, 0)), pl.BlockSpec((k, tn), lambda i, j: (0, j + jb))],
        out_specs=[pl.BlockSpec((tm, tn), lambda i, j: (i, j)) for _ in out_dtypes],
        out_shape=[jax.ShapeDtypeStruct((m, n), dt) for dt in out_dtypes],
        compiler_params=_cparams("parallel", "arbitrary"),
        name="matmul",
    )(a, w)
    return outs


def _mm_relu2_kernel(a_ref, w_ref, ss_ref, o_ref, *, n_norm):
    tn = o_ref.shape[1]
    sumsq = ss_ref[:, 0:HEAD_DIM]
    for c in range(1, ss_ref.shape[1] // HEAD_DIM):
        sumsq = sumsq + ss_ref[:, c * HEAD_DIM:(c + 1) * HEAD_DIM]
    inv_rms = lax.rsqrt(sumsq * (1.0 / n_norm) + EPS)
    acc = _dot(a_ref[...], w_ref[...]) * jnp.concatenate([inv_rms] * (tn // HEAD_DIM), axis=1)
    o_ref[...] = jnp.square(jnp.maximum(acc, 0.0)).astype(o_ref.dtype)


def _matmul_relu2(a, w, sumsq):
    m, k = a.shape
    n = w.shape[1]
    tm, tn = _mm_tiles(m, n)
    return pl.pallas_call(
        functools.partial(_mm_relu2_kernel, n_norm=k),
        grid=(m // tm, n // tn),
        in_specs=[pl.BlockSpec((tm, k), lambda i, j: (i, 0)), pl.BlockSpec((k, tn), lambda i, j: (0, j)),
                  pl.BlockSpec((tm, sumsq.shape[1]), lambda i, j: (i, 0))],
        out_specs=pl.BlockSpec((tm, tn), lambda i, j: (i, j)),
        out_shape=jax.ShapeDtypeStruct((m, n), BF16),
        compiler_params=_cparams("parallel", "arbitrary"),
        name="matmul_relu2",
    )(a, w, sumsq)


def _mm_res_kernel(a_ref, w_ref, r_ref, g_ref, o_ref, ob_ref, ss_ref):
    h = r_ref[...] + _dot(a_ref[...], w_ref[...])
    o_ref[...] = h
    ob_ref[...] = (h * g_ref[...]).astype(ob_ref.dtype)
    ss_ref[...] = jnp.broadcast_to(jnp.sum(h * h, axis=-1, keepdims=True), ss_ref.shape)


def _matmul_residual(a, w, res, gain):
    m, k = a.shape
    n = w.shape[1]
    tm, tn = _mm_tiles(m, n, 512)
    tile = pl.BlockSpec((tm, tn), lambda i, j: (i, j))
    return pl.pallas_call(
        _mm_res_kernel,
        grid=(m // tm, n // tn),
        in_specs=[pl.BlockSpec((tm, k), lambda i, j: (i, 0)), pl.BlockSpec((k, tn), lambda i, j: (0, j)), tile,
                  pl.BlockSpec((1, tn), lambda i, j: (0, j))],
        out_specs=[tile, tile, pl.BlockSpec((tm, HEAD_DIM), lambda i, j: (i, j))],
        out_shape=[jax.ShapeDtypeStruct((m, n), F32), jax.ShapeDtypeStruct((m, n), BF16),
                   jax.ShapeDtypeStruct((m, (n // tn) * HEAD_DIM), F32)],
        compiler_params=_cparams("parallel", "arbitrary"),
        name="matmul_residual",
    )(a, w, res, gain.reshape(1, n).astype(F32))


def _mm_kacc_kernel(a_ref, w_ref, r_ref, o_ref):
    @pl.when(pl.program_id(2) == 0)
    def _():
        o_ref[...] = r_ref[...]

    o_ref[...] += _dot(a_ref[...], w_ref[...])


def _matmul_residual_ktiled(a, w, res):
    m, k = a.shape
    n = w.shape[1]
    tm = min(m, 1024)
    tn = min(n, 1024)
    tk = min(k, 2048)
    return pl.pallas_call(
        _mm_kacc_kernel,
        grid=(m // tm, n // tn, k // tk),
        in_specs=[pl.BlockSpec((tm, tk), lambda i, j, l: (i, l)), pl.BlockSpec((tk, tn), lambda i, j, l: (l, j)),
                  pl.BlockSpec((tm, tn), lambda i, j, l: (i, j))],
        out_specs=pl.BlockSpec((tm, tn), lambda i, j, l: (i, j)),
        out_shape=jax.ShapeDtypeStruct((m, n), F32),
        compiler_params=_cparams("parallel", "parallel", "arbitrary"),
        name="matmul_residual_ktiled",
    )(a, w, res)


def _merge_kernel(oa_ref, ob_ref, wa_ref, wb_ref, ga_ref, gb_ref, o_ref):
    ya = _dot(oa_ref[...], wa_ref[...])
    yb = _dot(ob_ref[...], wb_ref[...])
    o_ref[...] = (_sigmoid(ga_ref[...]) * ya + _sigmoid(gb_ref[...]) * yb).astype(o_ref.dtype)


def _merge(oa, ob, wa, wb, gates):
    m, k = oa.shape
    n = wa.shape[1]
    tm, tn = _mm_tiles(m, n, 512)
    nb = n // tn
    return pl.pallas_call(
        _merge_kernel,
        grid=(m // tm, nb),
        in_specs=[pl.BlockSpec((tm, k), lambda i, j: (i, 0)), pl.BlockSpec((tm, k), lambda i, j: (i, 0)),
                  pl.BlockSpec((k, tn), lambda i, j: (0, j)), pl.BlockSpec((k, tn), lambda i, j: (0, j)),
                  pl.BlockSpec((tm, tn), lambda i, j: (i, j)), pl.BlockSpec((tm, tn), lambda i, j: (i, j + nb))],
        out_specs=pl.BlockSpec((tm, tn), lambda i, j: (i, j)),
        out_shape=jax.ShapeDtypeStruct((m, n), BF16),
        compiler_params=_cparams("parallel", "arbitrary"),
        name="merge",
    )(oa, ob, wa, wb, gates, gates)


def _gdn_kernel(qkv_ref, z_ref, abc_ref, ar_ref, *rest, chunk, group, n_sub):
    o_ref, act_ref = rest[8], rest[11]
    for sc in range(n_sub):
        rows = slice(sc * chunk, (sc + 1) * chunk)
        views = (qkv_ref.at[:, rows], z_ref.at[:, rows], abc_ref.at[:, rows], ar_ref.at[:, sc:sc + 1])
        _gdn_chunk(sc, *views, *rest[:8], o_ref.at[:, rows], *rest[9:11], act_ref.at[rows], *rest[12:],
                   chunk=chunk, group=group)


def _gdn_chunk(sub, qkv_ref, z_ref, abc_ref, ar_ref, past_ref, s0_ref, cw_ref, alc_ref, dtc_ref, alr_ref, dtr_ref,
               gnw_ref, o_ref, s_ref, xp_ref, act_ref, ltri_ref, ubd_ref, cmask_ref, smask_ref, *, chunk, group):
    c_len, g_sz = chunk, group
    r_len = c_len * g_sz
    n_groups = GDN_V_HEADS // g_sz
    rep = GDN_V_HEADS // GDN_QK_HEADS
    first_of_stream = jnp.logical_and(pl.program_id(1) == 0, sub == 0)

    @pl.when(first_of_stream)
    def _init():
        s_ref[...] = s0_ref[...]
        xp_ref[0:SUBLANES, :] = past_ref[0]
        ri = lax.broadcasted_iota(jnp.int32, (r_len, r_len), 0)
        ci = lax.broadcasted_iota(jnp.int32, (r_len, r_len), 1)
        shift = int(math.log2(c_len))
        same = (ri >> shift) == (ci >> shift)
        cmask_ref[...] = jnp.where(same & (ri >= ci), 1.0, 0.0)
        smask_ref[...] = jnp.where(same & (ri > ci), 1.0, 0.0)
        ubd_ref[...] = jnp.where(same & (ri <= ci), 1.0, 0.0).astype(BF16)
        rc = lax.broadcasted_iota(jnp.int32, (c_len, c_len), 0)
        cc = lax.broadcasted_iota(jnp.int32, (c_len, c_len), 1)
        ltri_ref[...] = jnp.where(rc >= cc, 1.0, 0.0).astype(BF16)

    @pl.when(jnp.logical_not(first_of_stream))
    def _carry():
        xp_ref[0:SUBLANES, :] = xp_ref[c_len:c_len + SUBLANES, :]

    xp_ref[SUBLANES:SUBLANES + c_len, :] = qkv_ref[0]

    for s in range(CONV_CH // HEAD_DIM):
        cols = slice(s * HEAD_DIM, (s + 1) * HEAD_DIM)
        y = cw_ref[0:1, cols] * xp_ref[SUBLANES - 3:SUBLANES - 3 + c_len, cols]
        for i in range(1, CONV_WIDTH):
            y = y + cw_ref[i:i + 1, cols] * xp_ref[SUBLANES - 3 + i:SUBLANES - 3 + i + c_len, cols]
        y = y * _sigmoid(y)
        if s < 2 * GDN_QK_HEADS:
            y = y * lax.rsqrt(jnp.sum(y * y, axis=-1, keepdims=True) + EPS)
            if s < GDN_QK_HEADS:
                y = y * (HEAD_DIM ** -0.5)
        act_ref[:, cols] = y

    ab = abc_ref[0]
    g_col = -jnp.exp(alc_ref[...]) * _softplus(ab + dtc_ref[...])
    beta_col = _sigmoid(ab)
    ltri = ltri_ref[...]
    gc_col = sum(_dot(ltri, t) for t in _split3(g_col))
    g_row = -jnp.exp(alr_ref[...]) * _softplus(ar_ref[0, 0] + dtr_ref[...])
    ubd = ubd_ref[...]
    gc_row = sum(_dot(t, ubd) for t in _split3(g_row))

    cmask = cmask_ref[...]
    smask = smask_ref[...]
    gnw = gnw_ref[...]

    groups = range(n_groups)
    heads_of = [[grp * g_sz + hb for hb in range(g_sz)] for grp in groups]

    def stacked(off, div):
        return [jnp.concatenate([act_ref[:, off + (h // div) * HEAD_DIM:off + (h // div + 1) * HEAD_DIM]
                                 for h in heads_of[grp]], axis=0) for grp in groups]

    kst = stacked(GDN_QK_WIDTH, rep)
    qst = stacked(0, rep)
    vst = stacked(2 * GDN_QK_WIDTH, 1)
    gcb = [jnp.concatenate([jnp.broadcast_to(gc_col[:, h:h + 1], (c_len, HEAD_DIM)) for h in heads_of[grp]], axis=0)
           for grp in groups]
    bcb = [jnp.concatenate([jnp.broadcast_to(beta_col[:, GDN_V_HEADS + h:GDN_V_HEADS + h + 1], (c_len, HEAD_DIM))
                            for h in heads_of[grp]], axis=0) for grp in groups]
    kst_b = [t.astype(BF16) for t in kst]
    gram = [_dot_nt(kst_b[grp], kst_b[grp]) for grp in groups]
    qk = [_dot_nt(qst[grp].astype(BF16), kst_b[grp]) for grp in groups]
    n_rep = r_len // HEAD_DIM
    aqk, tp, xpow = [], [], []
    for grp in groups:
        gcb2 = jnp.concatenate([gcb[grp]] * n_rep, axis=1)
        nbeta = -bcb[grp]
        nbeta2 = jnp.concatenate([nbeta] * n_rep, axis=1) * smask
        decay = jnp.exp(jnp.minimum(gcb2 - gc_row[grp:grp + 1, :], 0.0)) * cmask
        aqk.append((qk[grp] * decay).astype(BF16))
        nm = (gram[grp] * decay) * nbeta2
        tp.append(nm)
        xpow.append(nm)
    n_steps = int(math.log2(c_len))
    for step in range(n_steps):
        for grp in groups:
            xb = xpow[grp].astype(BF16)
            tp[grp] = tp[grp] + _dot(xb, tp[grp].astype(BF16))
            if step + 1 < n_steps:
                xpow[grp] = _dot(xb, xb)
    eg = [jnp.exp(gcb[grp]) for grp in groups]
    sol = []
    for grp in groups:
        rhs = jnp.concatenate([vst[grp] * bcb[grp], kst[grp] * (bcb[grp] * eg[grp])], axis=1)
        sol.append(rhs + _dot(tp[grp].astype(BF16), rhs.astype(BF16)))
    qss = {}
    ust = []
    for grp in groups:
        qg = qst[grp] * eg[grp]
        us = []
        for hb, h in enumerate(heads_of[grp]):
            rows = slice(hb * c_len, (hb + 1) * c_len)
            s_old = s_ref[0, h]
            wq = jnp.concatenate([sol[grp][rows, HEAD_DIM:2 * HEAD_DIM], qg[rows]], axis=0).astype(BF16)
            r1 = _dot(wq, s_old.astype(BF16))
            u = sol[grp][rows, 0:HEAD_DIM] - r1[0:c_len]
            qss[h] = r1[c_len:2 * c_len]
            gl = gcb[grp][(hb + 1) * c_len - 1:(hb + 1) * c_len, :]
            kd = kst[grp][rows] * jnp.exp(gl - gcb[grp][rows])
            s_ref[0, h] = jnp.exp(gl) * s_old + _dot_tn(kd.astype(BF16), u.astype(BF16))
            us.append(u)
        ust.append(jnp.concatenate(us, axis=0).astype(BF16))
    o_intra = [_dot(aqk[grp], ust[grp]) for grp in groups]
    for grp in groups:
        for hb, h in enumerate(heads_of[grp]):
            rows = slice(hb * c_len, (hb + 1) * c_len)
            o = qss[h] + o_intra[grp][rows]
            o = o * lax.rsqrt(jnp.mean(o * o, axis=-1, keepdims=True) + EPS) * gnw
            zz = z_ref[0, :, h * HEAD_DIM:(h + 1) * HEAD_DIM]
            o_ref[0, :, h * HEAD_DIM:(h + 1) * HEAD_DIM] = (o * (zz * _sigmoid(zz))).astype(o_ref.dtype)


def _gdn(proj_a, ab, conv_past, s0, conv_w, a_log, dt_bias, gnw, chunk):
    b, l, _ = proj_a.shape
    n_chunks = l // chunk
    group = V7X_MXU_DIM // chunk
    n_groups = GDN_V_HEADS // group
    r_len = chunk * group
    a_row = ab[:, :, :GDN_V_HEADS].reshape(b, n_chunks, chunk, n_groups, group)
    a_row = a_row.transpose(0, 1, 3, 4, 2).reshape(b, n_chunks, n_groups, r_len)
    past8 = jnp.pad(conv_past.astype(F32), ((0, 0), (SUBLANES - (CONV_WIDTH - 1), 0), (0, 0)))
    pad16 = HEAD_DIM - GDN_V_HEADS
    alc = jnp.pad(a_log.astype(F32), (0, pad16)).reshape(1, HEAD_DIM)
    dtc = jnp.pad(dt_bias.astype(F32), (0, pad16)).reshape(1, HEAD_DIM)
    alr = jnp.repeat(a_log.astype(F32), chunk).reshape(n_groups, r_len)
    dtr = jnp.repeat(dt_bias.astype(F32), chunk).reshape(n_groups, r_len)
    const = lambda *shape: pl.BlockSpec(shape, lambda i, c: (0,) * len(shape))
    n_sub = GDN_CHUNKS_PER_STEP if n_chunks % GDN_CHUNKS_PER_STEP == 0 else 1
    rows = n_sub * chunk
    return pl.pallas_call(
        functools.partial(_gdn_kernel, chunk=chunk, group=group, n_sub=n_sub),
        grid=(b, n_chunks // n_sub),
        in_specs=[
            pl.BlockSpec((1, rows, CONV_CH), lambda i, c: (i, c, 0)),
            pl.BlockSpec((1, rows, GDN_V_WIDTH), lambda i, c: (i, c, CONV_CH // GDN_V_WIDTH)),
            pl.BlockSpec((1, rows, HEAD_DIM), lambda i, c: (i, c, 0)),
            pl.BlockSpec((1, n_sub, n_groups, r_len), lambda i, c: (i, c, 0, 0)),
            pl.BlockSpec((1, SUBLANES, CONV_CH), lambda i, c: (i, 0, 0)),
            pl.BlockSpec((1, GDN_V_HEADS, HEAD_DIM, HEAD_DIM), lambda i, c: (i, 0, 0, 0)),
            const(CONV_WIDTH, CONV_CH), const(1, HEAD_DIM), const(1, HEAD_DIM),
            const(n_groups, r_len), const(n_groups, r_len), const(1, HEAD_DIM),
        ],
        out_specs=[
            pl.BlockSpec((1, rows, GDN_V_WIDTH), lambda i, c: (i, c, 0)),
            pl.BlockSpec((1, GDN_V_HEADS, HEAD_DIM, HEAD_DIM), lambda i, c: (i, 0, 0, 0)),
        ],
        out_shape=[jax.ShapeDtypeStruct((b, l, GDN_V_WIDTH), BF16),
                   jax.ShapeDtypeStruct((b, GDN_V_HEADS, HEAD_DIM, HEAD_DIM), F32)],
        scratch_shapes=[
            pltpu.VMEM((chunk + SUBLANES, CONV_CH), F32),
            pltpu.VMEM((rows, CONV_CH), F32),
            pltpu.VMEM((chunk, chunk), BF16),
            pltpu.VMEM((r_len, r_len), BF16),
            pltpu.VMEM((r_len, r_len), F32),
            pltpu.VMEM((r_len, r_len), F32),
        ],
        compiler_params=_cparams("parallel", "arbitrary"),
        name="gated_delta_rule",
    )(proj_a, proj_a, ab, a_row, past8, s0.astype(F32), conv_w.astype(F32), alc, dtc, alr, dtr,
      gnw.astype(F32).reshape(1, HEAD_DIM))


def _sb_kernel(q_ref, kt_ref, v_ref, o_ref, *, tq, q_pos0):
    i = pl.program_id(2)
    q = q_ref[0]
    qpos0 = q_pos0 + i * tq
    row = lax.broadcasted_iota(jnp.int32, (tq, SB_BLOCK), 0)
    col = lax.broadcasted_iota(jnp.int32, (tq, SB_BLOCK), 1)
    ri = lax.broadcasted_iota(jnp.int32, (SB_BLOCK, SB_BLOCK), 0)
    ci = lax.broadcasted_iota(jnp.int32, (SB_BLOCK, SB_BLOCK), 1)
    ustrict = jnp.where(ri > ci, 1.0, 0.0).astype(BF16)
    n_masked = max(tq // SB_BLOCK, 1)
    j_full = qpos0 // SB_BLOCK

    def tile(j, later, acc, masked):
        kt = kt_ref[0, 0, j]
        vj = v_ref[0, pl.ds(pl.multiple_of(j * SB_BLOCK, SB_BLOCK), SB_BLOCK), :]
        z = _dot(q, kt) * (HEAD_DIM ** -0.5)
        ls = -_softplus(z)
        if masked:
            vis = (j * SB_BLOCK + col) < (qpos0 + row)
            lf = jnp.where(vis, ls, 0.0)
        else:
            lf = ls
        within = _dot(lf.astype(BF16), ustrict)
        p = jnp.exp((z + ls) + within + later)
        if masked:
            p = jnp.where(vis, p, 0.0)
        acc = acc + _dot(p.astype(BF16), vj)
        return later + jnp.sum(lf, axis=-1, keepdims=True), acc

    later = jnp.zeros((tq, 1), F32)
    acc = jnp.zeros((tq, HEAD_DIM), F32)
    for m in range(n_masked):
        later, acc = tile(j_full + (n_masked - 1 - m), later, acc, True)

    def cond(carry):
        return jnp.logical_and(carry[0] >= 0, jnp.max(carry[1]) > -SB_UNDERFLOW)

    def body(carry):
        later, acc = tile(carry[0], carry[1], carry[2], False)
        return carry[0] - 1, later, acc

    _, later, acc = lax.while_loop(cond, body, (j_full - 1, later, acc))
    o_ref[0] = acc.astype(o_ref.dtype)


def _stick_breaking(q, k_bf, v_bf, q_pos0, tq):
    b, lq, _ = q.shape
    lk = k_bf.shape[1]
    nkb = lk // SB_BLOCK
    kt = k_bf.reshape(b, nkb, SB_BLOCK, SB_HEADS, HEAD_DIM).transpose(0, 3, 1, 4, 2)
    return pl.pallas_call(
        functools.partial(_sb_kernel, tq=tq, q_pos0=q_pos0),
        grid=(b, SB_HEADS, lq // tq),
        in_specs=[
            pl.BlockSpec((1, tq, HEAD_DIM), lambda s, h, i: (s, i, h)),
            pl.BlockSpec((1, 1, nkb, HEAD_DIM, SB_BLOCK), lambda s, h, i: (s, h, 0, 0, 0)),
            pl.BlockSpec((1, lk, HEAD_DIM), lambda s, h, i: (s, 0, h)),
        ],
        out_specs=pl.BlockSpec((1, tq, HEAD_DIM), lambda s, h, i: (s, i, h)),
        out_shape=jax.ShapeDtypeStruct((b, lq, SB_WIDTH), BF16),
        compiler_params=_cparams("parallel", "parallel", "arbitrary"),
        name="stick_breaking",
    )(q, kt, v_bf)


def _sb_window_kernel(q_ref, k0_ref, k1_ref, k2_ref, v0_ref, v1_ref, v2_ref, o_ref, need_ref, *, tq, base, older):
    i = pl.program_id(1)
    row = lax.broadcasted_iota(jnp.int32, (tq, SB_WINDOW * SB_BLOCK), 0)
    col = lax.broadcasted_iota(jnp.int32, (tq, SB_WINDOW * SB_BLOCK), 1)
    assert SB_WINDOW == 3 and SB_HEAD_BATCH % 2 == 0
    ri = lax.broadcasted_iota(jnp.int32, (2 * SB_BLOCK, 2 * SB_BLOCK), 0)
    ci = lax.broadcasted_iota(jnp.int32, (2 * SB_BLOCK, 2 * SB_BLOCK), 1)
    same = (ri < SB_BLOCK) == (ci < SB_BLOCK)
    upair = jnp.where(same & (ri > ci), 1.0, 0.0).astype(BF16)
    k_refs = (k0_ref, k1_ref, k2_ref)
    v_refs = (v0_ref, v1_ref, v2_ref)
    n_win = SB_WINDOW * SB_BLOCK
    limit = row
    for w in range(1, SB_WINDOW):
        limit = jnp.where(col < w * SB_BLOCK, limit, jnp.where(base + i - w >= 0, n_win, 0))
    vis = col < limit
    worst = jnp.full((tq, 1), -jnp.inf, F32)
    for h0 in range(0, SB_HEADS, SB_HEAD_BATCH):
        batch = range(h0, h0 + SB_HEAD_BATCH)
        cols = {h: slice(h * HEAD_DIM, (h + 1) * HEAD_DIM) for h in batch}
        z = {h: _dot_nt(q_ref[0, :, cols[h]], jnp.concatenate([r[0, :, cols[h]] for r in k_refs], axis=0))
             * (HEAD_DIM ** -0.5) for h in batch}
        ls = {h: -_softplus(z[h]) for h in batch}
        lf = {h: jnp.where(vis, ls[h], 0.0) for h in batch}
        lf_b = {h: lf[h].astype(BF16) for h in batch}
        within = {h: [] for h in batch}
        for h in batch:
            both = _dot(lf_b[h][:, 0:2 * SB_BLOCK], upair)
            within[h] += [both[:, 0:SB_BLOCK], both[:, SB_BLOCK:2 * SB_BLOCK]]
        for h in range(h0, h0 + SB_HEAD_BATCH, 2):
            both = _dot(jnp.concatenate([lf_b[h][:, 2 * SB_BLOCK:], lf_b[h + 1][:, 2 * SB_BLOCK:]], axis=1), upair)
            within[h].append(both[:, 0:SB_BLOCK])
            within[h + 1].append(both[:, SB_BLOCK:2 * SB_BLOCK])
        p = {}
        for h in batch:
            later = jnp.zeros((tq, 1), F32)
            shifted = []
            for w in range(SB_WINDOW):
                shifted.append(within[h][w] + later)
                later = later + jnp.sum(lf[h][:, w * SB_BLOCK:(w + 1) * SB_BLOCK], axis=-1, keepdims=True)
            worst = jnp.maximum(worst, later)
            p[h] = jnp.where(vis, jnp.exp((z[h] + ls[h]) + jnp.concatenate(shifted, axis=1)), 0.0).astype(BF16)
        for h in batch:
            acc = _dot(p[h], jnp.concatenate([r[0, :, cols[h]] for r in v_refs], axis=0))
            o_ref[0, :, cols[h]] = acc.astype(o_ref.dtype)
    has_older = base + i + older >= SB_WINDOW
    need_ref[...] = jnp.broadcast_to(jnp.where(has_older, jnp.max(worst), -jnp.inf), need_ref.shape)


def _stick_breaking_window(q, k_bf, v_bf, tq, base, older):
    b, lq, _ = q.shape
    nq = lq // tq
    qspec = pl.BlockSpec((1, tq, SB_WIDTH), lambda s, i: (s, i, 0))
    kspecs = [pl.BlockSpec((1, SB_BLOCK, SB_WIDTH), functools.partial(
        lambda s, i, w: (s, jnp.maximum(base + i - w, 0), 0), w=w)) for w in range(SB_WINDOW)]
    return pl.pallas_call(
        functools.partial(_sb_window_kernel, tq=tq, base=base, older=older),
        grid=(b, nq),
        in_specs=[qspec] + kspecs + kspecs,
        out_specs=[qspec, pl.BlockSpec((1, 1, SUBLANES, HEAD_DIM), lambda s, i: (s, i, 0, 0))],
        out_shape=[jax.ShapeDtypeStruct((b, lq, SB_WIDTH), BF16),
                   jax.ShapeDtypeStruct((b, nq, SUBLANES, HEAD_DIM), F32)],
        compiler_params=_cparams("parallel", "arbitrary"),
        name="stick_breaking_window",
    )(q, k_bf, k_bf, k_bf, v_bf, v_bf, v_bf)


def _layer(x, conv_past, s0, past_k, past_v, chunk, wts):
    b, l, d = x.shape
    m = b * l
    x2 = x.reshape(m, d)
    xn = _rmsnorm(x2, wts["norm1_w"], BF16)
    w_in = wts["w_in"]
    c_q = OFF_A
    c_g = c_q + 3 * SB_WIDTH
    c_ab = c_g + 2 * d
    (ab,) = _matmul(xn, wts["w_ab"], (F32,))
    if m <= 256:
        (proj,) = _matmul(xn, w_in, (F32,), 0, c_ab)
        proj_a, q_sb, gates = proj[:, :OFF_A], proj[:, c_q:c_q + SB_WIDTH].astype(BF16), proj[:, c_g:]
        k_sb, v_sb = proj[:, c_q + SB_WIDTH:c_q + 2 * SB_WIDTH], proj[:, c_q + 2 * SB_WIDTH:c_g]
        k_bf, v_bf = k_sb.astype(BF16), v_sb.astype(BF16)
    else:
        (proj_a,) = _matmul(xn, w_in, (F32,), 0, OFF_A)
        (gates,) = _matmul(xn, w_in, (F32,), c_g, 2 * d)
        v_sb, v_bf = _matmul(xn, w_in, (F32, BF16), c_q + 2 * SB_WIDTH, SB_WIDTH)
        k_sb, k_bf = _matmul(xn, w_in, (F32, BF16), c_q + SB_WIDTH, SB_WIDTH)
        (q_sb,) = _matmul(xn, w_in, (BF16,), c_q, SB_WIDTH)

    o_a, s_new = _gdn(proj_a.reshape(b, l, -1), ab.reshape(b, l, -1), conv_past, s0, wts["conv_w"], wts["A_log"],
                      wts["dt_bias"], wts["gdn_norm_w"], chunk)
    conv_state = proj_a.reshape(b, l, -1)[:, l - (CONV_WIDTH - 1):, :CONV_CH]

    k_bf = k_bf.reshape(b, l, SB_WIDTH)
    v_bf = v_bf.reshape(b, l, SB_WIDTH)
    q3 = q_sb.reshape(b, l, SB_WIDTH)
    if past_k is None:
        o_win, need = _stick_breaking_window(q3, k_bf, v_bf, min(l, SB_BLOCK), 0, 0)

        def full_sweep():
            return _stick_breaking(q3, k_bf, v_bf, 0, min(l, 256))
    else:
        p = past_k.shape[1]
        n_cached = (SB_WINDOW - 1) * SB_BLOCK
        assert p % SB_BLOCK == 0 and p >= n_cached and l <= SB_BLOCK

        def with_cache(new, past, keep):
            return jnp.concatenate([past[:, p - keep:].reshape(b, keep, SB_WIDTH).astype(BF16), new,
                                    jnp.zeros((b, SB_BLOCK - l, SB_WIDTH), BF16)], axis=1)

        o_win, need = _stick_breaking_window(q3, with_cache(k_bf, past_k, n_cached), with_cache(v_bf, past_v, n_cached),
                                             l, SB_WINDOW - 1, (p - n_cached) // SB_BLOCK)

        def full_sweep():
            return _stick_breaking(q3, with_cache(k_bf, past_k, p), with_cache(v_bf, past_v, p), p, l)
    o_b = lax.cond(jnp.max(need) > -SB_UNDERFLOW, full_sweep, lambda: o_win)

    merged = _merge(o_a.reshape(m, GDN_V_WIDTH), o_b.reshape(m, SB_WIDTH), wts["w_gdn_o"], wts["w_sb_o"], gates)
    h, h_gain, sumsq = _matmul_residual(merged, wts["w_out"], x2, wts["norm2_w"])
    hid = _matmul_relu2(h_gain, wts["w_up"], sumsq)
    h = _matmul_residual_ktiled(hid, wts["w_down"], h)
    return (h, conv_state, s_new, k_sb.reshape(b, l, SB_HEADS, HEAD_DIM), v_sb.reshape(b, l, SB_HEADS, HEAD_DIM))


def kernel(x_prompt, x_sample, cache_sb_k, cache_sb_v, state_gdn_S, state_gdn_conv, norm1_w, w_in, conv_w, A_log,
           dt_bias, gdn_norm_w, w_gdn_o, w_sb_o, w_out, norm2_w, w_up, w_down, final_norm_w):
    depth = w_in.shape[0]
    assert depth == 1
    w_in_t = jnp.swapaxes(w_in[0], 0, 1)
    wts = {
        "norm1_w": norm1_w[0], "norm2_w": norm2_w[0], "conv_w": conv_w[0], "A_log": A_log[0], "dt_bias": dt_bias[0],
        "gdn_norm_w": gdn_norm_w[0],
        "w_in": _repack_w_in(w_in_t), "w_ab": _ab_cols(w_in_t),
        "w_gdn_o": w_gdn_o[0].astype(BF16), "w_sb_o": w_sb_o[0].astype(BF16), "w_out": w_out[0].astype(BF16),
        "w_up": w_up[0].astype(BF16), "w_down": w_down[0].astype(BF16),
    }
    bp, lp, d = x_prompt.shape
    bs, ls, _ = x_sample.shape
    conv0 = jnp.zeros((bp, CONV_WIDTH - 1, CONV_CH), F32)
    s_zero = jnp.zeros((bp, GDN_V_HEADS, HEAD_DIM, HEAD_DIM), F32)
    hp, c_p, s_p, k_p, v_p = _layer(x_prompt, conv0, s_zero, None, None, min(PROMPT_CHUNK, lp), wts)
    hs, c_s, s_s, k_s, v_s = _layer(x_sample, state_gdn_conv[0], state_gdn_S[0], cache_sb_k[0], cache_sb_v[0], ls, wts)
    y_prompt = _rmsnorm(hp, final_norm_w, F32).reshape(bp, lp, d)
    y_sample = _rmsnorm(hs, final_norm_w, F32).reshape(bs, ls, d)
    return (y_prompt, y_sample, k_p[None], v_p[None], s_p[None], c_p[None],
            k_s[None], v_s[None], s_s[None].astype(state_gdn_S.dtype), c_s[None])
```

```python
import functools
import math

import jax
import jax.numpy as jnp
from jax import lax
from jax.experimental import pallas as pl
from jax.experimental.pallas import tpu as pltpu

F32 = jnp.float32
BF16 = jnp.bfloat16

EPS = 1e-6
HEAD_DIM = 128
GDN_QK_HEADS = 8
GDN_V_HEADS = 16
GDN_QK_WIDTH = GDN_QK_HEADS * HEAD_DIM
GDN_V_WIDTH = GDN_V_HEADS * HEAD_DIM
CONV_WIDTH = 4
CONV_CH = 2 * GDN_QK_WIDTH + GDN_V_WIDTH
SB_HEADS = 16
SB_WIDTH = SB_HEADS * HEAD_DIM
SB_BLOCK = 128
SB_UNDERFLOW = 106.0
SB_WINDOW = 3
SB_HEAD_BATCH = 4
PROMPT_CHUNK = 64
GDN_CHUNKS_PER_STEP = 2

V7X_MXU_DIM = 256
V7X_VMEM_LIMIT_BYTES = 56 * 1024 * 1024
SUBLANES = 8

OFF_Z = CONV_CH
OFF_A = OFF_Z + GDN_V_WIDTH
OFF_B = OFF_A + GDN_V_HEADS
OFF_SB = OFF_B + GDN_V_HEADS
OFF_GATE = OFF_SB + 3 * SB_WIDTH


def _cparams(*sem):
    return pltpu.CompilerParams(dimension_semantics=sem, vmem_limit_bytes=V7X_VMEM_LIMIT_BYTES)


def _dot(a, b):
    return jnp.dot(a, b, preferred_element_type=F32)


def _dot_nt(a, b):
    return lax.dot_general(a, b, (((1,), (1,)), ((), ())), preferred_element_type=F32)


def _dot_tn(a, b):
    return lax.dot_general(a, b, (((0,), (0,)), ((), ())), preferred_element_type=F32)


def _sigmoid(x):
    return 1.0 / (1.0 + jnp.exp(-x))


def _softplus(x):
    return jnp.maximum(x, 0.0) + jnp.log(1.0 + jnp.exp(-jnp.abs(x)))


def _split3(x):
    x1 = x.astype(BF16)
    r1 = x - x1.astype(F32)
    x2 = r1.astype(BF16)
    x3 = (r1 - x2.astype(F32)).astype(BF16)
    return x1, x2, x3


def _rmsnorm_kernel(x_ref, w_ref, o_ref):
    x = x_ref[...]
    var = jnp.mean(x * x, axis=-1, keepdims=True)
    o_ref[...] = (x * lax.rsqrt(var + EPS) * w_ref[...]).astype(o_ref.dtype)


def _rmsnorm(x, w, out_dtype):
    m, d = x.shape
    tm = min(m, 256)
    return pl.pallas_call(
        _rmsnorm_kernel,
        grid=(m // tm,),
        in_specs=[pl.BlockSpec((tm, d), lambda i: (i, 0)), pl.BlockSpec((1, d), lambda i: (0, 0))],
        out_specs=pl.BlockSpec((tm, d), lambda i: (i, 0)),
        out_shape=jax.ShapeDtypeStruct((m, d), out_dtype),
        compiler_params=_cparams("parallel"),
        name="rmsnorm",
    )(x, w.reshape(1, d).astype(F32))


def _repack_kernel(a_ref, b_ref, o_ref, *, first_shifted, shift):
    j = pl.program_id(0)

    @pl.when(j < first_shifted)
    def _():
        o_ref[...] = a_ref[...].T.astype(o_ref.dtype)

    @pl.when(j >= first_shifted)
    def _():
        src = jnp.concatenate([a_ref[shift:, :], b_ref[...]], axis=0)
        o_ref[...] = src.T.astype(o_ref.dtype)


def _repack_w_in(wt):
    n_src, k = wt.shape
    tn, tk = 1024, 1024
    shift = OFF_SB - OFF_A
    n_out = n_src - shift
    assert OFF_A % tn == 0 and n_out % tn == 0 and k % tk == 0 and tn % shift == 0 and shift % SUBLANES == 0
    return pl.pallas_call(
        functools.partial(_repack_kernel, first_shifted=OFF_A // tn, shift=shift),
        grid=(n_out // tn, k // tk),
        in_specs=[pl.BlockSpec((tn, tk), lambda j, i: (j, i)),
                  pl.BlockSpec((shift, tk), lambda j, i: ((j + 1) * (tn // shift), i))],
        out_specs=pl.BlockSpec((tk, tn), lambda j, i: (i, j)),
        out_shape=jax.ShapeDtypeStruct((k, n_out), BF16),
        compiler_params=_cparams("parallel", "arbitrary"),
        name="repack_w_in",
    )(wt, wt)


def _ab_cols_kernel(a_ref, o_ref, *, n_valid):
    t = a_ref[...].T
    lane = lax.broadcasted_iota(jnp.int32, t.shape, 1)
    o_ref[...] = jnp.where(lane < n_valid, t, 0.0).astype(o_ref.dtype)


def _ab_cols(wt):
    k = wt.shape[1]
    tk = 512
    assert OFF_A % HEAD_DIM == 0 and k % tk == 0
    return pl.pallas_call(
        functools.partial(_ab_cols_kernel, n_valid=OFF_SB - OFF_A),
        grid=(k // tk,),
        in_specs=[pl.BlockSpec((HEAD_DIM, tk), lambda i: (OFF_A // HEAD_DIM, i))],
        out_specs=pl.BlockSpec((tk, HEAD_DIM), lambda i: (i, 0)),
        out_shape=jax.ShapeDtypeStruct((k, HEAD_DIM), BF16),
        compiler_params=_cparams("arbitrary"),
        name="ab_cols",
    )(wt)


def _mm_tiles(m, n, tn_max=1024):
    tm = min(m, 1024)
    tn = min(n, tn_max)
    assert m % tm == 0 and n % tn == 0, (m, n)
    return tm, tn


def _mm_kernel(a_ref, w_ref, *o_refs):
    acc = _dot(a_ref[...], w_ref[...])
    for o_ref in o_refs:
        o_ref[...] = acc.astype(o_ref.dtype)


def _matmul(a, w, out_dtypes, col0=0, n=None):
    m, k = a.shape
    n = w.shape[1] if n is None else n
    tm, tn = _mm_tiles(m, n)
    assert col0 % tn == 0
    jb = col0 // tn
    outs = pl.pallas_call(
        _mm_kernel,
        grid=(m // tm, n // tn),
        in_specs=[pl.BlockSpec((tm, k), lambda i, j: (i, 0)), pl.BlockSpec((k, tn), lambda i, j: (0, j + jb))],
        out_specs=[pl.BlockSpec((tm, tn), lambda i, j: (i, j)) for _ in out_dtypes],
        out_shape=[jax.ShapeDtypeStruct((m, n), dt) for dt in out_dtypes],
        compiler_params=_cparams("parallel", "arbitrary"),
        name="matmul",
    )(a, w)
    return outs


def _mm_relu2_kernel(a_ref, w_ref, ss_ref, o_ref, *, n_norm):
    tn = o_ref.shape[1]
    sumsq = ss_ref[:, 0:HEAD_DIM]
    for c in range(1, ss_ref.shape[1] // HEAD_DIM):
        sumsq = sumsq + ss_ref[:, c * HEAD_DIM:(c + 1) * HEAD_DIM]
    inv_rms = lax.rsqrt(sumsq * (1.0 / n_norm) + EPS)
    acc = _dot(a_ref[...], w_ref[...]) * jnp.concatenate([inv_rms] * (tn // HEAD_DIM), axis=1)
    o_ref[...] = jnp.square(jnp.maximum(acc, 0.0)).astype(o_ref.dtype)


def _matmul_relu2(a, w, sumsq):
    m, k = a.shape
    n = w.shape[1]
    tm, tn = _mm_tiles(m, n)
    return pl.pallas_call(
        functools.partial(_mm_relu2_kernel, n_norm=k),
        grid=(m // tm, n // tn),
        in_specs=[pl.BlockSpec((tm, k), lambda i, j: (i, 0)), pl.BlockSpec((k, tn), lambda i, j: (0, j)),
                  pl.BlockSpec((tm, sumsq.shape[1]), lambda i, j: (i, 0))],
        out_specs=pl.BlockSpec((tm, tn), lambda i, j: (i, j)),
        out_shape=jax.ShapeDtypeStruct((m, n), BF16),
        compiler_params=_cparams("parallel", "arbitrary"),
        name="matmul_relu2",
    )(a, w, sumsq)


def _mm_res_kernel(a_ref, w_ref, r_ref, g_ref, o_ref, ob_ref, ss_ref):
    h = r_ref[...] + _dot(a_ref[...], w_ref[...])
    o_ref[...] = h
    ob_ref[...] = (h * g_ref[...]).astype(ob_ref.dtype)
    ss_ref[...] = jnp.broadcast_to(jnp.sum(h * h, axis=-1, keepdims=True), ss_ref.shape)


def _matmul_residual(a, w, res, gain):
    m, k = a.shape
    n = w.shape[1]
    tm, tn = _mm_tiles(m, n, 512)
    tile = pl.BlockSpec((tm, tn), lambda i, j: (i, j))
    return pl.pallas_call(
        _mm_res_kernel,
        grid=(m // tm, n // tn),
        in_specs=[pl.BlockSpec((tm, k), lambda i, j: (i, 0)), pl.BlockSpec((k, tn), lambda i, j: (0, j)), tile,
                  pl.BlockSpec((1, tn), lambda i, j: (0, j))],
        out_specs=[tile, tile, pl.BlockSpec((tm, HEAD_DIM), lambda i, j: (i, j))],
        out_shape=[jax.ShapeDtypeStruct((m, n), F32), jax.ShapeDtypeStruct((m, n), BF16),
                   jax.ShapeDtypeStruct((m, (n // tn) * HEAD_DIM), F32)],
        compiler_params=_cparams("parallel", "arbitrary"),
        name="matmul_residual",
    )(a, w, res, gain.reshape(1, n).astype(F32))


def _mm_kacc_kernel(a_ref, w_ref, r_ref, o_ref):
    @pl.when(pl.program_id(2) == 0)
    def _():
        o_ref[...] = r_ref[...]

    o_ref[...] += _dot(a_ref[...], w_ref[...])


def _matmul_residual_ktiled(a, w, res):
    m, k = a.shape
    n = w.shape[1]
    tm = min(m, 1024)
    tn = min(n, 1024)
    tk = min(k, 2048)
    return pl.pallas_call(
        _mm_kacc_kernel,
        grid=(m // tm, n // tn, k // tk),
        in_specs=[pl.BlockSpec((tm, tk), lambda i, j, l: (i, l)), pl.BlockSpec((tk, tn), lambda i, j, l: (l, j)),
                  pl.BlockSpec((tm, tn), lambda i, j, l: (i, j))],
        out_specs=pl.BlockSpec((tm, tn), lambda i, j, l: (i, j)),
        out_shape=jax.ShapeDtypeStruct((m, n), F32),
        compiler_params=_cparams("parallel", "parallel", "arbitrary"),
        name="matmul_residual_ktiled",
    )(a, w, res)


def _merge_kernel(oa_ref, ob_ref, wa_ref, wb_ref, ga_ref, gb_ref, o_ref):
    ya = _dot(oa_ref[...], wa_ref[...])
    yb = _dot(ob_ref[...], wb_ref[...])
    o_ref[...] = (_sigmoid(ga_ref[...]) * ya + _sigmoid(gb_ref[...]) * yb).astype(o_ref.dtype)


def _merge(oa, ob, wa, wb, gates):
    m, k = oa.shape
    n = wa.shape[1]
    tm, tn = _mm_tiles(m, n)
    nb = n // tn
    once = lambda: pl.BlockSpec((tm, k), lambda i, j: (i, 0), pipeline_mode=pl.Buffered(1))
    return pl.pallas_call(
        _merge_kernel,
        grid=(m // tm, nb),
        in_specs=[once(), once(),
                  pl.BlockSpec((k, tn), lambda i, j: (0, j)), pl.BlockSpec((k, tn), lambda i, j: (0, j)),
                  pl.BlockSpec((tm, tn), lambda i, j: (i, j)), pl.BlockSpec((tm, tn), lambda i, j: (i, j + nb))],
        out_specs=pl.BlockSpec((tm, tn), lambda i, j: (i, j)),
        out_shape=jax.ShapeDtypeStruct((m, n), BF16),
        compiler_params=_cparams("parallel", "arbitrary"),
        name="merge",
    )(oa, ob, wa, wb, gates, gates)


def _gdn_kernel(qkv_ref, z_ref, abc_ref, ar_ref, *rest, chunk, group, n_sub):
    o_ref, act_ref = rest[8], rest[11]
    for sc in range(n_sub):
        rows = slice(sc * chunk, (sc + 1) * chunk)
        views = (qkv_ref.at[:, rows], z_ref.at[:, rows], abc_ref.at[:, rows], ar_ref.at[:, sc:sc + 1])
        _gdn_chunk(sc, *views, *rest[:8], o_ref.at[:, rows], *rest[9:11], act_ref.at[rows], *rest[12:],
                   chunk=chunk, group=group)


def _gdn_chunk(sub, qkv_ref, z_ref, abc_ref, ar_ref, past_ref, s0_ref, cw_ref, alc_ref, dtc_ref, alr_ref, dtr_ref,
               gnw_ref, o_ref, s_ref, xp_ref, act_ref, ltri_ref, ubd_ref, cmask_ref, smask_ref, *, chunk, group):
    c_len, g_sz = chunk, group
    r_len = c_len * g_sz
    n_groups = GDN_V_HEADS // g_sz
    rep = GDN_V_HEADS // GDN_QK_HEADS
    first_of_stream = jnp.logical_and(pl.program_id(1) == 0, sub == 0)

    @pl.when(first_of_stream)
    def _init():
        s_ref[...] = s0_ref[...]
        xp_ref[0:SUBLANES, :] = past_ref[0]
        ri = lax.broadcasted_iota(jnp.int32, (r_len, r_len), 0)
        ci = lax.broadcasted_iota(jnp.int32, (r_len, r_len), 1)
        shift = int(math.log2(c_len))
        same = (ri >> shift) == (ci >> shift)
        cmask_ref[...] = jnp.where(same & (ri >= ci), 1.0, 0.0)
        smask_ref[...] = jnp.where(same & (ri > ci), 1.0, 0.0)
        ubd_ref[...] = jnp.where(same & (ri <= ci), 1.0, 0.0).astype(BF16)
        rc = lax.broadcasted_iota(jnp.int32, (c_len, c_len), 0)
        cc = lax.broadcasted_iota(jnp.int32, (c_len, c_len), 1)
        ltri_ref[...] = jnp.where(rc >= cc, 1.0, 0.0).astype(BF16)

    @pl.when(jnp.logical_not(first_of_stream))
    def _carry():
        xp_ref[0:SUBLANES, :] = xp_ref[c_len:c_len + SUBLANES, :]

    xp_ref[SUBLANES:SUBLANES + c_len, :] = qkv_ref[0]

    for s in range(CONV_CH // HEAD_DIM):
        cols = slice(s * HEAD_DIM, (s + 1) * HEAD_DIM)
        y = cw_ref[0:1, cols] * xp_ref[SUBLANES - 3:SUBLANES - 3 + c_len, cols]
        for i in range(1, CONV_WIDTH):
            y = y + cw_ref[i:i + 1, cols] * xp_ref[SUBLANES - 3 + i:SUBLANES - 3 + i + c_len, cols]
        y = y * _sigmoid(y)
        if s < 2 * GDN_QK_HEADS:
            y = y * lax.rsqrt(jnp.sum(y * y, axis=-1, keepdims=True) + EPS)
            if s < GDN_QK_HEADS:
                y = y * (HEAD_DIM ** -0.5)
        act_ref[:, cols] = y

    ab = abc_ref[0]
    g_col = -jnp.exp(alc_ref[...]) * _softplus(ab + dtc_ref[...])
    beta_col = _sigmoid(ab)
    ltri = ltri_ref[...]
    gc_col = sum(_dot(ltri, t) for t in _split3(g_col))
    g_row = -jnp.exp(alr_ref[...]) * _softplus(ar_ref[0, 0] + dtr_ref[...])
    ubd = ubd_ref[...]
    gc_row = sum(_dot(t, ubd) for t in _split3(g_row))

    cmask = cmask_ref[...]
    smask = smask_ref[...]
    gnw = gnw_ref[...]

    groups = range(n_groups)
    heads_of = [[grp * g_sz + hb for hb in range(g_sz)] for grp in groups]

    def stacked(off, div):
        return [jnp.concatenate([act_ref[:, off + (h // div) * HEAD_DIM:off + (h // div + 1) * HEAD_DIM]
                                 for h in heads_of[grp]], axis=0) for grp in groups]

    kst = stacked(GDN_QK_WIDTH, rep)
    qst = stacked(0, rep)
    vst = stacked(2 * GDN_QK_WIDTH, 1)
    gcb = [jnp.concatenate([jnp.broadcast_to(gc_col[:, h:h + 1], (c_len, HEAD_DIM)) for h in heads_of[grp]], axis=0)
           for grp in groups]
    bcb = [jnp.concatenate([jnp.broadcast_to(beta_col[:, GDN_V_HEADS + h:GDN_V_HEADS + h + 1], (c_len, HEAD_DIM))
                            for h in heads_of[grp]], axis=0) for grp in groups]
    kst_b = [t.astype(BF16) for t in kst]
    gram = [_dot_nt(kst_b[grp], kst_b[grp]) for grp in groups]
    qk = [_dot_nt(qst[grp].astype(BF16), kst_b[grp]) for grp in groups]
    n_rep = r_len // HEAD_DIM
    aqk, tp, xpow = [], [], []
    for grp in groups:
        gcb2 = jnp.concatenate([gcb[grp]] * n_rep, axis=1)
        nbeta = -bcb[grp]
        nbeta2 = jnp.concatenate([nbeta] * n_rep, axis=1) * smask
        decay = jnp.exp(jnp.minimum(gcb2 - gc_row[grp:grp + 1, :], 0.0)) * cmask
        aqk.append((qk[grp] * decay).astype(BF16))
        nm = (gram[grp] * decay) * nbeta2
        tp.append(nm)
        xpow.append(nm)
    n_steps = int(math.log2(c_len))
    for step in range(n_steps):
        for grp in groups:
            xb = xpow[grp].astype(BF16)
            tp[grp] = tp[grp] + _dot(xb, tp[grp].astype(BF16))
            if step + 1 < n_steps:
                xpow[grp] = _dot(xb, xb)
    eg = [jnp.exp(gcb[grp]) for grp in groups]
    sol = []
    for grp in groups:
        rhs = jnp.concatenate([vst[grp] * bcb[grp], kst[grp] * (bcb[grp] * eg[grp])], axis=1)
        sol.append(rhs + _dot(tp[grp].astype(BF16), rhs.astype(BF16)))
    qss = {}
    ust = []
    for grp in groups:
        qg = qst[grp] * eg[grp]
        us = []
        for hb, h in enumerate(heads_of[grp]):
            rows = slice(hb * c_len, (hb + 1) * c_len)
            s_old = s_ref[0, h]
            wq = jnp.concatenate([sol[grp][rows, HEAD_DIM:2 * HEAD_DIM], qg[rows]], axis=0).astype(BF16)
            r1 = _dot(wq, s_old.astype(BF16))
            u = sol[grp][rows, 0:HEAD_DIM] - r1[0:c_len]
            qss[h] = r1[c_len:2 * c_len]
            gl = gcb[grp][(hb + 1) * c_len - 1:(hb + 1) * c_len, :]
            kd = kst[grp][rows] * jnp.exp(gl - gcb[grp][rows])
            s_ref[0, h] = jnp.exp(gl) * s_old + _dot_tn(kd.astype(BF16), u.astype(BF16))
            us.append(u)
        ust.append(jnp.concatenate(us, axis=0).astype(BF16))
    o_intra = [_dot(aqk[grp], ust[grp]) for grp in groups]
    for grp in groups:
        for hb, h in enumerate(heads_of[grp]):
            rows = slice(hb * c_len, (hb + 1) * c_len)
            o = qss[h] + o_intra[grp][rows]
            o = o * lax.rsqrt(jnp.mean(o * o, axis=-1, keepdims=True) + EPS) * gnw
            zz = z_ref[0, :, h * HEAD_DIM:(h + 1) * HEAD_DIM]
            o_ref[0, :, h * HEAD_DIM:(h + 1) * HEAD_DIM] = (o * (zz * _sigmoid(zz))).astype(o_ref.dtype)


def _gdn(proj_a, ab, conv_past, s0, conv_w, a_log, dt_bias, gnw, chunk):
    b, l, _ = proj_a.shape
    n_chunks = l // chunk
    group = V7X_MXU_DIM // chunk
    n_groups = GDN_V_HEADS // group
    r_len = chunk * group
    a_row = ab[:, :, :GDN_V_HEADS].reshape(b, n_chunks, chunk, n_groups, group)
    a_row = a_row.transpose(0, 1, 3, 4, 2).reshape(b, n_chunks, n_groups, r_len)
    past8 = jnp.pad(conv_past.astype(F32), ((0, 0), (SUBLANES - (CONV_WIDTH - 1), 0), (0, 0)))
    pad16 = HEAD_DIM - GDN_V_HEADS
    alc = jnp.pad(a_log.astype(F32), (0, pad16)).reshape(1, HEAD_DIM)
    dtc = jnp.pad(dt_bias.astype(F32), (0, pad16)).reshape(1, HEAD_DIM)
    alr = jnp.repeat(a_log.astype(F32), chunk).reshape(n_groups, r_len)
    dtr = jnp.repeat(dt_bias.astype(F32), chunk).reshape(n_groups, r_len)
    const = lambda *shape: pl.BlockSpec(shape, lambda i, c: (0,) * len(shape))
    n_sub = GDN_CHUNKS_PER_STEP if n_chunks % GDN_CHUNKS_PER_STEP == 0 else 1
    rows = n_sub * chunk
    return pl.pallas_call(
        functools.partial(_gdn_kernel, chunk=chunk, group=group, n_sub=n_sub),
        grid=(b, n_chunks // n_sub),
        in_specs=[
            pl.BlockSpec((1, rows, CONV_CH), lambda i, c: (i, c, 0)),
            pl.BlockSpec((1, rows, GDN_V_WIDTH), lambda i, c: (i, c, CONV_CH // GDN_V_WIDTH)),
            pl.BlockSpec((1, rows, HEAD_DIM), lambda i, c: (i, c, 0)),
            pl.BlockSpec((1, n_sub, n_groups, r_len), lambda i, c: (i, c, 0, 0)),
            pl.BlockSpec((1, SUBLANES, CONV_CH), lambda i, c: (i, 0, 0)),
            pl.BlockSpec((1, GDN_V_HEADS, HEAD_DIM, HEAD_DIM), lambda i, c: (i, 0, 0, 0)),
            const(CONV_WIDTH, CONV_CH), const(1, HEAD_DIM), const(1, HEAD_DIM),
            const(n_groups, r_len), const(n_groups, r_len), const(1, HEAD_DIM),
        ],
        out_specs=[
            pl.BlockSpec((1, rows, GDN_V_WIDTH), lambda i, c: (i, c, 0)),
            pl.BlockSpec((1, GDN_V_HEADS, HEAD_DIM, HEAD_DIM), lambda i, c: (i, 0, 0, 0)),
        ],
        out_shape=[jax.ShapeDtypeStruct((b, l, GDN_V_WIDTH), BF16),
                   jax.ShapeDtypeStruct((b, GDN_V_HEADS, HEAD_DIM, HEAD_DIM), F32)],
        scratch_shapes=[
            pltpu.VMEM((chunk + SUBLANES, CONV_CH), F32),
            pltpu.VMEM((rows, CONV_CH), F32),
            pltpu.VMEM((chunk, chunk), BF16),
            pltpu.VMEM((r_len, r_len), BF16),
            pltpu.VMEM((r_len, r_len), F32),
            pltpu.VMEM((r_len, r_len), F32),
        ],
        compiler_params=_cparams("parallel", "arbitrary"),
        name="gated_delta_rule",
    )(proj_a, proj_a, ab, a_row, past8, s0.astype(F32), conv_w.astype(F32), alc, dtc, alr, dtr,
      gnw.astype(F32).reshape(1, HEAD_DIM))


def _sb_kernel(q_ref, kt_ref, v_ref, o_ref, *, tq, q_pos0):
    i = pl.program_id(2)
    q = q_ref[0]
    qpos0 = q_pos0 + i * tq
    row = lax.broadcasted_iota(jnp.int32, (tq, SB_BLOCK), 0)
    col = lax.broadcasted_iota(jnp.int32, (tq, SB_BLOCK), 1)
    ri = lax.broadcasted_iota(jnp.int32, (SB_BLOCK, SB_BLOCK), 0)
    ci = lax.broadcasted_iota(jnp.int32, (SB_BLOCK, SB_BLOCK), 1)
    ustrict = jnp.where(ri > ci, 1.0, 0.0).astype(BF16)
    n_masked = max(tq // SB_BLOCK, 1)
    j_full = qpos0 // SB_BLOCK

    def tile(j, later, acc, masked):
        kt = kt_ref[0, 0, j]
        vj = v_ref[0, pl.ds(pl.multiple_of(j * SB_BLOCK, SB_BLOCK), SB_BLOCK), :]
        z = _dot(q, kt) * (HEAD_DIM ** -0.5)
        ls = -_softplus(z)
        if masked:
            vis = (j * SB_BLOCK + col) < (qpos0 + row)
            lf = jnp.where(vis, ls, 0.0)
        else:
            lf = ls
        within = _dot(lf.astype(BF16), ustrict)
        p = jnp.exp((z + ls) + within + later)
        if masked:
            p = jnp.where(vis, p, 0.0)
        acc = acc + _dot(p.astype(BF16), vj)
        return later + jnp.sum(lf, axis=-1, keepdims=True), acc

    later = jnp.zeros((tq, 1), F32)
    acc = jnp.zeros((tq, HEAD_DIM), F32)
    for m in range(n_masked):
        later, acc = tile(j_full + (n_masked - 1 - m), later, acc, True)

    def cond(carry):
        return jnp.logical_and(carry[0] >= 0, jnp.max(carry[1]) > -SB_UNDERFLOW)

    def body(carry):
        later, acc = tile(carry[0], carry[1], carry[2], False)
        return carry[0] - 1, later, acc

    _, later, acc = lax.while_loop(cond, body, (j_full - 1, later, acc))
    o_ref[0] = acc.astype(o_ref.dtype)


def _stick_breaking(q, k_bf, v_bf, q_pos0, tq):
    b, lq, _ = q.shape
    lk = k_bf.shape[1]
    nkb = lk // SB_BLOCK
    kt = k_bf.reshape(b, nkb, SB_BLOCK, SB_HEADS, HEAD_DIM).transpose(0, 3, 1, 4, 2)
    return pl.pallas_call(
        functools.partial(_sb_kernel, tq=tq, q_pos0=q_pos0),
        grid=(b, SB_HEADS, lq // tq),
        in_specs=[
            pl.BlockSpec((1, tq, HEAD_DIM), lambda s, h, i: (s, i, h)),
            pl.BlockSpec((1, 1, nkb, HEAD_DIM, SB_BLOCK), lambda s, h, i: (s, h, 0, 0, 0)),
            pl.BlockSpec((1, lk, HEAD_DIM), lambda s, h, i: (s, 0, h)),
        ],
        out_specs=pl.BlockSpec((1, tq, HEAD_DIM), lambda s, h, i: (s, i, h)),
        out_shape=jax.ShapeDtypeStruct((b, lq, SB_WIDTH), BF16),
        compiler_params=_cparams("parallel", "parallel", "arbitrary"),
        name="stick_breaking",
    )(q, kt, v_bf)


def _sb_window_kernel(q_ref, k0_ref, k1_ref, k2_ref, v0_ref, v1_ref, v2_ref, o_ref, need_ref, *, tq, base, older):
    i = pl.program_id(1)
    row = lax.broadcasted_iota(jnp.int32, (tq, SB_WINDOW * SB_BLOCK), 0)
    col = lax.broadcasted_iota(jnp.int32, (tq, SB_WINDOW * SB_BLOCK), 1)
    assert SB_WINDOW == 3 and SB_HEAD_BATCH % 2 == 0
    ri = lax.broadcasted_iota(jnp.int32, (2 * SB_BLOCK, 2 * SB_BLOCK), 0)
    ci = lax.broadcasted_iota(jnp.int32, (2 * SB_BLOCK, 2 * SB_BLOCK), 1)
    same = (ri < SB_BLOCK) == (ci < SB_BLOCK)
    upair = jnp.where(same & (ri > ci), 1.0, 0.0).astype(BF16)
    k_refs = (k0_ref, k1_ref, k2_ref)
    v_refs = (v0_ref, v1_ref, v2_ref)
    n_win = SB_WINDOW * SB_BLOCK
    limit = row
    for w in range(1, SB_WINDOW):
        limit = jnp.where(col < w * SB_BLOCK, limit, jnp.where(base + i - w >= 0, n_win, 0))
    vis = col < limit
    worst = jnp.full((tq, 1), -jnp.inf, F32)
    for h0 in range(0, SB_HEADS, SB_HEAD_BATCH):
        batch = range(h0, h0 + SB_HEAD_BATCH)
        cols = {h: slice(h * HEAD_DIM, (h + 1) * HEAD_DIM) for h in batch}
        z = {h: _dot_nt(q_ref[0, :, cols[h]], jnp.concatenate([r[0, :, cols[h]] for r in k_refs], axis=0))
             * (HEAD_DIM ** -0.5) for h in batch}
        ls = {h: -_softplus(z[h]) for h in batch}
        lf = {h: jnp.where(vis, ls[h], 0.0) for h in batch}
        lf_b = {h: lf[h].astype(BF16) for h in batch}
        within = {h: [] for h in batch}
        for h in batch:
            both = _dot(lf_b[h][:, 0:2 * SB_BLOCK], upair)
            within[h] += [both[:, 0:SB_BLOCK], both[:, SB_BLOCK:2 * SB_BLOCK]]
        for h in range(h0, h0 + SB_HEAD_BATCH, 2):
            both = _dot(jnp.concatenate([lf_b[h][:, 2 * SB_BLOCK:], lf_b[h + 1][:, 2 * SB_BLOCK:]], axis=1), upair)
            within[h].append(both[:, 0:SB_BLOCK])
            within[h + 1].append(both[:, SB_BLOCK:2 * SB_BLOCK])
        p = {}
        for h in batch:
            later = jnp.zeros((tq, 1), F32)
            shifted = []
            for w in range(SB_WINDOW):
                shifted.append(within[h][w] + later)
                later = later + jnp.sum(lf[h][:, w * SB_BLOCK:(w + 1) * SB_BLOCK], axis=-1, keepdims=True)
            worst = jnp.maximum(worst, later)
            p[h] = jnp.where(vis, jnp.exp((z[h] + ls[h]) + jnp.concatenate(shifted, axis=1)), 0.0).astype(BF16)
        for h in batch:
            acc = _dot(p[h], jnp.concatenate([r[0, :, cols[h]] for r in v_refs], axis=0))
            o_ref[0, :, cols[h]] = acc.astype(o_ref.dtype)
    has_older = base + i + older >= SB_WINDOW
    need_ref[...] = jnp.broadcast_to(jnp.where(has_older, jnp.max(worst), -jnp.inf), need_ref.shape)


def _stick_breaking_window(q, k_bf, v_bf, tq, base, older):
    b, lq, _ = q.shape
    nq = lq // tq
    qspec = pl.BlockSpec((1, tq, SB_WIDTH), lambda s, i: (s, i, 0))
    kspecs = [pl.BlockSpec((1, SB_BLOCK, SB_WIDTH), functools.partial(
        lambda s, i, w: (s, jnp.maximum(base + i - w, 0), 0), w=w)) for w in range(SB_WINDOW)]
    return pl.pallas_call(
        functools.partial(_sb_window_kernel, tq=tq, base=base, older=older),
        grid=(b, nq),
        in_specs=[qspec] + kspecs + kspecs,
        out_specs=[qspec, pl.BlockSpec((1, 1, SUBLANES, HEAD_DIM), lambda s, i: (s, i, 0, 0))],
        out_shape=[jax.ShapeDtypeStruct((b, lq, SB_WIDTH), BF16),
                   jax.ShapeDtypeStruct((b, nq, SUBLANES, HEAD_DIM), F32)],
        compiler_params=_cparams("parallel", "arbitrary"),
        name="stick_breaking_window",
    )(q, k_bf, k_bf, k_bf, v_bf, v_bf, v_bf)


def _layer(x, conv_past, s0, past_k, past_v, chunk, wts):
    b, l, d = x.shape
    m = b * l
    x2 = x.reshape(m, d)
    xn = _rmsnorm(x2, wts["norm1_w"], BF16)
    w_in = wts["w_in"]
    c_q = OFF_A
    c_g = c_q + 3 * SB_WIDTH
    c_ab = c_g + 2 * d
    (ab,) = _matmul(xn, wts["w_ab"], (F32,))
    if m <= 256:
        (proj,) = _matmul(xn, w_in, (F32,), 0, c_ab)
        proj_a, q_sb, gates = proj[:, :OFF_A], proj[:, c_q:c_q + SB_WIDTH].astype(BF16), proj[:, c_g:]
        k_sb, v_sb = proj[:, c_q + SB_WIDTH:c_q + 2 * SB_WIDTH], proj[:, c_q + 2 * SB_WIDTH:c_g]
        k_bf, v_bf = k_sb.astype(BF16), v_sb.astype(BF16)
    else:
        (proj_a,) = _matmul(xn, w_in, (F32,), 0, OFF_A)
        (gates,) = _matmul(xn, w_in, (F32,), c_g, 2 * d)
        v_sb, v_bf = _matmul(xn, w_in, (F32, BF16), c_q + 2 * SB_WIDTH, SB_WIDTH)
        k_sb, k_bf = _matmul(xn, w_in, (F32, BF16), c_q + SB_WIDTH, SB_WIDTH)
        (q_sb,) = _matmul(xn, w_in, (BF16,), c_q, SB_WIDTH)

    o_a, s_new = _gdn(proj_a.reshape(b, l, -1), ab.reshape(b, l, -1), conv_past, s0, wts["conv_w"], wts["A_log"],
                      wts["dt_bias"], wts["gdn_norm_w"], chunk)
    conv_state = proj_a.reshape(b, l, -1)[:, l - (CONV_WIDTH - 1):, :CONV_CH]

    k_bf = k_bf.reshape(b, l, SB_WIDTH)
    v_bf = v_bf.reshape(b, l, SB_WIDTH)
    q3 = q_sb.reshape(b, l, SB_WIDTH)
    if past_k is None:
        o_win, need = _stick_breaking_window(q3, k_bf, v_bf, min(l, SB_BLOCK), 0, 0)

        def full_sweep():
            return _stick_breaking(q3, k_bf, v_bf, 0, min(l, 256))
    else:
        p = past_k.shape[1]
        n_cached = (SB_WINDOW - 1) * SB_BLOCK
        assert p % SB_BLOCK == 0 and p >= n_cached and l <= SB_BLOCK

        def with_cache(new, past, keep):
            return jnp.concatenate([past[:, p - keep:].reshape(b, keep, SB_WIDTH).astype(BF16), new,
                                    jnp.zeros((b, SB_BLOCK - l, SB_WIDTH), BF16)], axis=1)

        o_win, need = _stick_breaking_window(q3, with_cache(k_bf, past_k, n_cached), with_cache(v_bf, past_v, n_cached),
                                             l, SB_WINDOW - 1, (p - n_cached) // SB_BLOCK)

        def full_sweep():
            return _stick_breaking(q3, with_cache(k_bf, past_k, p), with_cache(v_bf, past_v, p), p, l)
    o_b = lax.cond(jnp.max(need) > -SB_UNDERFLOW, full_sweep, lambda: o_win)

    merged = _merge(o_a.reshape(m, GDN_V_WIDTH), o_b.reshape(m, SB_WIDTH), wts["w_gdn_o"], wts["w_sb_o"], gates)
    h, h_gain, sumsq = _matmul_residual(merged, wts["w_out"], x2, wts["norm2_w"])
    hid = _matmul_relu2(h_gain, wts["w_up"], sumsq)
    h = _matmul_residual_ktiled(hid, wts["w_down"], h)
    return (h, conv_state, s_new, k_sb.reshape(b, l, SB_HEADS, HEAD_DIM), v_sb.reshape(b, l, SB_HEADS, HEAD_DIM))


def kernel(x_prompt, x_sample, cache_sb_k, cache_sb_v, state_gdn_S, state_gdn_conv, norm1_w, w_in, conv_w, A_log,
           dt_bias, gdn_norm_w, w_gdn_o, w_sb_o, w_out, norm2_w, w_up, w_down, final_norm_w):
    depth = w_in.shape[0]
    assert depth == 1
    w_in_t = jnp.swapaxes(w_in[0], 0, 1)
    wts = {
        "norm1_w": norm1_w[0], "norm2_w": norm2_w[0], "conv_w": conv_w[0], "A_log": A_log[0], "dt_bias": dt_bias[0],
        "gdn_norm_w": gdn_norm_w[0],
        "w_in": _repack_w_in(w_in_t), "w_ab": _ab_cols(w_in_t),
        "w_gdn_o": w_gdn_o[0].astype(BF16), "w_sb_o": w_sb_o[0].astype(BF16), "w_out": w_out[0].astype(BF16),
        "w_up": w_up[0].astype(BF16), "w_down": w_down[0].astype(BF16),
    }
    bp, lp, d = x_prompt.shape
    bs, ls, _ = x_sample.shape
    conv0 = jnp.zeros((bp, CONV_WIDTH - 1, CONV_CH), F32)
    s_zero = jnp.zeros((bp, GDN_V_HEADS, HEAD_DIM, HEAD_DIM), F32)
    hp, c_p, s_p, k_p, v_p = _layer(x_prompt, conv0, s_zero, None, None, min(PROMPT_CHUNK, lp), wts)
    hs, c_s, s_s, k_s, v_s = _layer(x_sample, state_gdn_conv[0], state_gdn_S[0], cache_sb_k[0], cache_sb_v[0], ls, wts)
    y_prompt = _rmsnorm(hp, final_norm_w, F32).reshape(bp, lp, d)
    y_sample = _rmsnorm(hs, final_norm_w, F32).reshape(bs, ls, d)
    return (y_prompt, y_sample, k_p[None], v_p[None], s_p[None], c_p[None],
            k_s[None], v_s[None], s_s[None].astype(state_gdn_S.dtype), c_s[None])
```
